```python
import math
import jax, jax.numpy as jnp
from jax import lax
import numpy as np

D_MODEL = 2048
BATCH = 1
SEQ = 8192
DEPTH = 2

D_MIX = D_MODEL
D_FF = 4 * D_MODEL
CONV_W = 4
CHUNK = 64
EPS = 1e-6
ROPE_BASE = 10000.0
RWKV_LN_EPS = 64e-5

GDN_HEADS = 4
GDN_DK = D_MIX // 16
GDN_DV = D_MIX // 16
GDN_QK = GDN_HEADS * GDN_DK
GDN_V = GDN_HEADS * GDN_DV
GDN_PW = 2 * GDN_QK + 2 * GDN_V + 2 * GDN_HEADS
RET_HEADS = 4
RET_DV = D_MIX // 16
RET_DK = RET_DV // 2
RET_QK = RET_HEADS * RET_DK
RET_V = RET_HEADS * RET_DV
RET_PW = 2 * RET_QK + 2 * RET_V
M2_HEADS = 8
M2_HEADDIM = D_MIX // 32
M2_GROUPS = 2
M2_STATE = 128
M2_W = M2_HEADS * M2_HEADDIM
M2_BC = M2_GROUPS * M2_STATE
M2_PW = 2 * M2_W + 2 * M2_BC + M2_HEADS
RW_HEADS = 8
RW_N = D_MIX // 32
RW_W = RW_HEADS * RW_N
RW_W_LORA = 32
RW_A_LORA = 32
RW_G_LORA = 96
RW_PW = 3 * RW_W + RW_W_LORA + RW_A_LORA + RW_G_LORA

P_TOTAL = GDN_PW + RET_PW + M2_PW + RW_PW

kernel_name = 'hybrid_parallel_heads_block'


def _offsets(widths):
    return [int(o) for o in np.cumsum(widths)]


def _rmsnorm(x, w, eps=EPS):
    xf = x.astype(jnp.float32)
    y = xf * lax.rsqrt(jnp.mean(xf * xf, axis=-1, keepdims=True) + eps)
    return (y * w.astype(jnp.float32)).astype(x.dtype)


def _layernorm(x, w, eps):
    mu = jnp.mean(x, axis=-1, keepdims=True)
    xc = x - mu
    return xc * lax.rsqrt(jnp.mean(xc * xc, axis=-1, keepdims=True) + eps) * w


def _l2norm(x):
    return x * lax.rsqrt(jnp.sum(x * x, axis=-1, keepdims=True) + EPS)


def _causal_dwconv(x, w):
    T = x.shape[1]
    xp = jnp.pad(x, ((0, 0), (CONV_W - 1, 0), (0, 0)))
    return sum(xp[:, i:i + T] * w[i] for i in range(CONV_W))


def _to_chunks(t):
    Bsz, T = t.shape[:2]
    t = t.reshape(Bsz, T // CHUNK, CHUNK, *t.shape[2:])
    return jnp.moveaxis(t, 3, 1)


def _from_chunks(t):
    t = jnp.moveaxis(t, 1, 3)
    return t.reshape(t.shape[0], t.shape[1] * t.shape[2], *t.shape[3:])


def _rotary(x):
    T, D = x.shape[1], x.shape[-1]
    theta = 1.0 / (ROPE_BASE ** jnp.linspace(0.0, 1.0, D // 2, dtype=jnp.float32))
    ang = jnp.arange(T, dtype=jnp.float32)[:, None] * theta
    cos = jnp.cos(ang)[None, :, None, :]
    sin = jnp.sin(ang)[None, :, None, :]
    xp = x.reshape(*x.shape[:-1], D // 2, 2)
    x0, x1 = xp[..., 0], xp[..., 1]
    return jnp.stack([x0 * cos - x1 * sin, x1 * cos + x0 * sin], axis=-1).reshape(x.shape)


def _gated_deltanet(q, k, v, beta, log_a):
    Bsz, T, H, DK = q.shape
    DV = v.shape[-1]
    qc, kc, vc = _to_chunks(q), _to_chunks(k), _to_chunks(v)
    bc = _to_chunks(beta)
    g = jnp.cumsum(_to_chunks(log_a), axis=-1)
    idx = jnp.arange(CHUNK)
    causal = idx[:, None] >= idx[None, :]
    strict = idx[:, None] > idx[None, :]
    gamma = jnp.exp(jnp.where(causal, g[..., :, None] - g[..., None, :], -jnp.inf))
    kkt = jnp.einsum('bhncd,bhnmd->bhncm', kc, kc)
    lower = jnp.where(strict, kkt * gamma * bc[..., :, None], 0.0)
    m = lower + jnp.eye(CHUNK, dtype=lower.dtype)
    rhs = jnp.concatenate([vc * bc[..., None], kc * (bc * jnp.exp(g))[..., None]], axis=-1)
    sol = lax.linalg.triangular_solve(m, rhs, left_side=True, lower=True, unit_diagonal=True)
    u, w = sol[..., :DV], sol[..., DV:]
    qk = jnp.einsum('bhncd,bhnmd->bhncm', qc, kc) * gamma
    q_dec = qc * jnp.exp(g)[..., None]
    g_last = g[..., -1]
    k_tail = kc * jnp.exp(g_last[..., None] - g)[..., None]

    def step(S, inp):
        w_i, u_i, qk_i, qd_i, kt_i, gl_i = inp
        v_new = u_i - jnp.einsum('bhcd,bhde->bhce', w_i, S)
        o = jnp.einsum('bhcd,bhde->bhce', qd_i, S) + jnp.einsum('bhcm,bhme->bhce', qk_i, v_new)
        S = S * jnp.exp(gl_i)[..., None, None] + jnp.einsum('bhcd,bhce->bhde', kt_i, v_new)
        return S, o

    xs = tuple(jnp.moveaxis(t, 2, 0) for t in (w, u, qk, q_dec, k_tail, g_last))
    S0 = jnp.zeros((Bsz, H, DK, DV), jnp.float32)
    _, o = lax.scan(step, S0, xs)
    return _from_chunks(jnp.moveaxis(o, 0, 2))


def _retention(q, k, v):
    Bsz, T, H, DK = q.shape
    DV = v.shape[-1]
    lg = jnp.log1p(-jnp.exp2(-5.0 - jnp.arange(H, dtype=jnp.float32)))
    pos = jnp.arange(CHUNK, dtype=jnp.float32)
    causal = pos[:, None] >= pos[None, :]
    dmask = jnp.exp(jnp.where(causal, (pos[:, None] - pos[None, :]) * lg[:, None, None], -jnp.inf))
    qc, kc, vc = _to_chunks(q), _to_chunks(k), _to_chunks(v)
    scores = jnp.einsum('bhncd,bhnmd->bhncm', qc, kc) * dmask[:, None]
    o = jnp.einsum('bhncm,bhnme->bhnce', scores, vc)
    k_dec = jnp.exp((CHUNK - 1 - pos) * lg[:, None])
    states = jnp.einsum('bhncd,bhnce->bhnde', kc * k_dec[:, None, :, None], vc)
    chunk_decay = jnp.exp(CHUNK * lg)

    def step(R, s):
        return R * chunk_decay[:, None, None] + s, R

    _, r_prev = lax.scan(step, jnp.zeros((Bsz, H, DK, DV), jnp.float32), jnp.moveaxis(states, 2, 0))
    r_prev = jnp.moveaxis(r_prev, 0, 2)
    q_dec = jnp.exp((pos + 1.0) * lg[:, None])
    o = o + jnp.einsum('bhncd,bhnde->bhnce', qc * q_dec[:, None, :, None], r_prev)
    return _from_chunks(o)


def _ssd(x, dt, A, Bm, Cm):
    Bsz, T, H, P = x.shape
    G, N = Bm.shape[2], Bm.shape[3]
    Hg = H // G
    nc = T // CHUNK

    def grp(t):
        t = t.reshape(Bsz, nc, CHUNK, G, Hg, *t.shape[3:])
        return jnp.moveaxis(t, (3, 4), (1, 2))

    xc, dtc = grp(x), grp(dt)
    g = jnp.cumsum(grp(dt * A), axis=-1)
    Bc = jnp.moveaxis(Bm.reshape(Bsz, nc, CHUNK, G, N), 3, 1)
    Cc = jnp.moveaxis(Cm.reshape(Bsz, nc, CHUNK, G, N), 3, 1)
    idx = jnp.arange(CHUNK)
    causal = idx[:, None] >= idx[None, :]
    L = jnp.exp(jnp.where(causal, g[..., :, None] - g[..., None, :], -jnp.inf))
    cb = jnp.einsum('bgncs,bgnms->bgncm', Cc, Bc)
    y = jnp.einsum('bghncm,bghnmp->bghncp', cb[:, :, None] * L * dtc[..., None, :], xc)
    g_last = g[..., -1]
    states = jnp.einsum('bgncs,bghncp->bghnsp', Bc, xc * (jnp.exp(g_last[..., None] - g) * dtc)[..., None])

    def step(hs, inp):
        s, gl = inp
        return hs * jnp.exp(gl)[..., None, None] + s, hs

    _, h_prev = lax.scan(step, jnp.zeros((Bsz, G, Hg, N, P), jnp.float32),
                         (jnp.moveaxis(states, 3, 0), jnp.moveaxis(g_last, 3, 0)))
    h_prev = jnp.moveaxis(h_prev, 0, 3)
    y = y + jnp.einsum('bgncs,bghnsp->bghncp', Cc, h_prev) * jnp.exp(g)[..., None]
    return jnp.moveaxis(y, (1, 2), (3, 4)).reshape(Bsz, T, H, P)


def _rwkv7_scan(r, decay, k, v, kk, a):
    Bsz, T, H, N = r.shape

    def step(S, inp):
        r_t, d_t, k_t, v_t, kk_t, a_t = inp
        sa = jnp.einsum('bhvk,bhk->bhv', S, -kk_t)
        S = S * d_t[:, :, None, :] + sa[..., None] * (kk_t * a_t)[:, :, None, :] + v_t[..., None] * k_t[:, :, None, :]
        return S, jnp.einsum('bhvk,bhk->bhv', S, r_t)

    xs = tuple(jnp.moveaxis(t, 1, 0) for t in (r, decay, k, v, kk, a))
    _, y = lax.scan(step, jnp.zeros((Bsz, H, N, N), jnp.float32), xs)
    return jnp.moveaxis(y, 0, 1)


def _token_mix(h, w_in, gdn_conv_w, gdn_a_log, gdn_dt_bias, gdn_norm_w, ret_norm_w,
               m2_conv_w, m2_conv_b, m2_a_log, m2_dt_bias, m2_d, m2_norm_w,
               rw_mu, rw_w0, rw_w_up, rw_a0, rw_a_up, rw_g_up, rw_k_k, rw_k_a, rw_r_k,
               rw_ln_w, rw_ln_b, w_out):
    f32 = jnp.float32
    Bsz, T, _ = h.shape
    proj = jnp.einsum('btd,dp->btp', h, w_in).astype(f32)
    p_gdn, p_ret, p_m2, p_rw = jnp.split(proj, _offsets((GDN_PW, RET_PW, M2_PW)), axis=-1)

    gqkv, gz, gb, ga = jnp.split(p_gdn, _offsets((2 * GDN_QK + GDN_V, GDN_V, GDN_HEADS)), axis=-1)
    gqkv = jax.nn.silu(_causal_dwconv(gqkv, gdn_conv_w.astype(f32)))
    gq, gk, gv = jnp.split(gqkv, _offsets((GDN_QK, GDN_QK)), axis=-1)
    q = _l2norm(gq.reshape(Bsz, T, GDN_HEADS, GDN_DK)) * (GDN_DK ** -0.5)
    k = _l2norm(gk.reshape(Bsz, T, GDN_HEADS, GDN_DK))
    v = gv.reshape(Bsz, T, GDN_HEADS, GDN_DV)
    beta = jax.nn.sigmoid(gb)
    log_a = -jnp.exp(gdn_a_log.astype(f32)) * jax.nn.softplus(ga + gdn_dt_bias.astype(f32))
    o_a = _gated_deltanet(q, k, v, beta, log_a)
    o_a = _rmsnorm(o_a, gdn_norm_w) * jax.nn.silu(gz.reshape(Bsz, T, GDN_HEADS, GDN_DV))
    o_a = o_a.reshape(Bsz, T, GDN_V)

    rq, rk, rv, rg = jnp.split(p_ret, _offsets((RET_QK, RET_QK, RET_V)), axis=-1)
    q = _rotary(rq.reshape(Bsz, T, RET_HEADS, RET_DK))
    k = _rotary(rk.reshape(Bsz, T, RET_HEADS, RET_DK)) * (RET_DK ** -0.5)
    v = rv.reshape(Bsz, T, RET_HEADS, RET_DV)
    o_b = _retention(q, k, v)
    o_b = _layernorm(o_b, ret_norm_w.astype(f32).reshape(RET_HEADS, RET_DV), EPS)
    o_b = o_b.reshape(Bsz, T, RET_V) * jax.nn.silu(rg)

    mz, mxbc, mdt = jnp.split(p_m2, _offsets((M2_W, M2_W + 2 * M2_BC)), axis=-1)
    mxbc = jax.nn.silu(_causal_dwconv(mxbc, m2_conv_w.astype(f32)) + m2_conv_b.astype(f32))
    mx, mB, mC = jnp.split(mxbc, _offsets((M2_W, M2_BC)), axis=-1)
    xh = mx.reshape(Bsz, T, M2_HEADS, M2_HEADDIM)
    dt = jax.nn.softplus(mdt + m2_dt_bias.astype(f32))
    A = -jnp.exp(m2_a_log.astype(f32))
    y = _ssd(xh, dt, A, mB.reshape(Bsz, T, M2_GROUPS, M2_STATE), mC.reshape(Bsz, T, M2_GROUPS, M2_STATE))
    y = y + m2_d.astype(f32)[:, None] * xh
    y = y.reshape(Bsz, T, M2_W) * jax.nn.silu(mz)
    o_c = _rmsnorm(y.reshape(Bsz, T, M2_GROUPS, M2_W // M2_GROUPS),
                   m2_norm_w.reshape(M2_GROUPS, M2_W // M2_GROUPS)).reshape(Bsz, T, M2_W)

    p_prev = jnp.pad(p_rw, ((0, 0), (1, 0), (0, 0)))[:, :-1]
    mixed = p_rw + (p_prev - p_rw) * rw_mu.astype(f32)
    wr, wk, wv, wl, al, gl = jnp.split(mixed, _offsets((RW_W, RW_W, RW_W, RW_W_LORA, RW_A_LORA)), axis=-1)
    w_raw = -jax.nn.softplus(-(rw_w0.astype(f32) + jnp.tanh(wl) @ rw_w_up.astype(f32))) - 0.5
    decay = jnp.exp(-jnp.exp(w_raw))
    a = jax.nn.sigmoid(rw_a0.astype(f32) + al @ rw_a_up.astype(f32))
    g = jax.nn.sigmoid(gl) @ rw_g_up.astype(f32)
    hd = lambda t: t.reshape(Bsz, T, RW_HEADS, RW_N)
    r, k, v, a, decay = hd(wr), hd(wk), hd(wv), hd(a), hd(decay)
    kk = _l2norm(k * rw_k_k.astype(f32).reshape(RW_HEADS, RW_N))
    k = k * (1.0 + (a - 1.0) * rw_k_a.astype(f32).reshape(RW_HEADS, RW_N))
    y = _rwkv7_scan(r, decay, k, v, kk, a)
    y = _layernorm(y, rw_ln_w.astype(f32).reshape(RW_HEADS, RW_N), RWKV_LN_EPS) + rw_ln_b.astype(f32).reshape(RW_HEADS, RW_N)
    y = y + jnp.sum(r * k * rw_r_k.astype(f32), axis=-1, keepdims=True) * v
    o_d = y.reshape(Bsz, T, RW_W) * g

    o = jnp.concatenate([o_a, o_b, o_c, o_d], axis=-1)
    return jnp.einsum('btm,md->btd', o, w_out.astype(f32)).astype(h.dtype)


def _sq_relu_mlp(h, w_up, w_down):
    return jnp.square(jax.nn.relu(h @ w_up)) @ w_down


def setup_inputs(seed: int = 0) -> dict:
    key = jax.random.key(seed)
    ks = iter(jax.random.split(key, 32))
    L = DEPTH

    def nrm(shape, scale):
        return jax.random.normal(next(ks), shape, jnp.float32) * scale

    def unif(shape, lo, hi):
        return jax.random.uniform(next(ks), shape, jnp.float32, lo, hi)

    def dt_bias(shape):
        dt = jnp.exp(unif(shape, math.log(1e-3), math.log(1e-1)))
        return dt + jnp.log(-jnp.expm1(-dt))

    return {
        'x': nrm((BATCH, SEQ, D_MODEL), 1.0),
        'norm1_w': 1.0 + nrm((L, D_MODEL), 0.02),
        'w_in': nrm((L, D_MODEL, P_TOTAL), D_MODEL ** -0.5),
        'gdn_conv_w': nrm((L, CONV_W, 2 * GDN_QK + GDN_V), CONV_W ** -0.5),
        'gdn_a_log': jnp.log(unif((L, GDN_HEADS), 1.0, 16.0)),
        'gdn_dt_bias': dt_bias((L, GDN_HEADS)),
        'gdn_norm_w': 1.0 + nrm((L, GDN_DV), 0.02),
        'ret_norm_w': 1.0 + nrm((L, RET_V), 0.02),
        'm2_conv_w': nrm((L, CONV_W, M2_W + 2 * M2_BC), CONV_W ** -0.5),
        'm2_conv_b': nrm((L, M2_W + 2 * M2_BC), 0.1),
        'm2_a_log': jnp.log(unif((L, M2_HEADS), 1.0, 16.0)),
        'm2_dt_bias': dt_bias((L, M2_HEADS)),
        'm2_d': 1.0 + nrm((L, M2_HEADS), 0.1),
        'm2_norm_w': 1.0 + nrm((L, M2_W), 0.02),
        'rw_mu': unif((L, RW_PW), 0.0, 1.0),
        'rw_w0': unif((L, RW_W), -5.0, 1.0),
        'rw_w_up': nrm((L, RW_W_LORA, RW_W), 0.1),
        'rw_a0': nrm((L, RW_W), 0.1),
        'rw_a_up': nrm((L, RW_A_LORA, RW_W), 0.1),
        'rw_g_up': nrm((L, RW_G_LORA, RW_W), RW_G_LORA ** -0.5),
        'rw_k_k': 0.85 + nrm((L, RW_W), 0.05),
        'rw_k_a': 1.0 + nrm((L, RW_W), 0.05),
        'rw_r_k': nrm((L, RW_HEADS, RW_N), 0.1),
        'rw_ln_w': 1.0 + nrm((L, RW_W), 0.02),
        'rw_ln_b': nrm((L, RW_W), 0.02),
        'w_out': nrm((L, D_MIX, D_MODEL), D_MIX ** -0.5),
        'norm2_w': 1.0 + nrm((L, D_MODEL), 0.02),
        'w_ffn_up': nrm((L, D_MODEL, D_FF), D_MODEL ** -0.5),
        'w_ffn_down': nrm((L, D_FF, D_MODEL), D_FF ** -0.5),
        'final_norm_w': 1.0 + nrm((D_MODEL,), 0.02),
    }


def reference(x, norm1_w, w_in, gdn_conv_w, gdn_a_log, gdn_dt_bias, gdn_norm_w, ret_norm_w,
              m2_conv_w, m2_conv_b, m2_a_log, m2_dt_bias, m2_d, m2_norm_w,
              rw_mu, rw_w0, rw_w_up, rw_a0, rw_a_up, rw_g_up, rw_k_k, rw_k_a, rw_r_k,
              rw_ln_w, rw_ln_b, w_out, norm2_w, w_ffn_up, w_ffn_down, final_norm_w):
    h = x
    for l in range(DEPTH):
        a = _rmsnorm(h, norm1_w[l])
        h = h + _token_mix(a, w_in[l], gdn_conv_w[l], gdn_a_log[l], gdn_dt_bias[l], gdn_norm_w[l], ret_norm_w[l],
                           m2_conv_w[l], m2_conv_b[l], m2_a_log[l], m2_dt_bias[l], m2_d[l], m2_norm_w[l],
                           rw_mu[l], rw_w0[l], rw_w_up[l], rw_a0[l], rw_a_up[l], rw_g_up[l], rw_k_k[l], rw_k_a[l],
                           rw_r_k[l], rw_ln_w[l], rw_ln_b[l], w_out[l])
        m = _rmsnorm(h, norm2_w[l])
        h = h + _sq_relu_mlp(m, w_ffn_up[l], w_ffn_down[l])
    return _rmsnorm(h, final_norm_w)
```

```python
import functools
import math

import numpy as np
import jax
import jax.numpy as jnp
from jax import lax
from jax.experimental import pallas as pl
from jax.experimental.pallas import tpu as pltpu

F32 = jnp.float32
BF16 = jnp.bfloat16
HI = lax.Precision.HIGHEST

D_MODEL = 2048
D_FF = 4 * D_MODEL
CONV_W = 4
CHUNK = 64
EPS = 1e-6
ROPE_BASE = 10000.0
RWKV_LN_EPS = 64e-5

GDN_HEADS, GDN_DK, GDN_DV = 4, 128, 128
RET_HEADS, RET_DK, RET_DV = 4, 64, 128
M2_HEADS, M2_HEADDIM, M2_GROUPS, M2_STATE = 8, 64, 2, 128
RW_HEADS, RW_N = 8, 64
RW_W_LORA, RW_A_LORA, RW_G_LORA = 32, 32, 96
MIX_W = 512

_GDN0, _RET0, _M20, _RW0 = 0, 2056, 3592, 5136
P_TOTAL = 6832

P_PAD = 7168
_SRC_PIECES = (
    (_GDN0, 1536, 1536),
    (_RW0, 1536, 1536),
    (_M20 + 512, 1024, 1024),
    (_GDN0 + 1536, 512, 512),
    (_RET0, 512, 512),
    (_RET0 + 512, 512, 512),
    (_RET0 + 1024, 512, 512),
    (_M20, 512, 512),
    (_GDN0 + 2048, 8, 128),
    (_M20 + 1536, 8, 128),
    (_RW0 + 1536, 160, 256),
)

VMEM_LIMIT = 56 * 1024 * 1024


def _cparams(n_axes):
    return pltpu.CompilerParams(dimension_semantics=("arbitrary",) * n_axes,
                                vmem_limit_bytes=VMEM_LIMIT)


def _mm(a, b, prec=None):
    return jnp.dot(a, b, preferred_element_type=F32, precision=prec)


def _mm_nt(a, b, prec=None):
    return lax.dot_general(a, b, (((1,), (1,)), ((), ())), preferred_element_type=F32, precision=prec)


def _mm_tn(a, b, prec=None):
    return lax.dot_general(a, b, (((0,), (0,)), ((), ())), preferred_element_type=F32, precision=prec)


def _sigmoid(x):
    return 1.0 / (1.0 + jnp.exp(-x))


def _silu(x):
    return x * _sigmoid(x)


def _softplus(x):
    return jnp.maximum(x, 0.0) + jnp.log1p(jnp.exp(-jnp.abs(x)))


def _iota2(shape, axis):
    return lax.broadcasted_iota(jnp.int32, shape, axis)


def _chunk_masks():
    r = _iota2((CHUNK, CHUNK), 0)
    c = _iota2((CHUNK, CHUNK), 1)
    return r >= c, r > c, r == c


def _inv_unit_lower(low, eye):
    p = eye - low
    lp = low
    n = 2
    while n < CHUNK:
        lp = _mm(lp, lp, HI)
        p = p + _mm(p, lp, HI)
        n *= 2
    return p


def _norm_matmul_kernel(x_ref, nw_ref, w_ref, o_ref, a_ref):
    @pl.when(pl.program_id(1) == 0)
    def _():
        x = x_ref[...]
        y = x * lax.rsqrt(jnp.mean(x * x, axis=-1, keepdims=True) + EPS) * nw_ref[...]
        a_ref[...] = y.astype(BF16)

    o_ref[...] = jnp.dot(a_ref[...], w_ref[...], preferred_element_type=F32)


def _norm_matmul(x, nw, w, tm=1024, tn=1024):
    t, d = x.shape
    n = w.shape[1]
    tm = min(tm, t)
    return pl.pallas_call(
        _norm_matmul_kernel,
        out_shape=jax.ShapeDtypeStruct((t, n), F32),
        grid=(t // tm, n // tn),
        in_specs=[pl.BlockSpec((tm, d), lambda i, j: (i, 0)),
                  pl.BlockSpec((1, d), lambda i, j: (0, 0)),
                  pl.BlockSpec((d, tn), lambda i, j: (0, j))],
        out_specs=pl.BlockSpec((tm, tn), lambda i, j: (i, j)),
        scratch_shapes=[pltpu.VMEM((tm, d), BF16)],
        compiler_params=_cparams(2),
    )(x, nw.reshape(1, d), w)


def _out_proj_kernel(h_ref, oa_ref, ob_ref, oc_ref, od_ref, w_ref, o_ref):
    acc = h_ref[...]
    for idx, part in enumerate((oa_ref, ob_ref, oc_ref, od_ref)):
        acc = acc + jnp.dot(part[...].astype(BF16), w_ref[idx * MIX_W:(idx + 1) * MIX_W, :],
                            preferred_element_type=F32)
    o_ref[...] = acc


def _out_proj(h, parts, w, tm=256):
    t, d = h.shape
    tm = min(tm, t)
    part_spec = pl.BlockSpec((tm, MIX_W), lambda i: (i, 0))
    return pl.pallas_call(
        _out_proj_kernel,
        out_shape=jax.ShapeDtypeStruct((t, d), F32),
        grid=(t // tm,),
        in_specs=[pl.BlockSpec((tm, d), lambda i: (i, 0)),
                  part_spec, part_spec, part_spec, part_spec,
                  pl.BlockSpec((4 * MIX_W, d), lambda i: (0, 0))],
        out_specs=pl.BlockSpec((tm, d), lambda i: (i, 0)),
        compiler_params=_cparams(1),
    )(h, *parts, w)


def _ffn_kernel(h_ref, nw_ref, wu_ref, wd_ref, fw_ref, o_ref, a_ref, acc_ref, *, final_norm):
    f = pl.program_id(1)

    @pl.when(f == 0)
    def _():
        x = h_ref[...]
        y = x * lax.rsqrt(jnp.mean(x * x, axis=-1, keepdims=True) + EPS) * nw_ref[...]
        a_ref[...] = y.astype(BF16)
        acc_ref[...] = jnp.zeros_like(acc_ref)

    u = jnp.dot(a_ref[...], wu_ref[...], preferred_element_type=F32)
    s = jnp.square(jnp.maximum(u, 0.0)).astype(BF16)
    acc_ref[...] += jnp.dot(s, wd_ref[...], preferred_element_type=F32)

    @pl.when(f == pl.num_programs(1) - 1)
    def _():
        y = h_ref[...] + acc_ref[...]
        if final_norm:
            y = y * lax.rsqrt(jnp.mean(y * y, axis=-1, keepdims=True) + EPS) * fw_ref[...]
        o_ref[...] = y


def _ffn(h, nw, wu, wd, fw, final_norm, tm=512, tf=512):
    t, d = h.shape
    ff = wu.shape[1]
    tm = min(tm, t)
    return pl.pallas_call(
        functools.partial(_ffn_kernel, final_norm=final_norm),
        out_shape=jax.ShapeDtypeStruct((t, d), F32),
        grid=(t // tm, ff // tf),
        in_specs=[pl.BlockSpec((tm, d), lambda i, f: (i, 0)),
                  pl.BlockSpec((1, d), lambda i, f: (0, 0)),
                  pl.BlockSpec((d, tf), lambda i, f: (0, f)),
                  pl.BlockSpec((tf, d), lambda i, f: (f, 0)),
                  pl.BlockSpec((1, d), lambda i, f: (0, 0))],
        out_specs=pl.BlockSpec((tm, d), lambda i, f: (i, 0)),
        scratch_shapes=[pltpu.VMEM((tm, d), BF16), pltpu.VMEM((tm, d), F32)],
        compiler_params=_cparams(2),
    )(h, nw.reshape(1, d), wu, wd, fw.reshape(1, d))


def _stage_with_halo(ext_ref, pieces):
    @pl.when(pl.program_id(0) == 0)
    def _():
        ext_ref[0:8, :] = jnp.zeros((8, ext_ref.shape[1]), F32)

    col = 0
    for ref in pieces:
        w = ref.shape[1]
        ext_ref[8:8 + CHUNK, col:col + w] = ref[...]
        col += w


def _keep_halo(ext_ref):
    ext_ref[0:8, :] = ext_ref[CHUNK:CHUNK + 8, :]


def _causal_conv(ext_ref, cw):
    acc = ext_ref[pl.ds(8, CHUNK), :] * cw[3:4, :]
    for i in range(CONV_W - 1):
        acc = acc + ext_ref[pl.ds(5 + i, CHUNK), :] * cw[i:i + 1, :]
    return acc


def _gdn_kernel(qkv_ref, z_ref, gate_ref, cw_ref, alog_ref, dtb_ref, nw_ref, o_ref, ext_ref, s_ref):
    @pl.when(pl.program_id(0) == 0)
    def _():
        s_ref[...] = jnp.zeros_like(s_ref)

    _stage_with_halo(ext_ref, (qkv_ref,))
    qkv = _silu(_causal_conv(ext_ref, cw_ref[...]))
    _keep_halo(ext_ref)

    causal, strict, diag = _chunk_masks()
    eye = jnp.where(diag, 1.0, 0.0).astype(F32)
    tri = jnp.where(causal, 1.0, 0.0).astype(F32)

    gt = gate_ref[...]
    beta = _sigmoid(gt)
    log_a = -jnp.exp(alog_ref[...]) * _softplus(gt + dtb_ref[...])
    g = _mm(tri, log_a, HI)
    g_t = _mm_tn(g, eye, HI)
    z = z_ref[...]
    nw = nw_ref[...]

    for h in range(GDN_HEADS):
        sl = slice(h * GDN_DK, (h + 1) * GDN_DK)
        qh = qkv[:, h * GDN_DK:(h + 1) * GDN_DK]
        kh = qkv[:, 512 + h * GDN_DK:512 + (h + 1) * GDN_DK]
        vh = qkv[:, 1024 + h * GDN_DV:1024 + (h + 1) * GDN_DV]
        qh = qh * lax.rsqrt(jnp.sum(qh * qh, axis=-1, keepdims=True) + EPS) * (GDN_DK ** -0.5)
        kh = kh * lax.rsqrt(jnp.sum(kh * kh, axis=-1, keepdims=True) + EPS)
        bcol = beta[:, h:h + 1]
        gcol = g[:, 4 + h:5 + h]
        grow = g_t[4 + h:5 + h, :]
        gamma = jnp.exp(jnp.where(causal, gcol - grow, -jnp.inf))
        low = jnp.where(strict, _mm_nt(kh, kh, HI) * gamma * bcol, 0.0)
        tinv = _inv_unit_lower(low, eye)
        egc = jnp.exp(gcol)
        u = _mm(tinv, vh * bcol, HI)
        w = _mm(tinv, kh * (bcol * egc), HI)
        qk = _mm_nt(qh, kh, HI) * gamma
        s0 = s_ref[h]
        v_new = u - _mm(w, s0, HI)
        o = _mm(qh * egc, s0, HI) + _mm(qk, v_new, HI)
        glast = gcol[CHUNK - 1:CHUNK, :]
        s_ref[h] = s0 * jnp.exp(glast) + _mm_tn(kh * jnp.exp(glast - gcol), v_new, HI)
        o = o * lax.rsqrt(jnp.mean(o * o, axis=-1, keepdims=True) + EPS) * nw
        o_ref[:, sl] = o * _silu(z[:, sl])


def _gdn(proj, cw, alog, dtb, nw):
    t = proj.shape[0]
    row = lambda n: pl.BlockSpec((1, n), lambda i: (0, 0))
    return pl.pallas_call(
        _gdn_kernel,
        out_shape=jax.ShapeDtypeStruct((t, MIX_W), F32),
        grid=(t // CHUNK,),
        in_specs=[pl.BlockSpec((CHUNK, 1536), lambda i: (i, 0)),
                  pl.BlockSpec((CHUNK, 512), lambda i: (i, 8)),
                  pl.BlockSpec((CHUNK, 128), lambda i: (i, 52)),
                  pl.BlockSpec((CONV_W, 1536), lambda i: (0, 0)),
                  row(128), row(128), row(128)],
        out_specs=pl.BlockSpec((CHUNK, MIX_W), lambda i: (i, 0)),
        scratch_shapes=[pltpu.VMEM((CHUNK + 8, 1536), F32),
                        pltpu.VMEM((GDN_HEADS, GDN_DK, GDN_DV), F32)],
        compiler_params=_cparams(1),
    )(proj, proj, proj, cw, alog, dtb, nw)


def _ret_kernel(qk_ref, v_ref, g_ref, cos_ref, sin_ref, nw_ref, o_ref, r_ref):
    @pl.when(pl.program_id(0) == 0)
    def _():
        r_ref[...] = jnp.zeros_like(r_ref)

    qk = qk_ref[...]
    cos = cos_ref[...]
    sin = sin_ref[...]
    even = (_iota2((CHUNK, 256), 1) % 2) == 0

    def rot(x):
        partner = jnp.where(even, pltpu.roll(x, 255, 1), pltpu.roll(x, 1, 1))
        return x * cos + partner * sin

    q = rot(qk[:, 0:256])
    k = rot(qk[:, 256:512]) * (RET_DK ** -0.5)
    v = v_ref[...]
    gate = g_ref[...]
    nw = nw_ref[...]

    causal, _, _ = _chunk_masks()
    dist = (_iota2((CHUNK, CHUNK), 0) - _iota2((CHUNK, CHUNK), 1)).astype(F32)
    pos = _iota2((CHUNK, 1), 0).astype(F32)

    for h in range(RET_HEADS):
        lg = math.log1p(-(2.0 ** (-5.0 - h)))
        dmask = jnp.exp(jnp.where(causal, dist * lg, -jnp.inf))
        k_dec = jnp.exp((CHUNK - 1.0 - pos) * lg)
        q_dec = jnp.exp((pos + 1.0) * lg)
        qh = q[:, h * RET_DK:(h + 1) * RET_DK]
        kh = k[:, h * RET_DK:(h + 1) * RET_DK]
        sl = slice(h * RET_DV, (h + 1) * RET_DV)
        vh = v[:, sl]
        r0 = r_ref[h]
        o = _mm(_mm_nt(qh, kh, HI) * dmask, vh, HI) + _mm(qh * q_dec, r0, HI)
        r_ref[h] = r0 * math.exp(CHUNK * lg) + _mm_tn(kh * k_dec, vh, HI)
        mu = jnp.mean(o, axis=-1, keepdims=True)
        oc = o - mu
        o = oc * lax.rsqrt(jnp.mean(oc * oc, axis=-1, keepdims=True) + EPS) * nw[:, sl]
        o_ref[:, sl] = o * _silu(gate[:, sl])


def _retention(proj, cos, sin, nw):
    t = proj.shape[0]
    return pl.pallas_call(
        _ret_kernel,
        out_shape=jax.ShapeDtypeStruct((t, MIX_W), F32),
        grid=(t // CHUNK,),
        in_specs=[pl.BlockSpec((CHUNK, 512), lambda i: (i, 9)),
                  pl.BlockSpec((CHUNK, 512), lambda i: (i, 10)),
                  pl.BlockSpec((CHUNK, 512), lambda i: (i, 11)),
                  pl.BlockSpec((CHUNK, 256), lambda i: (i, 0)),
                  pl.BlockSpec((CHUNK, 256), lambda i: (i, 0)),
                  pl.BlockSpec((1, 512), lambda i: (0, 0))],
        out_specs=pl.BlockSpec((CHUNK, MIX_W), lambda i: (i, 0)),
        scratch_shapes=[pltpu.VMEM((RET_HEADS, RET_DK, RET_DV), F32)],
        compiler_params=_cparams(1),
    )(proj, proj, proj, cos, sin, nw)


def _head_expand(n_heads, width):
    r = _iota2((128, n_heads * width), 0)
    c = _iota2((128, n_heads * width), 1)
    return jnp.where(r * width == c - c % width, 1.0, 0.0).astype(F32)


def _ssd_kernel(z_ref, xbc_ref, dt_ref, cw_ref, cb_ref, alog_ref, dtb_ref, d_ref, nw_ref, o_ref,
                ext_ref, h_ref):
    @pl.when(pl.program_id(0) == 0)
    def _():
        h_ref[...] = jnp.zeros_like(h_ref)

    _stage_with_halo(ext_ref, (xbc_ref,))
    xbc = _silu(_causal_conv(ext_ref, cw_ref[...]) + cb_ref[...])
    _keep_halo(ext_ref)
    x = xbc[:, 0:512]

    causal, _, diag = _chunk_masks()
    eye = jnp.where(diag, 1.0, 0.0).astype(F32)
    tri = jnp.where(causal, 1.0, 0.0).astype(F32)
    expand = _head_expand(M2_HEADS, M2_HEADDIM)

    dt = _softplus(dt_ref[...] + dtb_ref[...])
    g = _mm(tri, dt * (-jnp.exp(alog_ref[...])), HI)
    g_t = _mm_tn(g, eye, HI)
    g_w = _mm(g, expand, HI)
    dt_w = _mm(dt, expand, HI)
    glast_w = g_w[CHUNK - 1:CHUNK, :]
    xdt = x * dt_w
    xtail = xdt * jnp.exp(glast_w - g_w)
    eg_w = jnp.exp(g_w)
    z = z_ref[...]
    d_w = d_ref[...]
    nw = nw_ref[...]
    gw = M2_HEADS // M2_GROUPS * M2_HEADDIM

    for gi in range(M2_GROUPS):
        gs = slice(gi * gw, (gi + 1) * gw)
        bg = xbc[:, 512 + gi * M2_STATE:512 + (gi + 1) * M2_STATE]
        cg = xbc[:, 768 + gi * M2_STATE:768 + (gi + 1) * M2_STATE]
        cb = _mm_nt(cg, bg, HI)
        h0 = h_ref[gi]
        ys = []
        for hh in range(M2_HEADS // M2_GROUPS):
            h = gi * (M2_HEADS // M2_GROUPS) + hh
            decay = jnp.exp(jnp.where(causal, g[:, h:h + 1] - g_t[h:h + 1, :], -jnp.inf))
            ys.append(_mm(cb * decay, xdt[:, h * M2_HEADDIM:(h + 1) * M2_HEADDIM], HI))
        y = jnp.concatenate(ys, axis=1) + _mm(cg, h0, HI) * eg_w[:, gs]
        h_ref[gi] = h0 * jnp.exp(glast_w[:, gs]) + _mm_tn(bg, xtail[:, gs], HI)
        y = (y + d_w[:, gs] * x[:, gs]) * _silu(z[:, gs])
        y = y * lax.rsqrt(jnp.mean(y * y, axis=-1, keepdims=True) + EPS) * nw[:, gs]
        o_ref[:, gs] = y


def _ssd(proj, cw, cb, alog, dtb, d_w, nw):
    t = proj.shape[0]
    row = lambda n: pl.BlockSpec((1, n), lambda i: (0, 0))
    return pl.pallas_call(
        _ssd_kernel,
        out_shape=jax.ShapeDtypeStruct((t, MIX_W), F32),
        grid=(t // CHUNK,),
        in_specs=[pl.BlockSpec((CHUNK, 512), lambda i: (i, 12)),
                  pl.BlockSpec((CHUNK, 1024), lambda i: (i, 3)),
                  pl.BlockSpec((CHUNK, 128), lambda i: (i, 53)),
                  pl.BlockSpec((CONV_W, 1024), lambda i: (0, 0)),
                  row(1024), row(128), row(128), row(512), row(512)],
        out_specs=pl.BlockSpec((CHUNK, MIX_W), lambda i: (i, 0)),
        scratch_shapes=[pltpu.VMEM((CHUNK + 8, 1024), F32),
                        pltpu.VMEM((M2_GROUPS, M2_STATE, 256), F32)],
        compiler_params=_cparams(1),
    )(proj, proj, proj, cw, cb, alog, dtb, d_w, nw)


def _rwkv_kernel(rkv_ref, lora_ref, mu_ref, w0_ref, wup_ref, a0_ref, aup_ref, gup_ref,
                 kk_ref, ka_ref, rk_ref, lnw_ref, lnb_ref, o_ref, ext_ref, s_ref):
    @pl.when(pl.program_id(0) == 0)
    def _():
        s_ref[...] = jnp.zeros_like(s_ref)

    _stage_with_halo(ext_ref, (rkv_ref, lora_ref))
    cur = ext_ref[pl.ds(8, CHUNK), :]
    prev = ext_ref[pl.ds(7, CHUNK), :]
    _keep_halo(ext_ref)
    mixed = cur + (prev - cur) * mu_ref[...]
    r = mixed[:, 0:512]
    k = mixed[:, 512:1024]
    v = mixed[:, 1024:1536]
    lora = mixed[:, 1536:1792]

    w_raw = -_softplus(-(w0_ref[...] + _mm(jnp.tanh(lora), wup_ref[...], HI))) - 0.5
    log_d = -jnp.exp(w_raw)
    a = _sigmoid(a0_ref[...] + _mm(lora, aup_ref[...], HI))
    gate = _mm(_sigmoid(lora), gup_ref[...], HI)

    causal, strict, diag = _chunk_masks()
    eye = jnp.where(diag, 1.0, 0.0).astype(F32)
    tri = jnp.where(causal, 1.0, 0.0).astype(F32)

    g_in = _mm(tri, log_d, HI)
    g_ex = g_in - log_d
    g_end = g_in[CHUNK - 1:CHUNK, :]
    e_in = jnp.exp(g_in)
    e_neg = jnp.exp(-g_in)
    e_tail = jnp.exp(g_end - g_in)
    e_end = jnp.exp(g_end)

    kk_raw = k * kk_ref[...]
    k_mod = k * (1.0 + (a - 1.0) * ka_ref[...])
    rk_w = r * k_mod * rk_ref[...]
    lnw = lnw_ref[...]
    lnb = lnb_ref[...]

    for h in range(RW_HEADS):
        sl = slice(h * RW_N, (h + 1) * RW_N)
        kkh = kk_raw[:, sl]
        kkh = kkh * lax.rsqrt(jnp.sum(kkh * kkh, axis=-1, keepdims=True) + EPS)
        ah = -(a[:, sl] * kkh)
        rh, kh, vh = r[:, sl], k_mod[:, sl], v[:, sl]
        r_t = rh * e_in[:, sl]
        b_t = kkh * jnp.exp(g_ex[:, sl])
        k_t = kh * e_neg[:, sl]
        a_t = ah * e_neg[:, sl]
        a_bk = jnp.where(strict, _mm_nt(b_t, k_t, HI), 0.0)
        a_ba = jnp.where(strict, _mm_nt(b_t, a_t, HI), 0.0)
        a_rk = jnp.where(causal, _mm_nt(r_t, k_t, HI), 0.0)
        a_ra = jnp.where(causal, _mm_nt(r_t, a_t, HI), 0.0)
        tinv = _inv_unit_lower(-a_ba, eye)
        s0 = s_ref[h]
        u = _mm(tinv, _mm_nt(b_t, s0, HI) + _mm(a_bk, vh, HI), HI)
        y = _mm_nt(r_t, s0, HI) + _mm(a_rk, vh, HI) + _mm(a_ra, u, HI)
        s_ref[h] = (s0 * e_end[:, sl] + _mm_tn(vh, kh * e_tail[:, sl], HI)
                    + _mm_tn(u, ah * e_tail[:, sl], HI))
        mu = jnp.mean(y, axis=-1, keepdims=True)
        yc = y - mu
        y = yc * lax.rsqrt(jnp.mean(yc * yc, axis=-1, keepdims=True) + RWKV_LN_EPS) * lnw[:, sl] + lnb[:, sl]
        y = y + jnp.sum(rk_w[:, sl], axis=-1, keepdims=True) * vh
        o_ref[:, sl] = y * gate[:, sl]


def _rwkv(proj, mu, w0, wup, a0, aup, gup, kk, ka, rk, lnw, lnb):
    t = proj.shape[0]
    row = lambda n: pl.BlockSpec((1, n), lambda i: (0, 0))
    mat = pl.BlockSpec((256, 512), lambda i: (0, 0))
    return pl.pallas_call(
        _rwkv_kernel,
        out_shape=jax.ShapeDtypeStruct((t, MIX_W), F32),
        grid=(t // CHUNK,),
        in_specs=[pl.BlockSpec((CHUNK, 1536), lambda i: (i, 1)),
                  pl.BlockSpec((CHUNK, 256), lambda i: (i, 27)),
                  row(1792), row(512), mat, row(512), mat, mat,
                  row(512), row(512), row(512), row(512), row(512)],
        out_specs=pl.BlockSpec((CHUNK, MIX_W), lambda i: (i, 0)),
        scratch_shapes=[pltpu.VMEM((CHUNK + 8, 1792), F32),
                        pltpu.VMEM((RW_HEADS, RW_N, RW_N), F32)],
        compiler_params=_cparams(1),
    )(proj, proj, mu, w0, wup, a0, aup, gup, kk, ka, rk, lnw, lnb)


def _pad_cols(a, width):
    return jnp.pad(a, ((0, 0), (0, width - a.shape[1])))


def _layout_w_in(w):
    cols = [_pad_cols(w[:, s:s + n], p) for s, n, p in _SRC_PIECES]
    return jnp.concatenate(cols, axis=1).astype(BF16)


def _row(v, width=None):
    v = v.reshape(1, -1).astype(F32)
    return v if width is None else _pad_cols(v, width)


def _lane_row(v, offset):
    return jnp.pad(v.reshape(1, -1).astype(F32), ((0, 0), (offset, 128 - offset - v.shape[0])))


def _pad_rows_at(m, offset, rows=256):
    return jnp.pad(m.astype(F32), ((offset, rows - offset - m.shape[0]), (0, 0)))


def _rotary_tables(t):
    theta = 1.0 / (ROPE_BASE ** jnp.linspace(0.0, 1.0, RET_DK // 2, dtype=F32))
    ang = jnp.arange(t, dtype=F32)[:, None] * theta
    cos = jnp.repeat(jnp.cos(ang), 2, axis=1)
    sin = jnp.stack([-jnp.sin(ang), jnp.sin(ang)], axis=-1).reshape(t, RET_DK)
    return jnp.tile(cos, (1, RET_HEADS)), jnp.tile(sin, (1, RET_HEADS))


def _token_mix(h, l, p, cos, sin):
    proj = _norm_matmul(h, p['norm1_w'][l], _layout_w_in(p['w_in'][l]))

    o_a = _gdn(proj, p['gdn_conv_w'][l].astype(F32),
               _lane_row(p['gdn_a_log'][l], 4), _lane_row(p['gdn_dt_bias'][l], 4),
               _row(p['gdn_norm_w'][l]))

    o_b = _retention(proj, cos, sin, _row(p['ret_norm_w'][l]))

    o_c = _ssd(proj, p['m2_conv_w'][l].astype(F32), _row(p['m2_conv_b'][l]),
               _lane_row(p['m2_a_log'][l], 0), _lane_row(p['m2_dt_bias'][l], 0),
               _row(jnp.repeat(p['m2_d'][l], M2_HEADDIM)), _row(p['m2_norm_w'][l]))

    mu = p['rw_mu'][l]
    o_d = _rwkv(proj, _row(mu, 1792), _row(p['rw_w0'][l]),
                _pad_rows_at(p['rw_w_up'][l], 0), _row(p['rw_a0'][l]),
                _pad_rows_at(p['rw_a_up'][l], RW_W_LORA),
                _pad_rows_at(p['rw_g_up'][l], RW_W_LORA + RW_A_LORA),
                _row(p['rw_k_k'][l]), _row(p['rw_k_a'][l]), _row(p['rw_r_k'][l]),
                _row(p['rw_ln_w'][l]), _row(p['rw_ln_b'][l]))

    return _out_proj(h, (o_a, o_b, o_c, o_d), p['w_out'][l].astype(BF16))


def kernel(x, norm1_w, w_in, gdn_conv_w, gdn_a_log, gdn_dt_bias, gdn_norm_w, ret_norm_w, m2_conv_w, m2_conv_b, m2_a_log, m2_dt_bias, m2_d, m2_norm_w, rw_mu, rw_w0, rw_w_up, rw_a0, rw_a_up, rw_g_up, rw_k_k, rw_k_a, rw_r_k, rw_ln_w, rw_ln_b, w_out, norm2_w, w_ffn_up, w_ffn_down, final_norm_w):
    p = dict(norm1_w=norm1_w, w_in=w_in, gdn_conv_w=gdn_conv_w, gdn_a_log=gdn_a_log,
             gdn_dt_bias=gdn_dt_bias, gdn_norm_w=gdn_norm_w, ret_norm_w=ret_norm_w,
             m2_conv_w=m2_conv_w, m2_conv_b=m2_conv_b, m2_a_log=m2_a_log, m2_dt_bias=m2_dt_bias,
             m2_d=m2_d, m2_norm_w=m2_norm_w, rw_mu=rw_mu, rw_w0=rw_w0, rw_w_up=rw_w_up,
             rw_a0=rw_a0, rw_a_up=rw_a_up, rw_g_up=rw_g_up, rw_k_k=rw_k_k, rw_k_a=rw_k_a,
             rw_r_k=rw_r_k, rw_ln_w=rw_ln_w, rw_ln_b=rw_ln_b, w_out=w_out)
    bsz, t, d = x.shape
    depth = w_in.shape[0]
    cos, sin = _rotary_tables(t)
    outs = []
    for b in range(bsz):
        h = x[b]
        for l in range(depth):
            h = _token_mix(h, l, p, cos, sin)
            h = _ffn(h, norm2_w[l], w_ffn_up[l].astype(BF16), w_ffn_down[l].astype(BF16),
                     final_norm_w, final_norm=(l == depth - 1))
        outs.append(h)
    return jnp.stack(outs, axis=0)
```

```python
import functools
import math

import jax
import jax.numpy as jnp
from jax import lax
from jax.experimental import pallas as pl
from jax.experimental.pallas import tpu as pltpu

F32 = jnp.float32
BF16 = jnp.bfloat16
HI = lax.Precision.HIGHEST

D_MODEL = 2048
D_FF = 4 * D_MODEL
CONV_W = 4
CHUNK = 64
EPS = 1e-6
ROPE_BASE = 10000.0
RWKV_LN_EPS = 64e-5

GDN_HEADS, GDN_DK, GDN_DV = 4, 128, 128
RET_HEADS, RET_DK, RET_DV = 4, 64, 128
M2_HEADS, M2_HEADDIM, M2_GROUPS, M2_STATE = 8, 64, 2, 128
RW_HEADS, RW_N = 8, 64
RW_W_LORA, RW_A_LORA, RW_G_LORA = 32, 32, 96
MIX_W = 512

_GDN0, _RET0, _M20, _RW0 = 0, 2056, 3592, 5136

P_PAD = 7168
_SRC_PIECES = (
    (_GDN0, 1536, 1536),
    (_RW0, 1536, 1536),
    (_M20 + 512, 1024, 1024),
    (_GDN0 + 1536, 512, 512),
    (_RET0, 512, 512),
    (_RET0 + 512, 512, 512),
    (_RET0 + 1024, 512, 512),
    (_M20, 512, 512),
    (_GDN0 + 2048, 8, 128),
    (_M20 + 1536, 8, 128),
    (_RW0 + 1536, 160, 256),
)

VMEM_LIMIT = 56 * 1024 * 1024

NN = (((1,), (0,)), ((), ()))
NT = (((1,), (1,)), ((), ()))
TN = (((0,), (0,)), ((), ()))


def _cparams(n_axes):
    return pltpu.CompilerParams(dimension_semantics=("arbitrary",) * n_axes,
                                vmem_limit_bytes=VMEM_LIMIT)


def _dg(a, b, dims=NN, prec=None):
    return lax.dot_general(a, b, dims, preferred_element_type=F32, precision=prec)


def _split(a):
    hi = a.astype(BF16)
    lo = (a - hi.astype(F32)).astype(BF16)
    return hi, lo


def _dot1(a, b, dims=NN):
    return _dg(a.astype(BF16), b.astype(BF16), dims)


def _dot3(a, b, dims=NN):
    ah, al = _split(a)
    bh, bl = _split(b)
    return _dg(ah, bh, dims) + _dg(ah, bl, dims) + _dg(al, bh, dims)


def _sum_bcast(x, blk):
    xh, xl = _split(x)
    return _dg(xh, blk) + _dg(xl, blk)


def _sigmoid(x):
    return 1.0 / (1.0 + jnp.exp(-x))


def _silu(x):
    return x * _sigmoid(x)


def _softplus(x):
    return jnp.maximum(x, 0.0) + jnp.log1p(jnp.exp(-jnp.abs(x)))


def _iota2(shape, axis):
    return lax.broadcasted_iota(jnp.int32, shape, axis)


def _chunk_masks():
    r = _iota2((CHUNK, CHUNK), 0)
    c = _iota2((CHUNK, CHUNK), 1)
    return r >= c, r > c, r == c


class _Pair:
    def __init__(self):
        lane = _iota2((CHUNK, 2 * CHUNK), 1)
        row = _iota2((CHUNK, 2 * CHUNK), 0)
        self.lo = lane < CHUNK
        col = jnp.where(self.lo, lane, lane - CHUNK)
        self.dist = (row - col).astype(F32)
        self.causal = row >= col
        self.strict = row > col
        self.eye = jnp.where(row == col, 1.0, 0.0).astype(F32)
        r2 = _iota2((2 * CHUNK, 2 * CHUNK), 0)
        c2 = _iota2((2 * CHUNK, 2 * CHUNK), 1)
        self.same_block = (r2 < CHUNK) == (c2 < CHUNK)
        self.eye2 = jnp.where(r2 == c2, 1.0, 0.0).astype(F32)

    def bd(self, x):
        z = jnp.zeros_like(x)
        return jnp.concatenate([jnp.where(self.lo, x, z), jnp.where(self.lo, z, x)], axis=0)

    def inverse(self, lows):
        ps = [self.eye - low for low in lows]
        curs = list(lows)
        n = 1
        while n < CHUNK:
            rhss = [self.bd(cur) for cur in curs]
            if n == 1:
                curs = [_dot3(cur, rhs) for cur, rhs in zip(curs, rhss)]
            elif 2 * n < CHUNK:
                outs = [_dot3(jnp.concatenate([cur, p], axis=0), rhs)
                        for cur, p, rhs in zip(curs, ps, rhss)]
                curs = [out[:CHUNK] for out in outs]
                ps = [p + out[CHUNK:] for p, out in zip(ps, outs)]
            else:
                ps = [p + _dot3(p, rhs) for p, rhs in zip(ps, rhss)]
            n *= 2
        return ps


def _norm_matmul_kernel(x_ref, nw_ref, w_ref, o_ref, a_ref):
    @pl.when(pl.program_id(1) == 0)
    def _():
        x = x_ref[...]
        y = x * lax.rsqrt(jnp.mean(x * x, axis=-1, keepdims=True) + EPS) * nw_ref[...]
        a_ref[...] = y.astype(BF16)

    o_ref[...] = jnp.dot(a_ref[...], w_ref[...], preferred_element_type=F32)


def _norm_matmul(x, nw, w, tm=1024, tn=1024):
    t, d = x.shape
    n = w.shape[1]
    tm = min(tm, t)
    return pl.pallas_call(
        _norm_matmul_kernel,
        out_shape=jax.ShapeDtypeStruct((t, n), F32),
        grid=(t // tm, n // tn),
        in_specs=[pl.BlockSpec((tm, d), lambda i, j: (i, 0)),
                  pl.BlockSpec((1, d), lambda i, j: (0, 0)),
                  pl.BlockSpec((d, tn), lambda i, j: (0, j))],
        out_specs=pl.BlockSpec((tm, tn), lambda i, j: (i, j)),
        scratch_shapes=[pltpu.VMEM((tm, d), BF16)],
        compiler_params=_cparams(2),
    )(x, nw.reshape(1, d), w)


def _out_proj_kernel(h_ref, oa_ref, ob_ref, oc_ref, od_ref, w_ref, o_ref):
    acc = h_ref[...]
    for idx, part in enumerate((oa_ref, ob_ref, oc_ref, od_ref)):
        acc = acc + jnp.dot(part[...].astype(BF16), w_ref[idx * MIX_W:(idx + 1) * MIX_W, :],
                            preferred_element_type=F32)
    o_ref[...] = acc


def _out_proj(h, parts, w, tm=256):
    t, d = h.shape
    tm = min(tm, t)
    part_spec = pl.BlockSpec((tm, MIX_W), lambda i: (i, 0))
    return pl.pallas_call(
        _out_proj_kernel,
        out_shape=jax.ShapeDtypeStruct((t, d), F32),
        grid=(t // tm,),
        in_specs=[pl.BlockSpec((tm, d), lambda i: (i, 0)),
                  part_spec, part_spec, part_spec, part_spec,
                  pl.BlockSpec((4 * MIX_W, d), lambda i: (0, 0))],
        out_specs=pl.BlockSpec((tm, d), lambda i: (i, 0)),
        compiler_params=_cparams(1),
    )(h, *parts, w)


def _ffn_kernel(h_ref, nw_ref, wu_ref, wd_ref, fw_ref, o_ref, a_ref, acc_ref, *, final_norm):
    f = pl.program_id(1)

    @pl.when(f == 0)
    def _():
        x = h_ref[...]
        y = x * lax.rsqrt(jnp.mean(x * x, axis=-1, keepdims=True) + EPS) * nw_ref[...]
        a_ref[...] = y.astype(BF16)
        acc_ref[...] = jnp.zeros_like(acc_ref)

    u = jnp.dot(a_ref[...], wu_ref[...], preferred_element_type=F32)
    s = jnp.square(jnp.maximum(u, 0.0)).astype(BF16)
    acc_ref[...] += jnp.dot(s, wd_ref[...], preferred_element_type=F32)

    @pl.when(f == pl.num_programs(1) - 1)
    def _():
        y = h_ref[...] + acc_ref[...]
        if final_norm:
            y = y * lax.rsqrt(jnp.mean(y * y, axis=-1, keepdims=True) + EPS) * fw_ref[...]
        o_ref[...] = y


def _ffn(h, nw, wu, wd, fw, final_norm, tm=512, tf=512):
    t, d = h.shape
    ff = wu.shape[1]
    tm = min(tm, t)
    return pl.pallas_call(
        functools.partial(_ffn_kernel, final_norm=final_norm),
        out_shape=jax.ShapeDtypeStruct((t, d), F32),
        grid=(t // tm, ff // tf),
        in_specs=[pl.BlockSpec((tm, d), lambda i, f: (i, 0)),
                  pl.BlockSpec((1, d), lambda i, f: (0, 0)),
                  pl.BlockSpec((d, tf), lambda i, f: (0, f)),
                  pl.BlockSpec((tf, d), lambda i, f: (f, 0)),
                  pl.BlockSpec((1, d), lambda i, f: (0, 0))],
        out_specs=pl.BlockSpec((tm, d), lambda i, f: (i, 0)),
        scratch_shapes=[pltpu.VMEM((tm, d), BF16), pltpu.VMEM((tm, d), F32)],
        compiler_params=_cparams(2),
    )(h, nw.reshape(1, d), wu, wd, fw.reshape(1, d))


def _stage_with_halo(ext_ref, pieces, rows):
    @pl.when(pl.program_id(0) == 0)
    def _():
        ext_ref[0:8, :] = jnp.zeros((8, ext_ref.shape[1]), F32)

    col = 0
    for ref in pieces:
        w = ref.shape[1]
        ext_ref[8:8 + rows, col:col + w] = ref[...]
        col += w


def _keep_halo(ext_ref, rows):
    ext_ref[0:8, :] = ext_ref[rows:rows + 8, :]


def _causal_conv(ext_ref, cw, rows):
    acc = ext_ref[pl.ds(8, rows), :] * cw[3:4, :]
    for i in range(CONV_W - 1):
        acc = acc + ext_ref[pl.ds(5 + i, rows), :] * cw[i:i + 1, :]
    return acc


def _const_spec(shape):
    return pl.BlockSpec(shape, lambda i: (0,) * len(shape))


def _block_diag_ones(n, block, lower=False, dtype=F32):
    r = jnp.arange(n)[:, None]
    c = jnp.arange(n)[None, :]
    m = (r // block) == (c // block)
    if lower:
        m = m & (r >= c)
    return m.astype(dtype)


def _gdn_kernel(qkv_ref, z_ref, gate_ref, cw_ref, alog_ref, dtb_ref, nw_ref, btri_ref, o_ref,
                ext_ref, s_ref, *, nch):
    rows = nch * CHUNK

    @pl.when(pl.program_id(0) == 0)
    def _():
        s_ref[...] = jnp.zeros_like(s_ref)

    _stage_with_halo(ext_ref, (qkv_ref,), rows)
    qkv = _silu(_causal_conv(ext_ref, cw_ref[...], rows))
    _keep_halo(ext_ref, rows)

    causal, strict, diag = _chunk_masks()
    eye = jnp.where(diag, 1.0, 0.0).astype(F32)
    pair = _Pair()

    gt = gate_ref[...]
    beta = _sigmoid(gt)
    log_a = -jnp.exp(alog_ref[...]) * _softplus(gt + dtb_ref[...])
    g_all = _dg(btri_ref[...], log_a, NN, HI)
    z = z_ref[...]
    nw = nw_ref[...]

    qs, ks, vs = [], [], []
    for h in range(GDN_HEADS):
        qh = qkv[:, h * GDN_DK:(h + 1) * GDN_DK]
        kh = qkv[:, 512 + h * GDN_DK:512 + (h + 1) * GDN_DK]
        qs.append(qh * lax.rsqrt(jnp.sum(qh * qh, axis=-1, keepdims=True) + EPS) * (GDN_DK ** -0.5))
        ks.append(kh * lax.rsqrt(jnp.sum(kh * kh, axis=-1, keepdims=True) + EPS))
        vs.append(qkv[:, 1024 + h * GDN_DV:1024 + (h + 1) * GDN_DV])

    combos = [(c, h) for c in range(nch) for h in range(GDN_HEADS)]
    rsl = lambda c: slice(c * CHUNK, (c + 1) * CHUNK)
    g_ts = [_dg(g_all[rsl(c)], eye, TN, HI) for c in range(nch)]
    bcol = {(c, h): beta[rsl(c), h:h + 1] for c, h in combos}
    gcol = {(c, h): g_all[rsl(c), 4 + h:5 + h] for c, h in combos}
    kq = {(c, h): _dot3(jnp.concatenate([ks[h][rsl(c)], qs[h][rsl(c)]], axis=0), ks[h][rsl(c)], NT)
          for c, h in combos}
    lows, qks = {}, {}
    for c, h in combos:
        gamma = jnp.exp(jnp.where(causal, gcol[c, h] - g_ts[c][4 + h:5 + h, :], -jnp.inf))
        lows[c, h] = jnp.where(strict, kq[c, h][:CHUNK] * gamma * bcol[c, h], 0.0)
        qks[c, h] = kq[c, h][CHUNK:] * gamma
    pairs = [(c, p) for c in range(nch) for p in range(GDN_HEADS // 2)]
    tps = pair.inverse([jnp.concatenate([lows[c, 2 * p], lows[c, 2 * p + 1]], axis=1) for c, p in pairs])
    tinv = {}
    for (c, p), tp in zip(pairs, tps):
        tinv[c, 2 * p] = tp[:, :CHUNK]
        tinv[c, 2 * p + 1] = tp[:, CHUNK:]
    egc = {ch: jnp.exp(gcol[ch]) for ch in combos}
    glast = {ch: gcol[ch][CHUNK - 1:CHUNK, :] for ch in combos}
    wu = {(c, h): _dot3(tinv[c, h], jnp.concatenate(
        [ks[h][rsl(c)] * (bcol[c, h] * egc[c, h]), vs[h][rsl(c)] * bcol[c, h]], axis=1)) for c, h in combos}
    qwu = {ch: _dot1(qks[ch], wu[ch]) for ch in combos}
    kwu = {(c, h): _dot3(ks[h][rsl(c)] * jnp.exp(glast[c, h] - gcol[c, h]), wu[c, h], TN) for c, h in combos}
    for c in range(nch):
        rs = rsl(c)
        s0 = [s_ref[h] if c == 0 else s_new[h] for h in range(GDN_HEADS)]
        s_new = []
        for h in range(GDN_HEADS):
            sl = slice(h * GDN_DV, (h + 1) * GDN_DV)
            q_eff = qs[h][rs] * egc[c, h] - qwu[c, h][:, :GDN_DK]
            o = _dot1(q_eff, s0[h]) + qwu[c, h][:, GDN_DK:]
            p_mat = pair.eye2 * jnp.exp(glast[c, h]) - kwu[c, h][:, :GDN_DK]
            s_new.append(_dot3(p_mat, s0[h]) + kwu[c, h][:, GDN_DK:])
            o = o * lax.rsqrt(jnp.mean(o * o, axis=-1, keepdims=True) + EPS) * nw
            o_ref[rs, sl] = o * _silu(z[rs, sl])
    for h in range(GDN_HEADS):
        s_ref[h] = s_new[h]


def _gdn(proj, cw, alog, dtb, nw, nch):
    t = proj.shape[0]
    rows = nch * CHUNK
    return pl.pallas_call(
        functools.partial(_gdn_kernel, nch=nch),
        out_shape=jax.ShapeDtypeStruct((t, MIX_W), F32),
        grid=(t // rows,),
        in_specs=[pl.BlockSpec((rows, 1536), lambda i: (i, 0)),
                  pl.BlockSpec((rows, 512), lambda i: (i, 8)),
                  pl.BlockSpec((rows, 128), lambda i: (i, 52)),
                  _const_spec((CONV_W, 1536)),
                  _const_spec((1, 128)), _const_spec((1, 128)), _const_spec((1, 128)),
                  _const_spec((rows, rows))],
        out_specs=pl.BlockSpec((rows, MIX_W), lambda i: (i, 0)),
        scratch_shapes=[pltpu.VMEM((rows + 8, 1536), F32),
                        pltpu.VMEM((GDN_HEADS, GDN_DK, GDN_DV), F32)],
        compiler_params=_cparams(1),
    )(proj, proj, proj, cw, alog, dtb, nw, _block_diag_ones(rows, CHUNK, lower=True))


def _ret_kernel(qk_ref, v_ref, g_ref, cos_ref, sin_ref, nw_ref, o_ref, r_ref, *, nch):
    @pl.when(pl.program_id(0) == 0)
    def _():
        r_ref[...] = jnp.zeros_like(r_ref)

    rows = nch * CHUNK
    qk = qk_ref[...]
    cos = cos_ref[...]
    sin = sin_ref[...]
    even = (_iota2((rows, 256), 1) % 2) == 0

    def rot(x):
        partner = jnp.where(even, pltpu.roll(x, 255, 1), pltpu.roll(x, 1, 1))
        return x * cos + partner * sin

    q = rot(qk[:, 0:256])
    k = rot(qk[:, 256:512]) * (RET_DK ** -0.5)
    v = v_ref[...]
    gate = g_ref[...]
    nw = nw_ref[...]

    pair = _Pair()
    pos = _iota2((CHUNK, 2 * CHUNK), 0).astype(F32)
    zeros_v = jnp.zeros((CHUNK, RET_DV), F32)
    r2 = _iota2((2 * RET_DK, 2 * RET_DV), 0)
    c2 = _iota2((2 * RET_DK, 2 * RET_DV), 1)
    state_block = (r2 < RET_DK) == (c2 < RET_DV)

    for p in range(RET_HEADS // 2):
        lg0 = math.log1p(-(2.0 ** (-5.0 - 2 * p)))
        lg1 = math.log1p(-(2.0 ** (-5.0 - (2 * p + 1))))
        lg = jnp.where(pair.lo, lg0, lg1)
        dmask = jnp.exp(jnp.where(pair.causal, pair.dist * lg, -jnp.inf))
        k_dec = jnp.exp((CHUNK - 1.0 - pos) * lg)
        q_dec = jnp.exp((pos + 1.0) * lg)
        lg_v = jnp.where(_iota2((1, 2 * RET_DV), 1) < RET_DV, lg0, lg1)
        chunk_decay = jnp.exp(CHUNK * lg_v)
        ls = slice(p * 2 * RET_DK, (p + 1) * 2 * RET_DK)
        vsl = slice(p * 2 * RET_DV, (p + 1) * 2 * RET_DV)
        for c in range(nch):
            rs = slice(c * CHUNK, (c + 1) * CHUNK)
            qp, kp, vp = q[rs, ls], k[rs, ls], v[rs, vsl]
            sc = _dot1(qp, pair.bd(kp), NT) * dmask
            v_bd = jnp.concatenate(
                [jnp.concatenate([vp[:, :RET_DV], zeros_v], axis=1),
                 jnp.concatenate([zeros_v, vp[:, RET_DV:]], axis=1)], axis=0)
            r0 = r_ref[p]
            o = _dot1(sc, v_bd) + _dot1(qp * q_dec, r0)
            upd = _dot1(kp * k_dec, vp, TN)
            r_ref[p] = r0 * chunk_decay + jnp.where(state_block, upd, 0.0)
            for hh in range(2):
                sl = slice(vsl.start + hh * RET_DV, vsl.start + (hh + 1) * RET_DV)
                oh = o[:, hh * RET_DV:(hh + 1) * RET_DV]
                mu = jnp.mean(oh, axis=-1, keepdims=True)
                oc = oh - mu
                oh = oc * lax.rsqrt(jnp.mean(oc * oc, axis=-1, keepdims=True) + EPS) * nw[:, sl]
                o_ref[rs, sl] = oh * _silu(gate[rs, sl])


def _retention(proj, cos, sin, nw, nch):
    t = proj.shape[0]
    rows = nch * CHUNK
    return pl.pallas_call(
        functools.partial(_ret_kernel, nch=nch),
        out_shape=jax.ShapeDtypeStruct((t, MIX_W), F32),
        grid=(t // rows,),
        in_specs=[pl.BlockSpec((rows, 512), lambda i: (i, 9)),
                  pl.BlockSpec((rows, 512), lambda i: (i, 10)),
                  pl.BlockSpec((rows, 512), lambda i: (i, 11)),
                  pl.BlockSpec((rows, 256), lambda i: (i, 0)),
                  pl.BlockSpec((rows, 256), lambda i: (i, 0)),
                  _const_spec((1, 512))],
        out_specs=pl.BlockSpec((rows, MIX_W), lambda i: (i, 0)),
        scratch_shapes=[pltpu.VMEM((RET_HEADS // 2, 2 * RET_DK, 2 * RET_DV), F32)],
        compiler_params=_cparams(1),
    )(proj, proj, proj, cos, sin, nw)


def _ssd_kernel(z_ref, xbc_ref, dt_ref, cw_ref, cb_ref, alog_ref, dtb_ref, d_ref, nw_ref,
                btri_ref, bones_ref, expand_ref, o_ref, ext_ref, h_ref, *, nch):
    rows = nch * CHUNK

    @pl.when(pl.program_id(0) == 0)
    def _():
        h_ref[...] = jnp.zeros_like(h_ref)

    _stage_with_halo(ext_ref, (xbc_ref,), rows)
    xbc = _silu(_causal_conv(ext_ref, cw_ref[...], rows) + cb_ref[...])
    _keep_halo(ext_ref, rows)
    x = xbc[:, 0:512]

    _, _, diag = _chunk_masks()
    eye = jnp.where(diag, 1.0, 0.0).astype(F32)
    pair = _Pair()
    expand = expand_ref[...]

    dt = _softplus(dt_ref[...] + dtb_ref[...])
    da = dt * (-jnp.exp(alog_ref[...]))
    g_all = _dg(btri_ref[...], da, NN, HI)
    g_w = _dg(g_all, expand, NN, HI)
    gtot_w = _dg(_dg(bones_ref[...], da, NN, HI), expand, NN, HI)
    dt_w = _dg(dt, expand, NN, HI)
    xdt = x * dt_w
    xtail = xdt * jnp.exp(gtot_w - g_w)
    eg_w = jnp.exp(g_w)
    etot_w = jnp.exp(gtot_w)
    z = z_ref[...]
    d_w = d_ref[...]
    nw = nw_ref[...]
    gw = M2_HEADS // M2_GROUPS * M2_HEADDIM

    for c in range(nch):
        rs = slice(c * CHUNK, (c + 1) * CHUNK)
        g = g_all[rs]
        g_t = _dg(g, eye, TN, HI)
        for gi in range(M2_GROUPS):
            gs = slice(gi * gw, (gi + 1) * gw)
            bg = xbc[rs, 512 + gi * M2_STATE:512 + (gi + 1) * M2_STATE]
            cg = xbc[rs, 768 + gi * M2_STATE:768 + (gi + 1) * M2_STATE]
            cb2 = _dot1(cg, jnp.concatenate([bg, bg], axis=0), NT)
            h0 = h_ref[gi]
            ys = []
            for pp in range(2):
                ha = gi * 4 + 2 * pp
                gcol = jnp.where(pair.lo, g[:, ha:ha + 1], g[:, ha + 1:ha + 2])
                grow = jnp.concatenate([g_t[ha:ha + 1, :], g_t[ha + 1:ha + 2, :]], axis=1)
                decay = jnp.exp(jnp.where(pair.causal, gcol - grow, -jnp.inf))
                ys.append(_dot1(cb2 * decay, pair.bd(xdt[rs, ha * M2_HEADDIM:(ha + 2) * M2_HEADDIM])))
            y = jnp.concatenate(ys, axis=1) + _dot1(cg, h0) * eg_w[rs, gs]
            h_ref[gi] = h0 * etot_w[c * CHUNK:c * CHUNK + 1, gs] + _dot1(bg, xtail[rs, gs], TN)
            y = (y + d_w[:, gs] * x[rs, gs]) * _silu(z[rs, gs])
            y = y * lax.rsqrt(jnp.mean(y * y, axis=-1, keepdims=True) + EPS) * nw[:, gs]
            o_ref[rs, gs] = y


def _head_expand_matrix(n_heads, width):
    r = jnp.arange(128)[:, None]
    c = jnp.arange(n_heads * width)[None, :]
    return (r == c // width).astype(F32)


def _ssd(proj, cw, cb, alog, dtb, d_w, nw, nch):
    t = proj.shape[0]
    rows = nch * CHUNK
    return pl.pallas_call(
        functools.partial(_ssd_kernel, nch=nch),
        out_shape=jax.ShapeDtypeStruct((t, MIX_W), F32),
        grid=(t // rows,),
        in_specs=[pl.BlockSpec((rows, 512), lambda i: (i, 12)),
                  pl.BlockSpec((rows, 1024), lambda i: (i, 3)),
                  pl.BlockSpec((rows, 128), lambda i: (i, 53)),
                  _const_spec((CONV_W, 1024)),
                  _const_spec((1, 1024)), _const_spec((1, 128)), _const_spec((1, 128)),
                  _const_spec((1, 512)), _const_spec((1, 512)),
                  _const_spec((rows, rows)), _const_spec((rows, rows)), _const_spec((128, 512))],
        out_specs=pl.BlockSpec((rows, MIX_W), lambda i: (i, 0)),
        scratch_shapes=[pltpu.VMEM((rows + 8, 1024), F32),
                        pltpu.VMEM((M2_GROUPS, M2_STATE, 256), F32)],
        compiler_params=_cparams(1),
    )(proj, proj, proj, cw, cb, alog, dtb, d_w, nw,
      _block_diag_ones(rows, CHUNK, lower=True), _block_diag_ones(rows, CHUNK),
      _head_expand_matrix(M2_HEADS, M2_HEADDIM))


def _rwkv_kernel(rkv_ref, lora_ref, mu_ref, w0_ref, wup_ref, a0_ref, aup_ref, gup_ref,
                 kk_ref, ka_ref, rk_ref, lnw_ref, lnb_ref, btri_ref, bones_ref, hblk_ref,
                 o_ref, ext_ref, s_ref, *, nch):
    rows = nch * CHUNK

    @pl.when(pl.program_id(0) == 0)
    def _():
        s_ref[...] = jnp.zeros_like(s_ref)

    _stage_with_halo(ext_ref, (rkv_ref, lora_ref), rows)
    cur = ext_ref[pl.ds(8, rows), :]
    prev = ext_ref[pl.ds(7, rows), :]
    _keep_halo(ext_ref, rows)
    mixed = cur + (prev - cur) * mu_ref[...]
    r = mixed[:, 0:512]
    k = mixed[:, 512:1024]
    v = mixed[:, 1024:1536]
    lora = mixed[:, 1536:1792]

    w_raw = -_softplus(-(w0_ref[...] + _dot3(jnp.tanh(lora), wup_ref[...]))) - 0.5
    log_d = -jnp.exp(w_raw)
    a = _sigmoid(a0_ref[...] + _dot1(lora, aup_ref[...]))
    gate = _dot1(_sigmoid(lora), gup_ref[...])
    hblk = hblk_ref[...]

    g_in = _dg(btri_ref[...], log_d, NN, HI)
    g_tot = _dg(bones_ref[...], log_d, NN, HI)
    e_in = jnp.exp(g_in)
    e_neg = jnp.exp(-g_in)
    e_ex = jnp.exp(g_in - log_d)
    e_tail = jnp.exp(g_tot - g_in)
    e_end = jnp.exp(g_tot)

    kk = k * kk_ref[...]
    kk = kk * lax.rsqrt(_sum_bcast(kk * kk, hblk) + EPS)
    k_mod = k * (1.0 + (a - 1.0) * ka_ref[...])
    a_vec = -(a * kk)
    r_t = r * e_in
    b_t = kk * e_ex
    k_t = k_mod * e_neg
    a_t = a_vec * e_neg
    k_c = k_mod * e_tail
    a_c = a_vec * e_tail

    pair = _Pair()
    npair = RW_HEADS // 2
    combos = [(c, j) for c in range(nch) for j in range(npair)]
    cut = lambda arr, cj: arr[cj[0] * CHUNK:(cj[0] + 1) * CHUNK, cj[1] * 2 * RW_N:(cj[1] + 1) * 2 * RW_N]
    lhs = {cj: jnp.concatenate([cut(b_t, cj), cut(r_t, cj)], axis=0) for cj in combos}
    x1 = {cj: _dot1(lhs[cj], pair.bd(cut(k_t, cj)), NT) for cj in combos}
    x2 = {cj: _dot3(lhs[cj], pair.bd(cut(a_t, cj)), NT) for cj in combos}
    a_ra = {cj: jnp.where(pair.causal, x2[cj][CHUNK:], 0.0) for cj in combos}
    tinv = dict(zip(combos, pair.inverse([-jnp.where(pair.strict, x2[cj][:CHUNK], 0.0) for cj in combos])))
    av = {cj: _dot1(jnp.concatenate([jnp.where(pair.strict, x1[cj][:CHUNK], 0.0),
                                     jnp.where(pair.causal, x1[cj][CHUNK:], 0.0)], axis=0),
                    pair.bd(cut(v, cj))) for cj in combos}
    tz = {cj: _dot3(tinv[cj], jnp.concatenate([pair.bd(cut(b_t, cj)), pair.bd(av[cj][:CHUNK])], axis=1))
          for cj in combos}
    az = {cj: _dot1(a_ra[cj], jnp.concatenate([pair.bd(tz[cj][:, :2 * RW_N]), pair.bd(tz[cj][:, 2 * RW_N:])],
                                              axis=1)) for cj in combos}
    r_eff = {cj: cut(r_t, cj) + az[cj][:, :2 * RW_N] for cj in combos}
    y0 = {cj: av[cj][CHUNK:] + az[cj][:, 2 * RW_N:] for cj in combos}
    p_mat = {cj: pair.eye2 * e_end[cj[0] * CHUNK:cj[0] * CHUNK + 1, cj[1] * 2 * RW_N:(cj[1] + 1) * 2 * RW_N]
             + jnp.where(pair.same_block, _dot3(tz[cj][:, :2 * RW_N], cut(a_c, cj), TN), 0.0) for cj in combos}
    q_mat = {cj: jnp.where(pair.same_block,
                           _dot1(jnp.concatenate([cut(v, cj), tz[cj][:, 2 * RW_N:]], axis=0),
                                 jnp.concatenate([cut(k_c, cj), cut(a_c, cj)], axis=0), TN), 0.0)
             for cj in combos}
    y_rows = []
    for c in range(nch):
        s0 = [s_ref[j] if c == 0 else s_new[j] for j in range(npair)]
        y_rows.append(jnp.concatenate(
            [_dot1(r_eff[c, j], s0[j], NT) + y0[c, j] for j in range(npair)], axis=1))
        s_new = [_dot3(s0[j], p_mat[c, j]) + q_mat[c, j] for j in range(npair)]
    for j in range(npair):
        s_ref[j] = s_new[j]
    y = jnp.concatenate(y_rows, axis=0)

    inv_n = 1.0 / RW_N
    mu = _sum_bcast(y, hblk) * inv_n
    yc = y - mu
    var = _sum_bcast(yc * yc, hblk) * inv_n
    y = yc * lax.rsqrt(var + RWKV_LN_EPS) * lnw_ref[...] + lnb_ref[...]
    y = y + _sum_bcast(r * k_mod * rk_ref[...], hblk) * v
    o_ref[...] = y * gate


def _rwkv(proj, mu, w0, wup, a0, aup, gup, kk, ka, rk, lnw, lnb, nch):
    t = proj.shape[0]
    rows = nch * CHUNK
    row = lambda n: _const_spec((1, n))
    mat = _const_spec((256, 512))
    return pl.pallas_call(
        functools.partial(_rwkv_kernel, nch=nch),
        out_shape=jax.ShapeDtypeStruct((t, MIX_W), F32),
        grid=(t // rows,),
        in_specs=[pl.BlockSpec((rows, 1536), lambda i: (i, 1)),
                  pl.BlockSpec((rows, 256), lambda i: (i, 27)),
                  row(1792), row(512), mat, row(512), mat, mat,
                  row(512), row(512), row(512), row(512), row(512),
                  _const_spec((rows, rows)), _const_spec((rows, rows)), _const_spec((512, 512))],
        out_specs=pl.BlockSpec((rows, MIX_W), lambda i: (i, 0)),
        scratch_shapes=[pltpu.VMEM((rows + 8, 1792), F32),
                        pltpu.VMEM((RW_HEADS // 2, 2 * RW_N, 2 * RW_N), F32)],
        compiler_params=_cparams(1),
    )(proj, proj, mu, w0, wup, a0, aup, gup, kk, ka, rk, lnw, lnb,
      _block_diag_ones(rows, CHUNK, lower=True), _block_diag_ones(rows, CHUNK),
      _block_diag_ones(MIX_W, RW_N, dtype=BF16))


def _pad_cols(a, width):
    return jnp.pad(a, ((0, 0), (0, width - a.shape[1])))


def _layout_w_in(w):
    cols = [_pad_cols(w[:, s:s + n], p) for s, n, p in _SRC_PIECES]
    return jnp.concatenate(cols, axis=1).astype(BF16)


def _row(v, width=None):
    v = v.reshape(1, -1).astype(F32)
    return v if width is None else _pad_cols(v, width)


def _lane_row(v, offset):
    return jnp.pad(v.reshape(1, -1).astype(F32), ((0, 0), (offset, 128 - offset - v.shape[0])))


def _pad_rows_at(m, offset, rows=256):
    return jnp.pad(m.astype(F32), ((offset, rows - offset - m.shape[0]), (0, 0)))


def _rotary_tables(t):
    theta = 1.0 / (ROPE_BASE ** jnp.linspace(0.0, 1.0, RET_DK // 2, dtype=F32))
    ang = jnp.arange(t, dtype=F32)[:, None] * theta
    cos = jnp.repeat(jnp.cos(ang), 2, axis=1)
    sin = jnp.stack([-jnp.sin(ang), jnp.sin(ang)], axis=-1).reshape(t, RET_DK)
    return jnp.tile(cos, (1, RET_HEADS)), jnp.tile(sin, (1, RET_HEADS))


NCH_GDN, NCH_RET, NCH_SSD, NCH_RWKV = 4, 2, 2, 4


def _token_mix(h, l, p, cos, sin):
    t = h.shape[0]
    nch = lambda n: min(n, t // CHUNK)
    proj = _norm_matmul(h, p['norm1_w'][l], _layout_w_in(p['w_in'][l]))

    o_a = _gdn(proj, p['gdn_conv_w'][l].astype(F32),
               _lane_row(p['gdn_a_log'][l], 4), _lane_row(p['gdn_dt_bias'][l], 4),
               _row(p['gdn_norm_w'][l]), nch(NCH_GDN))

    o_b = _retention(proj, cos, sin, _row(p['ret_norm_w'][l]), nch(NCH_RET))

    o_c = _ssd(proj, p['m2_conv_w'][l].astype(F32), _row(p['m2_conv_b'][l]),
               _lane_row(p['m2_a_log'][l], 0), _lane_row(p['m2_dt_bias'][l], 0),
               _row(jnp.repeat(p['m2_d'][l], M2_HEADDIM)), _row(p['m2_norm_w'][l]), nch(NCH_SSD))

    o_d = _rwkv(proj, _row(p['rw_mu'][l], 1792), _row(p['rw_w0'][l]),
                _pad_rows_at(p['rw_w_up'][l], 0), _row(p['rw_a0'][l]),
                _pad_rows_at(p['rw_a_up'][l], RW_W_LORA),
                _pad_rows_at(p['rw_g_up'][l], RW_W_LORA + RW_A_LORA),
                _row(p['rw_k_k'][l]), _row(p['rw_k_a'][l]), _row(p['rw_r_k'][l]),
                _row(p['rw_ln_w'][l]), _row(p['rw_ln_b'][l]), nch(NCH_RWKV))

    return _out_proj(h, (o_a, o_b, o_c, o_d), p['w_out'][l].astype(BF16))


def kernel(x, norm1_w, w_in, gdn_conv_w, gdn_a_log, gdn_dt_bias, gdn_norm_w, ret_norm_w, m2_conv_w, m2_conv_b, m2_a_log, m2_dt_bias, m2_d, m2_norm_w, rw_mu, rw_w0, rw_w_up, rw_a0, rw_a_up, rw_g_up, rw_k_k, rw_k_a, rw_r_k, rw_ln_w, rw_ln_b, w_out, norm2_w, w_ffn_up, w_ffn_down, final_norm_w):
    p = dict(norm1_w=norm1_w, w_in=w_in, gdn_conv_w=gdn_conv_w, gdn_a_log=gdn_a_log,
             gdn_dt_bias=gdn_dt_bias, gdn_norm_w=gdn_norm_w, ret_norm_w=ret_norm_w,
             m2_conv_w=m2_conv_w, m2_conv_b=m2_conv_b, m2_a_log=m2_a_log, m2_dt_bias=m2_dt_bias,
             m2_d=m2_d, m2_norm_w=m2_norm_w, rw_mu=rw_mu, rw_w0=rw_w0, rw_w_up=rw_w_up,
             rw_a0=rw_a0, rw_a_up=rw_a_up, rw_g_up=rw_g_up, rw_k_k=rw_k_k, rw_k_a=rw_k_a,
             rw_r_k=rw_r_k, rw_ln_w=rw_ln_w, rw_ln_b=rw_ln_b, w_out=w_out)
    bsz, t, d = x.shape
    depth = w_in.shape[0]
    cos, sin = _rotary_tables(t)
    outs = []
    for b in range(bsz):
        h = x[b]
        for l in range(depth):
            h = _token_mix(h, l, p, cos, sin)
            h = _ffn(h, norm2_w[l], w_ffn_up[l].astype(BF16), w_ffn_down[l].astype(BF16),
                     final_norm_w, final_norm=(l == depth - 1))
        outs.append(h)
    return jnp.stack(outs, axis=0)
```

```python
import functools
import math

import jax
import jax.numpy as jnp
from jax import lax
from jax.experimental import pallas as pl
from jax.experimental.pallas import tpu as pltpu

F32 = jnp.float32
BF16 = jnp.bfloat16

D_MODEL = 2048
D_FF = 4 * D_MODEL
CONV_W = 4
CHUNK = 64
EPS = 1e-6
ROPE_BASE = 10000.0
RWKV_LN_EPS = 64e-5

GDN_HEADS, GDN_DK, GDN_DV = 4, 128, 128
RET_HEADS, RET_DK, RET_DV = 4, 64, 128
M2_HEADS, M2_HEADDIM, M2_GROUPS, M2_STATE = 8, 64, 2, 128
RW_HEADS, RW_N = 8, 64
RW_W_LORA, RW_A_LORA, RW_G_LORA = 32, 32, 96
MIX_W = 512

_GDN0, _RET0, _M20, _RW0 = 0, 2056, 3592, 5136

P_PAD = 7168
_SRC_PIECES = (
    (_GDN0, 1536, 1536),
    (_RW0, 1536, 1536),
    (_M20 + 512, 1024, 1024),
    (_GDN0 + 1536, 512, 512),
    (_RET0, 512, 512),
    (_RET0 + 512, 512, 512),
    (_RET0 + 1024, 512, 512),
    (_M20, 512, 512),
    (_GDN0 + 2048, 8, 128),
    (_M20 + 1536, 8, 128),
    (_RW0 + 1536, 160, 256),
)

VMEM_LIMIT = 56 * 1024 * 1024

NN = (((1,), (0,)), ((), ()))
NT = (((1,), (1,)), ((), ()))
TN = (((0,), (0,)), ((), ()))


def _cparams(n_axes):
    return pltpu.CompilerParams(dimension_semantics=("arbitrary",) * n_axes,
                                vmem_limit_bytes=VMEM_LIMIT)


def _dg(a, b, dims=NN, prec=None):
    return lax.dot_general(a, b, dims, preferred_element_type=F32, precision=prec)


def _split(a):
    hi = a.astype(BF16)
    lo = (a - hi.astype(F32)).astype(BF16)
    return hi, lo


def _dot1(a, b, dims=NN):
    return _dg(a.astype(BF16), b.astype(BF16), dims)


def _split3(a):
    hi = a.astype(BF16)
    r1 = a - hi.astype(F32)
    mid = r1.astype(BF16)
    lo = (r1 - mid.astype(F32)).astype(BF16)
    return hi, mid, lo


def _dot_sel(a, b, dims=NN):
    if a.dtype == BF16:
        return sum(_dg(a, piece, dims) for piece in _split3(b))
    return sum(_dg(piece, b, dims) for piece in _split3(a))


def _sum_bcast(x, blk, pieces=1):
    outs = []
    for s in range(0, x.shape[1], 256):
        xs, bs = x[:, s:s + 256], blk[s:s + 256, s:s + 256]
        if pieces == 1:
            outs.append(_dg(xs.astype(BF16), bs))
        else:
            xh, xl = _split(xs)
            outs.append(_dg(xh, bs) + _dg(xl, bs))
    return jnp.concatenate(outs, axis=1)


def _rows_bcast(x, nch):
    return jnp.concatenate(
        [jnp.broadcast_to(x[(c + 1) * CHUNK - 1:(c + 1) * CHUNK, :], (CHUNK, x.shape[1])) for c in range(nch)],
        axis=0)


def _sigmoid(x):
    return 1.0 / (1.0 + jnp.exp(-x))


def _silu(x):
    return x * _sigmoid(x)


def _softplus(x):
    return jnp.maximum(x, 0.0) + jnp.log1p(jnp.exp(-jnp.abs(x)))


def _iota2(shape, axis):
    return lax.broadcasted_iota(jnp.int32, shape, axis)


def _chunk_masks():
    r = _iota2((CHUNK, CHUNK), 0)
    c = _iota2((CHUNK, CHUNK), 1)
    return r >= c, r > c, r == c


class _Pair:
    def __init__(self):
        lane = _iota2((CHUNK, 2 * CHUNK), 1)
        row = _iota2((CHUNK, 2 * CHUNK), 0)
        self.lo = lane < CHUNK
        col = jnp.where(self.lo, lane, lane - CHUNK)
        self.dist = (row - col).astype(F32)
        self.causal = row >= col
        self.strict = row > col
        self.eye = jnp.where(row == col, 1.0, 0.0).astype(F32)
        r2 = _iota2((2 * CHUNK, 2 * CHUNK), 0)
        c2 = _iota2((2 * CHUNK, 2 * CHUNK), 1)
        self.same_block = (r2 < CHUNK) == (c2 < CHUNK)

    def bd(self, x):
        z = jnp.zeros_like(x)
        return jnp.concatenate([jnp.where(self.lo, x, z), jnp.where(self.lo, z, x)], axis=0)

    def inverse(self, lows):
        ps = [self.eye - low for low in lows]
        curs = list(lows)
        n = 1
        while n < CHUNK:
            rhss = [self.bd(cur) for cur in curs]
            if n == 1:
                curs = [_dot1(cur, rhs) for cur, rhs in zip(curs, rhss)]
            elif 2 * n < CHUNK:
                outs = [_dot1(jnp.concatenate([cur, p], axis=0), rhs)
                        for cur, p, rhs in zip(curs, ps, rhss)]
                curs = [out[:CHUNK] for out in outs]
                ps = [p + out[CHUNK:] for p, out in zip(ps, outs)]
            else:
                ps = [p + _dot1(p, rhs) for p, rhs in zip(ps, rhss)]
            n *= 2
        return ps


def _norm_matmul_kernel(x_ref, nw_ref, w_ref, o_ref, a_ref):
    @pl.when(pl.program_id(1) == 0)
    def _():
        x = x_ref[...]
        y = x * lax.rsqrt(jnp.mean(x * x, axis=-1, keepdims=True) + EPS) * nw_ref[...]
        a_ref[...] = y.astype(BF16)

    o_ref[...] = jnp.dot(a_ref[...], w_ref[...], preferred_element_type=F32)


def _norm_matmul(x, nw, w, tm=1024, tn=1024):
    t, d = x.shape
    n = w.shape[1]
    tm = min(tm, t)
    return pl.pallas_call(
        _norm_matmul_kernel,
        out_shape=jax.ShapeDtypeStruct((t, n), F32),
        grid=(t // tm, n // tn),
        in_specs=[pl.BlockSpec((tm, d), lambda i, j: (i, 0)),
                  pl.BlockSpec((1, d), lambda i, j: (0, 0)),
                  pl.BlockSpec((d, tn), lambda i, j: (0, j))],
        out_specs=pl.BlockSpec((tm, tn), lambda i, j: (i, j)),
        scratch_shapes=[pltpu.VMEM((tm, d), BF16)],
        compiler_params=_cparams(2),
    )(x, nw.reshape(1, d), w)


def _out_proj_kernel(h_ref, oa_ref, ob_ref, oc_ref, od_ref, w_ref, o_ref):
    acc = h_ref[...]
    for idx, part in enumerate((oa_ref, ob_ref, oc_ref, od_ref)):
        acc = acc + jnp.dot(part[...].astype(BF16), w_ref[idx * MIX_W:(idx + 1) * MIX_W, :],
                            preferred_element_type=F32)
    o_ref[...] = acc


def _out_proj(h, parts, w, tm=256):
    t, d = h.shape
    tm = min(tm, t)
    part_spec = pl.BlockSpec((tm, MIX_W), lambda i: (i, 0))
    return pl.pallas_call(
        _out_proj_kernel,
        out_shape=jax.ShapeDtypeStruct((t, d), F32),
        grid=(t // tm,),
        in_specs=[pl.BlockSpec((tm, d), lambda i: (i, 0)),
                  part_spec, part_spec, part_spec, part_spec,
                  pl.BlockSpec((4 * MIX_W, d), lambda i: (0, 0))],
        out_specs=pl.BlockSpec((tm, d), lambda i: (i, 0)),
        compiler_params=_cparams(1),
    )(h, *parts, w)


def _ffn_kernel(h_ref, nw_ref, wu_ref, wd_ref, fw_ref, o_ref, a_ref, acc_ref, *, final_norm):
    f = pl.program_id(1)

    @pl.when(f == 0)
    def _():
        x = h_ref[...]
        y = x * lax.rsqrt(jnp.mean(x * x, axis=-1, keepdims=True) + EPS) * nw_ref[...]
        a_ref[...] = y.astype(BF16)
        acc_ref[...] = jnp.zeros_like(acc_ref)

    u = jnp.dot(a_ref[...], wu_ref[...], preferred_element_type=F32)
    s = jnp.square(jnp.maximum(u, 0.0)).astype(BF16)
    acc_ref[...] += jnp.dot(s, wd_ref[...], preferred_element_type=F32)

    @pl.when(f == pl.num_programs(1) - 1)
    def _():
        y = h_ref[...] + acc_ref[...]
        if final_norm:
            y = y * lax.rsqrt(jnp.mean(y * y, axis=-1, keepdims=True) + EPS) * fw_ref[...]
        o_ref[...] = y


def _ffn(h, nw, wu, wd, fw, final_norm, tm=512, tf=512):
    t, d = h.shape
    ff = wu.shape[1]
    tm = min(tm, t)
    return pl.pallas_call(
        functools.partial(_ffn_kernel, final_norm=final_norm),
        out_shape=jax.ShapeDtypeStruct((t, d), F32),
        grid=(t // tm, ff // tf),
        in_specs=[pl.BlockSpec((tm, d), lambda i, f: (i, 0)),
                  pl.BlockSpec((1, d), lambda i, f: (0, 0)),
                  pl.BlockSpec((d, tf), lambda i, f: (0, f)),
                  pl.BlockSpec((tf, d), lambda i, f: (f, 0)),
                  pl.BlockSpec((1, d), lambda i, f: (0, 0))],
        out_specs=pl.BlockSpec((tm, d), lambda i, f: (i, 0)),
        scratch_shapes=[pltpu.VMEM((tm, d), BF16), pltpu.VMEM((tm, d), F32)],
        compiler_params=_cparams(2),
    )(h, nw.reshape(1, d), wu, wd, fw.reshape(1, d))


def _stage_with_halo(ext_ref, pieces, rows):
    @pl.when(pl.program_id(0) == 0)
    def _():
        ext_ref[0:8, :] = jnp.zeros((8, ext_ref.shape[1]), F32)

    col = 0
    for ref in pieces:
        w = ref.shape[1]
        ext_ref[8:8 + rows, col:col + w] = ref[...]
        col += w


def _keep_halo(ext_ref, rows):
    ext_ref[0:8, :] = ext_ref[rows:rows + 8, :]


def _causal_conv(ext_ref, cw, rows):
    acc = ext_ref[pl.ds(8, rows), :] * cw[3:4, :]
    for i in range(CONV_W - 1):
        acc = acc + ext_ref[pl.ds(5 + i, rows), :] * cw[i:i + 1, :]
    return acc


def _const_spec(shape):
    return pl.BlockSpec(shape, lambda i: (0,) * len(shape))


def _block_diag_ones(n, block, lower=False, dtype=F32):
    r = jnp.arange(n)[:, None]
    c = jnp.arange(n)[None, :]
    m = (r // block) == (c // block)
    if lower:
        m = m & (r >= c)
    return m.astype(dtype)


def _gdn_kernel(qkv_ref, z_ref, gate_ref, cw_ref, alog_ref, dtb_ref, nw_ref, btri_ref, eye_ref, o_ref,
                ext_ref, s_ref, *, nch):
    rows = nch * CHUNK

    @pl.when(pl.program_id(0) == 0)
    def _():
        s_ref[...] = jnp.zeros_like(s_ref)

    _stage_with_halo(ext_ref, (qkv_ref,), rows)
    qkv = _silu(_causal_conv(ext_ref, cw_ref[...], rows))
    _keep_halo(ext_ref, rows)

    causal, strict, _ = _chunk_masks()
    pair = _Pair()

    gt = gate_ref[...]
    beta = _sigmoid(gt)
    log_a = -jnp.exp(alog_ref[...]) * _softplus(gt + dtb_ref[...])
    g_all = _dot_sel(btri_ref[...], log_a)
    g_t = _dot_sel(g_all, eye_ref[...], TN)
    z = z_ref[...]
    nw = nw_ref[...]

    qs, ks, vs = [], [], []
    for h in range(GDN_HEADS):
        qh = qkv[:, h * GDN_DK:(h + 1) * GDN_DK]
        kh = qkv[:, 512 + h * GDN_DK:512 + (h + 1) * GDN_DK]
        qs.append(qh * lax.rsqrt(jnp.sum(qh * qh, axis=-1, keepdims=True) + EPS) * (GDN_DK ** -0.5))
        ks.append(kh * lax.rsqrt(jnp.sum(kh * kh, axis=-1, keepdims=True) + EPS))
        vs.append(qkv[:, 1024 + h * GDN_DV:1024 + (h + 1) * GDN_DV])

    combos = [(c, h) for c in range(nch) for h in range(GDN_HEADS)]
    rsl = lambda c: slice(c * CHUNK, (c + 1) * CHUNK)
    bcol = {(c, h): beta[rsl(c), h:h + 1] for c, h in combos}
    gcol = {(c, h): g_all[rsl(c), 4 + h:5 + h] for c, h in combos}
    kq = {(c, h): _dot1(jnp.concatenate([ks[h][rsl(c)], qs[h][rsl(c)]], axis=0), ks[h][rsl(c)], NT)
          for c, h in combos}
    lows, qks = {}, {}
    for c, h in combos:
        grow = g_t[4 + h:5 + h, c * CHUNK:(c + 1) * CHUNK]
        gamma = jnp.exp(jnp.where(causal, gcol[c, h] - grow, -jnp.inf))
        lows[c, h] = jnp.where(strict, kq[c, h][:CHUNK] * gamma * bcol[c, h], 0.0)
        qks[c, h] = kq[c, h][CHUNK:] * gamma
    pairs = [(c, p) for c in range(nch) for p in range(GDN_HEADS // 2)]
    tps = pair.inverse([jnp.concatenate([lows[c, 2 * p], lows[c, 2 * p + 1]], axis=1) for c, p in pairs])
    tinv = {}
    for (c, p), tp in zip(pairs, tps):
        tinv[c, 2 * p] = tp[:, :CHUNK]
        tinv[c, 2 * p + 1] = tp[:, CHUNK:]
    egc = {ch: jnp.exp(gcol[ch]) for ch in combos}
    glast = {ch: gcol[ch][CHUNK - 1:CHUNK, :] for ch in combos}
    wu = {(c, h): _dot1(tinv[c, h], jnp.concatenate(
        [ks[h][rsl(c)] * (bcol[c, h] * egc[c, h]), vs[h][rsl(c)] * bcol[c, h]], axis=1)) for c, h in combos}
    qwu = {ch: _dot1(qks[ch], wu[ch]) for ch in combos}
    kwu = {(c, h): _dot1(ks[h][rsl(c)] * jnp.exp(glast[c, h] - gcol[c, h]), wu[c, h], TN) for c, h in combos}
    for c in range(nch):
        rs = rsl(c)
        s0 = [s_ref[h] if c == 0 else s_new[h] for h in range(GDN_HEADS)]
        s_new = []
        for h in range(GDN_HEADS):
            sl = slice(h * GDN_DV, (h + 1) * GDN_DV)
            q_eff = qs[h][rs] * egc[c, h] - qwu[c, h][:, :GDN_DK]
            o = _dot1(q_eff, s0[h]) + qwu[c, h][:, GDN_DK:]
            s_new.append(s0[h] * jnp.exp(glast[c, h]) - _dot1(kwu[c, h][:, :GDN_DK], s0[h])
                         + kwu[c, h][:, GDN_DK:])
            o = o * lax.rsqrt(jnp.mean(o * o, axis=-1, keepdims=True) + EPS) * nw
            o_ref[rs, sl] = o * _silu(z[rs, sl])
    for h in range(GDN_HEADS):
        s_ref[h] = s_new[h]


def _gdn(proj, cw, alog, dtb, nw, nch):
    t = proj.shape[0]
    rows = nch * CHUNK
    return pl.pallas_call(
        functools.partial(_gdn_kernel, nch=nch),
        out_shape=jax.ShapeDtypeStruct((t, MIX_W), F32),
        grid=(t // rows,),
        in_specs=[pl.BlockSpec((rows, 1536), lambda i: (i, 0)),
                  pl.BlockSpec((rows, 512), lambda i: (i, 8)),
                  pl.BlockSpec((rows, 128), lambda i: (i, 52)),
                  _const_spec((CONV_W, 1536)),
                  _const_spec((1, 128)), _const_spec((1, 128)), _const_spec((1, 128)),
                  _const_spec((rows, rows)), _const_spec((rows, rows))],
        out_specs=pl.BlockSpec((rows, MIX_W), lambda i: (i, 0)),
        scratch_shapes=[pltpu.VMEM((rows + 8, 1536), F32),
                        pltpu.VMEM((GDN_HEADS, GDN_DK, GDN_DV), F32)],
        compiler_params=_cparams(1),
    )(proj, proj, proj, cw, alog, dtb, nw, _block_diag_ones(rows, CHUNK, lower=True, dtype=BF16),
      jnp.eye(rows, dtype=BF16))


def _ret_kernel(qk_ref, v_ref, g_ref, cos_ref, sin_ref, nw_ref, o_ref, r_ref, *, nch):
    @pl.when(pl.program_id(0) == 0)
    def _():
        r_ref[...] = jnp.zeros_like(r_ref)

    rows = nch * CHUNK
    qk = qk_ref[...]
    cos = cos_ref[...]
    sin = sin_ref[...]
    even = (_iota2((rows, 256), 1) % 2) == 0

    def rot(x):
        partner = jnp.where(even, pltpu.roll(x, 255, 1), pltpu.roll(x, 1, 1))
        return x * cos + partner * sin

    q = rot(qk[:, 0:256])
    k = rot(qk[:, 256:512]) * (RET_DK ** -0.5)
    v = v_ref[...]
    gate = g_ref[...]
    nw = nw_ref[...]

    pair = _Pair()
    pos = _iota2((CHUNK, 2 * CHUNK), 0).astype(F32)
    zeros_v = jnp.zeros((CHUNK, RET_DV), F32)
    r2 = _iota2((2 * RET_DK, 2 * RET_DV), 0)
    c2 = _iota2((2 * RET_DK, 2 * RET_DV), 1)
    state_block = (r2 < RET_DK) == (c2 < RET_DV)

    for p in range(RET_HEADS // 2):
        lg0 = math.log1p(-(2.0 ** (-5.0 - 2 * p)))
        lg1 = math.log1p(-(2.0 ** (-5.0 - (2 * p + 1))))
        lg = jnp.where(pair.lo, lg0, lg1)
        dmask = jnp.exp(jnp.where(pair.causal, pair.dist * lg, -jnp.inf))
        k_dec = jnp.exp((CHUNK - 1.0 - pos) * lg)
        q_dec = jnp.exp((pos + 1.0) * lg)
        lg_v = jnp.where(_iota2((1, 2 * RET_DV), 1) < RET_DV, lg0, lg1)
        chunk_decay = jnp.exp(CHUNK * lg_v)
        ls = slice(p * 2 * RET_DK, (p + 1) * 2 * RET_DK)
        vsl = slice(p * 2 * RET_DV, (p + 1) * 2 * RET_DV)
        rsl = lambda c: slice(c * CHUNK, (c + 1) * CHUNK)
        sc = [_dot1(q[rsl(c), ls], pair.bd(k[rsl(c), ls]), NT) * dmask for c in range(nch)]
        upd = [_dot1(k[rsl(c), ls] * k_dec, v[rsl(c), vsl], TN) for c in range(nch)]
        r_start = []
        r_cur = r_ref[p]
        for c in range(nch):
            r_start.append(r_cur)
            r_cur = r_cur * chunk_decay + jnp.where(state_block, upd[c], 0.0)
        r_ref[p] = r_cur
        for c in range(nch):
            rs = rsl(c)
            vp = v[rs, vsl]
            v_bd = jnp.concatenate(
                [jnp.concatenate([vp[:, :RET_DV], zeros_v], axis=1),
                 jnp.concatenate([zeros_v, vp[:, RET_DV:]], axis=1)], axis=0)
            o = _dot1(sc[c], v_bd) + _dot1(q[rs, ls] * q_dec, r_start[c])
            for hh in range(2):
                sl = slice(vsl.start + hh * RET_DV, vsl.start + (hh + 1) * RET_DV)
                oh = o[:, hh * RET_DV:(hh + 1) * RET_DV]
                mu = jnp.mean(oh, axis=-1, keepdims=True)
                oc = oh - mu
                oh = oc * lax.rsqrt(jnp.mean(oc * oc, axis=-1, keepdims=True) + EPS) * nw[:, sl]
                o_ref[rs, sl] = oh * _silu(gate[rs, sl])


def _retention(proj, cos, sin, nw, nch):
    t = proj.shape[0]
    rows = nch * CHUNK
    return pl.pallas_call(
        functools.partial(_ret_kernel, nch=nch),
        out_shape=jax.ShapeDtypeStruct((t, MIX_W), F32),
        grid=(t // rows,),
        in_specs=[pl.BlockSpec((rows, 512), lambda i: (i, 9)),
                  pl.BlockSpec((rows, 512), lambda i: (i, 10)),
                  pl.BlockSpec((rows, 512), lambda i: (i, 11)),
                  pl.BlockSpec((rows, 256), lambda i: (i, 0)),
                  pl.BlockSpec((rows, 256), lambda i: (i, 0)),
                  _const_spec((1, 512))],
        out_specs=pl.BlockSpec((rows, MIX_W), lambda i: (i, 0)),
        scratch_shapes=[pltpu.VMEM((RET_HEADS // 2, 2 * RET_DK, 2 * RET_DV), F32)],
        compiler_params=_cparams(1),
    )(proj, proj, proj, cos, sin, nw)


def _ssd_kernel(z_ref, xbc_ref, dt_ref, cw_ref, cb_ref, alog_ref, dtb_ref, d_ref, nw_ref,
                btri_ref, eye_ref, expand_ref, o_ref, ext_ref, h_ref, *, nch):
    rows = nch * CHUNK

    @pl.when(pl.program_id(0) == 0)
    def _():
        h_ref[...] = jnp.zeros_like(h_ref)

    _stage_with_halo(ext_ref, (xbc_ref,), rows)
    xbc = _silu(_causal_conv(ext_ref, cw_ref[...], rows) + cb_ref[...])
    _keep_halo(ext_ref, rows)
    x = xbc[:, 0:512]

    pair = _Pair()
    dt = _softplus(dt_ref[...] + dtb_ref[...])
    g_all = _dot_sel(btri_ref[...], dt * (-jnp.exp(alog_ref[...])))
    g_t = _dot_sel(g_all, eye_ref[...], TN)
    wide = _dot_sel(jnp.concatenate([g_all, dt], axis=0), expand_ref[...])
    g_w, dt_w = wide[:rows], wide[rows:]
    gtot_w = _rows_bcast(g_w, nch)
    xdt = x * dt_w
    xtail = xdt * jnp.exp(gtot_w - g_w)
    eg_w = jnp.exp(g_w)
    etot_w = jnp.exp(gtot_w)
    z = z_ref[...]
    d_w = d_ref[...]
    nw = nw_ref[...]
    gw = M2_HEADS // M2_GROUPS * M2_HEADDIM

    combos = [(c, gi) for c in range(nch) for gi in range(M2_GROUPS)]
    rsl = lambda c: slice(c * CHUNK, (c + 1) * CHUNK)
    bg = {(c, gi): xbc[rsl(c), 512 + gi * M2_STATE:512 + (gi + 1) * M2_STATE] for c, gi in combos}
    cg = {(c, gi): xbc[rsl(c), 768 + gi * M2_STATE:768 + (gi + 1) * M2_STATE] for c, gi in combos}
    cb2 = {k: _dot1(cg[k], jnp.concatenate([bg[k], bg[k]], axis=0), NT) for k in combos}
    upd = {(c, gi): _dot1(bg[c, gi], xtail[rsl(c), gi * gw:(gi + 1) * gw], TN) for c, gi in combos}
    y_in = {}
    for c, gi in combos:
        for pp in range(2):
            ha = gi * 4 + 2 * pp
            gcol = jnp.where(pair.lo, g_all[rsl(c), ha:ha + 1], g_all[rsl(c), ha + 1:ha + 2])
            grow = jnp.concatenate([g_t[ha:ha + 1, rsl(c)], g_t[ha + 1:ha + 2, rsl(c)]], axis=1)
            decay = jnp.exp(jnp.where(pair.causal, gcol - grow, -jnp.inf))
            y_in[c, gi, pp] = _dot1(cb2[c, gi] * decay,
                                    pair.bd(xdt[rsl(c), ha * M2_HEADDIM:(ha + 2) * M2_HEADDIM]))
    h_start = {}
    for gi in range(M2_GROUPS):
        h = h_ref[gi]
        for c in range(nch):
            h_start[c, gi] = h
            h = h * etot_w[c * CHUNK:c * CHUNK + 1, gi * gw:(gi + 1) * gw] + upd[c, gi]
        h_ref[gi] = h
    for c, gi in combos:
        rs, gs = rsl(c), slice(gi * gw, (gi + 1) * gw)
        y = jnp.concatenate([y_in[c, gi, 0], y_in[c, gi, 1]], axis=1)
        y = y + _dot1(cg[c, gi], h_start[c, gi]) * eg_w[rs, gs]
        y = (y + d_w[:, gs] * x[rs, gs]) * _silu(z[rs, gs])
        y = y * lax.rsqrt(jnp.mean(y * y, axis=-1, keepdims=True) + EPS) * nw[:, gs]
        o_ref[rs, gs] = y


def _head_expand_matrix(n_heads, width):
    r = jnp.arange(128)[:, None]
    c = jnp.arange(n_heads * width)[None, :]
    return (r == c // width).astype(BF16)


def _ssd(proj, cw, cb, alog, dtb, d_w, nw, nch):
    t = proj.shape[0]
    rows = nch * CHUNK
    return pl.pallas_call(
        functools.partial(_ssd_kernel, nch=nch),
        out_shape=jax.ShapeDtypeStruct((t, MIX_W), F32),
        grid=(t // rows,),
        in_specs=[pl.BlockSpec((rows, 512), lambda i: (i, 12)),
                  pl.BlockSpec((rows, 1024), lambda i: (i, 3)),
                  pl.BlockSpec((rows, 128), lambda i: (i, 53)),
                  _const_spec((CONV_W, 1024)),
                  _const_spec((1, 1024)), _const_spec((1, 128)), _const_spec((1, 128)),
                  _const_spec((1, 512)), _const_spec((1, 512)),
                  _const_spec((rows, rows)), _const_spec((rows, rows)), _const_spec((128, 512))],
        out_specs=pl.BlockSpec((rows, MIX_W), lambda i: (i, 0)),
        scratch_shapes=[pltpu.VMEM((rows + 8, 1024), F32),
                        pltpu.VMEM((M2_GROUPS, M2_STATE, 256), F32)],
        compiler_params=_cparams(1),
    )(proj, proj, proj, cw, cb, alog, dtb, d_w, nw,
      _block_diag_ones(rows, CHUNK, lower=True, dtype=BF16), jnp.eye(rows, dtype=BF16),
      _head_expand_matrix(M2_HEADS, M2_HEADDIM))


def _rwkv_kernel(rkv_ref, lora_ref, mu_ref, w0_ref, wup_ref, a0_ref, aup_ref, gup_ref,
                 kk_ref, ka_ref, rk_ref, lnw_ref, lnb_ref, btri_ref, hblk_ref,
                 o_ref, ext_ref, s_ref, *, nch):
    rows = nch * CHUNK

    @pl.when(pl.program_id(0) == 0)
    def _():
        s_ref[...] = jnp.zeros_like(s_ref)

    _stage_with_halo(ext_ref, (rkv_ref, lora_ref), rows)
    cur = ext_ref[pl.ds(8, rows), :]
    prev = ext_ref[pl.ds(7, rows), :]
    _keep_halo(ext_ref, rows)
    mixed = cur + (prev - cur) * mu_ref[...]
    r = mixed[:, 0:512]
    k = mixed[:, 512:1024]
    v = mixed[:, 1024:1536]
    lora = mixed[:, 1536:1792]

    w_raw = -_softplus(-(w0_ref[...] + _dot1(jnp.tanh(lora), wup_ref[...]))) - 0.5
    log_d = -jnp.exp(w_raw)
    a = _sigmoid(a0_ref[...] + _dot1(lora, aup_ref[...]))
    gate = _dot1(_sigmoid(lora), gup_ref[...])
    hblk = hblk_ref[...]

    g_in = _dot_sel(btri_ref[...], log_d)
    g_tot = _rows_bcast(g_in, nch)
    e_in = jnp.exp(g_in)
    e_neg = jnp.exp(-g_in)
    e_ex = jnp.exp(g_in - log_d)
    e_tail = jnp.exp(g_tot - g_in)
    e_end = jnp.exp(g_tot)

    kk = k * kk_ref[...]
    kk = kk * lax.rsqrt(_sum_bcast(kk * kk, hblk) + EPS)
    k_mod = k * (1.0 + (a - 1.0) * ka_ref[...])
    a_vec = -(a * kk)
    r_t = r * e_in
    b_t = kk * e_ex
    k_t = k_mod * e_neg
    a_t = a_vec * e_neg
    k_c = k_mod * e_tail
    a_c = a_vec * e_tail

    pair = _Pair()
    npair = RW_HEADS // 2
    combos = [(c, j) for c in range(nch) for j in range(npair)]
    cut = lambda arr, cj: arr[cj[0] * CHUNK:(cj[0] + 1) * CHUNK, cj[1] * 2 * RW_N:(cj[1] + 1) * 2 * RW_N]
    lhs = {cj: jnp.concatenate([cut(b_t, cj), cut(r_t, cj)], axis=0) for cj in combos}
    x1 = {cj: _dot1(lhs[cj], pair.bd(cut(k_t, cj)), NT) for cj in combos}
    x2 = {cj: _dot1(lhs[cj], pair.bd(cut(a_t, cj)), NT) for cj in combos}
    a_ra = {cj: jnp.where(pair.causal, x2[cj][CHUNK:], 0.0) for cj in combos}
    tinv = dict(zip(combos, pair.inverse([-jnp.where(pair.strict, x2[cj][:CHUNK], 0.0) for cj in combos])))
    av = {cj: _dot1(jnp.concatenate([jnp.where(pair.strict, x1[cj][:CHUNK], 0.0),
                                     jnp.where(pair.causal, x1[cj][CHUNK:], 0.0)], axis=0),
                    pair.bd(cut(v, cj))) for cj in combos}
    tz = {cj: _dot1(tinv[cj], jnp.concatenate([pair.bd(cut(b_t, cj)), pair.bd(av[cj][:CHUNK])], axis=1))
          for cj in combos}
    az = {cj: _dot1(a_ra[cj], jnp.concatenate([pair.bd(tz[cj][:, :2 * RW_N]), pair.bd(tz[cj][:, 2 * RW_N:])],
                                              axis=1)) for cj in combos}
    r_eff = {cj: cut(r_t, cj) + az[cj][:, :2 * RW_N] for cj in combos}
    y0 = {cj: av[cj][CHUNK:] + az[cj][:, 2 * RW_N:] for cj in combos}
    p_low = {cj: jnp.where(pair.same_block, _dot1(tz[cj][:, :2 * RW_N], cut(a_c, cj), TN), 0.0) for cj in combos}
    q_mat = {cj: jnp.where(pair.same_block,
                           _dot1(jnp.concatenate([cut(v, cj), tz[cj][:, 2 * RW_N:]], axis=0),
                                 jnp.concatenate([cut(k_c, cj), cut(a_c, cj)], axis=0), TN), 0.0)
             for cj in combos}
    y_rows = []
    for c in range(nch):
        s0 = [s_ref[j] if c == 0 else s_new[j] for j in range(npair)]
        y_rows.append(jnp.concatenate(
            [_dot1(r_eff[c, j], s0[j], NT) + y0[c, j] for j in range(npair)], axis=1))
        s_new = [s0[j] * e_end[c * CHUNK:c * CHUNK + 1, j * 2 * RW_N:(j + 1) * 2 * RW_N]
                 + _dot1(s0[j], p_low[c, j]) + q_mat[c, j] for j in range(npair)]
    for j in range(npair):
        s_ref[j] = s_new[j]
    y = jnp.concatenate(y_rows, axis=0)

    inv_n = 1.0 / RW_N
    mu = _sum_bcast(y, hblk, pieces=2) * inv_n
    yc = y - mu
    var = _sum_bcast(yc * yc, hblk) * inv_n
    y = yc * lax.rsqrt(var + RWKV_LN_EPS) * lnw_ref[...] + lnb_ref[...]
    y = y + _sum_bcast(r * k_mod * rk_ref[...], hblk) * v
    o_ref[...] = y * gate


def _rwkv(proj, mu, w0, wup, a0, aup, gup, kk, ka, rk, lnw, lnb, nch):
    t = proj.shape[0]
    rows = nch * CHUNK
    row = lambda n: _const_spec((1, n))
    mat = _const_spec((256, 512))
    return pl.pallas_call(
        functools.partial(_rwkv_kernel, nch=nch),
        out_shape=jax.ShapeDtypeStruct((t, MIX_W), F32),
        grid=(t // rows,),
        in_specs=[pl.BlockSpec((rows, 1536), lambda i: (i, 1)),
                  pl.BlockSpec((rows, 256), lambda i: (i, 27)),
                  row(1792), row(512), mat, row(512), mat, mat,
                  row(512), row(512), row(512), row(512), row(512),
                  _const_spec((rows, rows)), _const_spec((512, 512))],
        out_specs=pl.BlockSpec((rows, MIX_W), lambda i: (i, 0)),
        scratch_shapes=[pltpu.VMEM((rows + 8, 1792), F32),
                        pltpu.VMEM((RW_HEADS // 2, 2 * RW_N, 2 * RW_N), F32)],
        compiler_params=_cparams(1),
    )(proj, proj, mu, w0, wup, a0, aup, gup, kk, ka, rk, lnw, lnb,
      _block_diag_ones(rows, CHUNK, lower=True, dtype=BF16), _block_diag_ones(MIX_W, RW_N, dtype=BF16))


def _pad_cols(a, width):
    return jnp.pad(a, ((0, 0), (0, width - a.shape[1])))


def _layout_w_in(w):
    cols = [_pad_cols(w[:, s:s + n], p) for s, n, p in _SRC_PIECES]
    return jnp.concatenate(cols, axis=1).astype(BF16)


def _row(v, width=None):
    v = v.reshape(1, -1).astype(F32)
    return v if width is None else _pad_cols(v, width)


def _lane_row(v, offset):
    return jnp.pad(v.reshape(1, -1).astype(F32), ((0, 0), (offset, 128 - offset - v.shape[0])))


def _pad_rows_at(m, offset, rows=256):
    return jnp.pad(m.astype(F32), ((offset, rows - offset - m.shape[0]), (0, 0)))


def _rotary_tables(t):
    theta = 1.0 / (ROPE_BASE ** jnp.linspace(0.0, 1.0, RET_DK // 2, dtype=F32))
    ang = jnp.arange(t, dtype=F32)[:, None] * theta
    cos = jnp.repeat(jnp.cos(ang), 2, axis=1)
    sin = jnp.stack([-jnp.sin(ang), jnp.sin(ang)], axis=-1).reshape(t, RET_DK)
    return jnp.tile(cos, (1, RET_HEADS)), jnp.tile(sin, (1, RET_HEADS))


NCH_GDN, NCH_RET, NCH_SSD, NCH_RWKV = 4, 4, 4, 4


def _token_mix(h, l, p, cos, sin):
    t = h.shape[0]
    nch = lambda n: min(n, t // CHUNK)
    proj = _norm_matmul(h, p['norm1_w'][l], _layout_w_in(p['w_in'][l]))

    o_a = _gdn(proj, p['gdn_conv_w'][l].astype(F32),
               _lane_row(p['gdn_a_log'][l], 4), _lane_row(p['gdn_dt_bias'][l], 4),
               _row(p['gdn_norm_w'][l]), nch(NCH_GDN))

    o_b = _retention(proj, cos, sin, _row(p['ret_norm_w'][l]), nch(NCH_RET))

    o_c = _ssd(proj, p['m2_conv_w'][l].astype(F32), _row(p['m2_conv_b'][l]),
               _lane_row(p['m2_a_log'][l], 0), _lane_row(p['m2_dt_bias'][l], 0),
               _row(jnp.repeat(p['m2_d'][l], M2_HEADDIM)), _row(p['m2_norm_w'][l]), nch(NCH_SSD))

    o_d = _rwkv(proj, _row(p['rw_mu'][l], 1792), _row(p['rw_w0'][l]),
                _pad_rows_at(p['rw_w_up'][l], 0), _row(p['rw_a0'][l]),
                _pad_rows_at(p['rw_a_up'][l], RW_W_LORA),
                _pad_rows_at(p['rw_g_up'][l], RW_W_LORA + RW_A_LORA),
                _row(p['rw_k_k'][l]), _row(p['rw_k_a'][l]), _row(p['rw_r_k'][l]),
                _row(p['rw_ln_w'][l]), _row(p['rw_ln_b'][l]), nch(NCH_RWKV))

    return _out_proj(h, (o_a, o_b, o_c, o_d), p['w_out'][l].astype(BF16))


def kernel(x, norm1_w, w_in, gdn_conv_w, gdn_a_log, gdn_dt_bias, gdn_norm_w, ret_norm_w, m2_conv_w, m2_conv_b, m2_a_log, m2_dt_bias, m2_d, m2_norm_w, rw_mu, rw_w0, rw_w_up, rw_a0, rw_a_up, rw_g_up, rw_k_k, rw_k_a, rw_r_k, rw_ln_w, rw_ln_b, w_out, norm2_w, w_ffn_up, w_ffn_down, final_norm_w):
    p = dict(norm1_w=norm1_w, w_in=w_in, gdn_conv_w=gdn_conv_w, gdn_a_log=gdn_a_log,
             gdn_dt_bias=gdn_dt_bias, gdn_norm_w=gdn_norm_w, ret_norm_w=ret_norm_w,
             m2_conv_w=m2_conv_w, m2_conv_b=m2_conv_b, m2_a_log=m2_a_log, m2_dt_bias=m2_dt_bias,
             m2_d=m2_d, m2_norm_w=m2_norm_w, rw_mu=rw_mu, rw_w0=rw_w0, rw_w_up=rw_w_up,
             rw_a0=rw_a0, rw_a_up=rw_a_up, rw_g_up=rw_g_up, rw_k_k=rw_k_k, rw_k_a=rw_k_a,
             rw_r_k=rw_r_k, rw_ln_w=rw_ln_w, rw_ln_b=rw_ln_b, w_out=w_out)
    bsz, t, d = x.shape
    depth = w_in.shape[0]
    cos, sin = _rotary_tables(t)
    outs = []
    for b in range(bsz):
        h = x[b]
        for l in range(depth):
            h = _token_mix(h, l, p, cos, sin)
            h = _ffn(h, norm2_w[l], w_ffn_up[l].astype(BF16), w_ffn_down[l].astype(BF16),
                     final_norm_w, final_norm=(l == depth - 1))
        outs.append(h)
    return jnp.stack(outs, axis=0)
```

```python
import functools
import math

import jax
import jax.numpy as jnp
from jax import lax
from jax.experimental import pallas as pl
from jax.experimental.pallas import tpu as pltpu

F32 = jnp.float32
BF16 = jnp.bfloat16

D_MODEL = 2048
D_FF = 4 * D_MODEL
CONV_W = 4
CHUNK = 64
EPS = 1e-6
ROPE_BASE = 10000.0
RWKV_LN_EPS = 64e-5

GDN_HEADS, GDN_DK, GDN_DV = 4, 128, 128
RET_HEADS, RET_DK, RET_DV = 4, 64, 128
M2_HEADS, M2_HEADDIM, M2_GROUPS, M2_STATE = 8, 64, 2, 128
RW_HEADS, RW_N = 8, 64
RW_W_LORA, RW_A_LORA, RW_G_LORA = 32, 32, 96
MIX_W = 512

_GDN0, _RET0, _M20, _RW0 = 0, 2056, 3592, 5136

P_PAD = 7168
_SRC_PIECES = (
    (_GDN0, 1536, 1536),
    (_RW0, 1536, 1536),
    (_M20 + 512, 1024, 1024),
    (_GDN0 + 1536, 512, 512),
    (_RET0, 512, 512),
    (_RET0 + 512, 512, 512),
    (_RET0 + 1024, 512, 512),
    (_M20, 512, 512),
    (_GDN0 + 2048, 8, 128),
    (_M20 + 1536, 8, 128),
    (_RW0 + 1536, 160, 256),
)

VMEM_LIMIT = 56 * 1024 * 1024

NN = (((1,), (0,)), ((), ()))
NT = (((1,), (1,)), ((), ()))
TN = (((0,), (0,)), ((), ()))


def _cparams(n_axes):
    return pltpu.CompilerParams(dimension_semantics=("arbitrary",) * n_axes,
                                vmem_limit_bytes=VMEM_LIMIT)


def _dg(a, b, dims=NN, prec=None):
    return lax.dot_general(a, b, dims, preferred_element_type=F32, precision=prec)


def _split(a):
    hi = a.astype(BF16)
    lo = (a - hi.astype(F32)).astype(BF16)
    return hi, lo


def _dot1(a, b, dims=NN):
    return _dg(a.astype(BF16), b.astype(BF16), dims)


def _split3(a):
    hi = a.astype(BF16)
    r1 = a - hi.astype(F32)
    mid = r1.astype(BF16)
    lo = (r1 - mid.astype(F32)).astype(BF16)
    return hi, mid, lo


def _dot_sel(a, b, dims=NN):
    if a.dtype == BF16:
        return sum(_dg(a, piece, dims) for piece in _split3(b))
    return sum(_dg(piece, b, dims) for piece in _split3(a))


def _sum_bcast(x, blk, pieces=1):
    outs = []
    for s in range(0, x.shape[1], 256):
        xs, bs = x[:, s:s + 256], blk[s:s + 256, s:s + 256]
        if pieces == 1:
            outs.append(_dg(xs.astype(BF16), bs))
        else:
            xh, xl = _split(xs)
            outs.append(_dg(xh, bs) + _dg(xl, bs))
    return jnp.concatenate(outs, axis=1)


def _rows_bcast(x, nch):
    return jnp.concatenate(
        [jnp.broadcast_to(x[(c + 1) * CHUNK - 1:(c + 1) * CHUNK, :], (CHUNK, x.shape[1])) for c in range(nch)],
        axis=0)


def _sigmoid(x):
    return 1.0 / (1.0 + jnp.exp(-x))


def _silu(x):
    return x * _sigmoid(x)


def _softplus(x):
    return jnp.maximum(x, 0.0) + jnp.log1p(jnp.exp(-jnp.abs(x)))


def _iota2(shape, axis):
    return lax.broadcasted_iota(jnp.int32, shape, axis)


def _chunk_masks():
    r = _iota2((CHUNK, CHUNK), 0)
    c = _iota2((CHUNK, CHUNK), 1)
    return r >= c, r > c, r == c


class _Pair:
    def __init__(self):
        lane = _iota2((CHUNK, 2 * CHUNK), 1)
        row = _iota2((CHUNK, 2 * CHUNK), 0)
        self.lo = lane < CHUNK
        col = jnp.where(self.lo, lane, lane - CHUNK)
        self.dist = (row - col).astype(F32)
        self.causal = row >= col
        self.strict = row > col
        self.eye = jnp.where(row == col, 1.0, 0.0).astype(F32)
        r2 = _iota2((2 * CHUNK, 2 * CHUNK), 0)
        c2 = _iota2((2 * CHUNK, 2 * CHUNK), 1)
        self.same_block = (r2 < CHUNK) == (c2 < CHUNK)

    def bd(self, x):
        z = jnp.zeros_like(x)
        return jnp.concatenate([jnp.where(self.lo, x, z), jnp.where(self.lo, z, x)], axis=0)

    def inverse(self, lows):
        ps = [self.eye - low for low in lows]
        curs = list(lows)
        n = 1
        while n < CHUNK:
            rhss = [self.bd(cur) for cur in curs]
            if n == 1:
                curs = [_dot1(cur, rhs) for cur, rhs in zip(curs, rhss)]
            elif 2 * n < CHUNK:
                outs = [_dot1(jnp.concatenate([cur, p], axis=0), rhs)
                        for cur, p, rhs in zip(curs, ps, rhss)]
                curs = [out[:CHUNK] for out in outs]
                ps = [p + out[CHUNK:] for p, out in zip(ps, outs)]
            else:
                ps = [p + _dot1(p, rhs) for p, rhs in zip(ps, rhss)]
            n *= 2
        return ps


def _norm_matmul_kernel(x_ref, nw_ref, w_ref, o_ref, a_ref):
    @pl.when(pl.program_id(1) == 0)
    def _():
        x = x_ref[...]
        y = x * lax.rsqrt(jnp.mean(x * x, axis=-1, keepdims=True) + EPS) * nw_ref[...]
        a_ref[...] = y.astype(BF16)

    o_ref[...] = jnp.dot(a_ref[...], w_ref[...], preferred_element_type=F32)


def _norm_matmul(x, nw, w, l, tm=1024, tn=1792):
    t, d = x.shape
    n = w.shape[2]
    tm = min(tm, t)
    return pl.pallas_call(
        _norm_matmul_kernel,
        out_shape=jax.ShapeDtypeStruct((t, n), F32),
        grid=(t // tm, n // tn),
        in_specs=[pl.BlockSpec((tm, d), lambda i, j: (i, 0)),
                  pl.BlockSpec((None, 1, d), lambda i, j: (l, 0, 0)),
                  pl.BlockSpec((None, d, tn), lambda i, j: (l, 0, j))],
        out_specs=pl.BlockSpec((tm, tn), lambda i, j: (i, j)),
        scratch_shapes=[pltpu.VMEM((tm, d), BF16)],
        compiler_params=_cparams(2),
    )(x, nw, w)


def _out_proj_kernel(h_ref, oa_ref, ob_ref, oc_ref, od_ref, w_ref, o_ref):
    acc = h_ref[...]
    for idx, part in enumerate((oa_ref, ob_ref, oc_ref, od_ref)):
        acc = acc + jnp.dot(part[...].astype(BF16), w_ref[idx * MIX_W:(idx + 1) * MIX_W, :],
                            preferred_element_type=F32)
    o_ref[...] = acc


def _out_proj(h, parts, w, l, tm=512):
    t, d = h.shape
    tm = min(tm, t)
    part_spec = pl.BlockSpec((tm, MIX_W), lambda i: (i, 0))
    return pl.pallas_call(
        _out_proj_kernel,
        out_shape=jax.ShapeDtypeStruct((t, d), F32),
        grid=(t // tm,),
        in_specs=[pl.BlockSpec((tm, d), lambda i: (i, 0)),
                  part_spec, part_spec, part_spec, part_spec,
                  pl.BlockSpec((None, 4 * MIX_W, d), lambda i: (l, 0, 0))],
        out_specs=pl.BlockSpec((tm, d), lambda i: (i, 0)),
        compiler_params=_cparams(1),
    )(h, *parts, w)


def _ffn_kernel(h_ref, nw_ref, wu_ref, wd_ref, fw_ref, o_ref, a_ref, *, final_norm):
    f = pl.program_id(1)

    @pl.when(f == 0)
    def _():
        x = h_ref[...]
        y = x * lax.rsqrt(jnp.mean(x * x, axis=-1, keepdims=True) + EPS) * nw_ref[...]
        a_ref[...] = y.astype(BF16)
        o_ref[...] = x

    u = jnp.dot(a_ref[...], wu_ref[...], preferred_element_type=F32)
    s = jnp.square(jnp.maximum(u, 0.0)).astype(BF16)
    o_ref[...] += jnp.dot(s, wd_ref[...], preferred_element_type=F32)

    if final_norm:
        @pl.when(f == pl.num_programs(1) - 1)
        def _():
            y = o_ref[...]
            o_ref[...] = y * lax.rsqrt(jnp.mean(y * y, axis=-1, keepdims=True) + EPS) * fw_ref[...]


def _ffn(h, nw, wu, wd, fw, l, final_norm, tm=1024, tf=512):
    t, d = h.shape
    ff = wu.shape[2]
    tm = min(tm, t)
    return pl.pallas_call(
        functools.partial(_ffn_kernel, final_norm=final_norm),
        out_shape=jax.ShapeDtypeStruct((t, d), F32),
        grid=(t // tm, ff // tf),
        in_specs=[pl.BlockSpec((tm, d), lambda i, f: (i, 0)),
                  pl.BlockSpec((None, 1, d), lambda i, f: (l, 0, 0)),
                  pl.BlockSpec((None, d, tf), lambda i, f: (l, 0, f)),
                  pl.BlockSpec((None, tf, d), lambda i, f: (l, f, 0)),
                  pl.BlockSpec((1, d), lambda i, f: (0, 0))],
        out_specs=pl.BlockSpec((tm, d), lambda i, f: (i, 0)),
        scratch_shapes=[pltpu.VMEM((tm, d), BF16)],
        compiler_params=_cparams(2),
    )(h, nw, wu, wd, fw)


def _stage_with_halo(ext_ref, pieces, rows):
    @pl.when(pl.program_id(0) == 0)
    def _():
        ext_ref[0:8, :] = jnp.zeros((8, ext_ref.shape[1]), F32)

    col = 0
    for ref in pieces:
        w = ref.shape[1]
        ext_ref[8:8 + rows, col:col + w] = ref[...]
        col += w


def _keep_halo(ext_ref, rows):
    ext_ref[0:8, :] = ext_ref[rows:rows + 8, :]


def _causal_conv(ext_ref, cw, rows):
    acc = ext_ref[pl.ds(8, rows), :] * cw[3:4, :]
    for i in range(CONV_W - 1):
        acc = acc + ext_ref[pl.ds(5 + i, rows), :] * cw[i:i + 1, :]
    return acc


def _const_spec(shape):
    return pl.BlockSpec(shape, lambda i: (0,) * len(shape))


PAR_ROWS, PAR_W = 32, 1792
(R_GDN_CONV, R_GDN_ALOG, R_GDN_DTB, R_GDN_NW, R_RET_NW, R_SSD_CONV, R_SSD_CB, R_SSD_ALOG, R_SSD_DTB,
 R_SSD_D, R_SSD_NW, R_RW_MU, R_RW_W0, R_RW_A0, R_RW_KK, R_RW_KA, R_RW_RK, R_RW_LNW, R_RW_LNB) = (
    0, 4, 5, 6, 7, 8, 12, 13, 14, 15, 16, 17, 18, 19, 20, 21, 22, 23, 24)


def _layer_spec(shape, l):
    return pl.BlockSpec((None,) + tuple(shape), lambda i: (l,) + (0,) * len(shape))


def _par(par_ref, row, width, nrows=1):
    return par_ref[row:row + nrows, 0:width]


def _block_diag_ones(n, block, lower=False, dtype=F32):
    r = jnp.arange(n)[:, None]
    c = jnp.arange(n)[None, :]
    m = (r // block) == (c // block)
    if lower:
        m = m & (r >= c)
    return m.astype(dtype)


def _gdn_kernel(qkv_ref, z_ref, gate_ref, par_ref, btri_ref, eye_ref, o_ref, ext_ref, s_ref, *, nch):
    rows = nch * CHUNK

    @pl.when(pl.program_id(0) == 0)
    def _():
        s_ref[...] = jnp.zeros_like(s_ref)

    _stage_with_halo(ext_ref, (qkv_ref,), rows)
    qkv = _silu(_causal_conv(ext_ref, _par(par_ref, R_GDN_CONV, 1536, CONV_W), rows))
    _keep_halo(ext_ref, rows)

    causal, strict, _ = _chunk_masks()
    pair = _Pair()

    gt = gate_ref[...]
    beta = _sigmoid(gt)
    log_a = -jnp.exp(_par(par_ref, R_GDN_ALOG, 128)) * _softplus(gt + _par(par_ref, R_GDN_DTB, 128))
    g_all = _dot_sel(btri_ref[...], log_a)
    g_t = _dot_sel(g_all, eye_ref[...], TN)
    z = z_ref[...]
    nw = _par(par_ref, R_GDN_NW, GDN_DV)

    qs, ks, vs = [], [], []
    for h in range(GDN_HEADS):
        qh = qkv[:, h * GDN_DK:(h + 1) * GDN_DK]
        kh = qkv[:, 512 + h * GDN_DK:512 + (h + 1) * GDN_DK]
        qs.append(qh * lax.rsqrt(jnp.sum(qh * qh, axis=-1, keepdims=True) + EPS) * (GDN_DK ** -0.5))
        ks.append(kh * lax.rsqrt(jnp.sum(kh * kh, axis=-1, keepdims=True) + EPS))
        vs.append(qkv[:, 1024 + h * GDN_DV:1024 + (h + 1) * GDN_DV])

    combos = [(c, h) for c in range(nch) for h in range(GDN_HEADS)]
    rsl = lambda c: slice(c * CHUNK, (c + 1) * CHUNK)
    bcol = {(c, h): beta[rsl(c), h:h + 1] for c, h in combos}
    gcol = {(c, h): g_all[rsl(c), 4 + h:5 + h] for c, h in combos}
    kq = {(c, h): _dot1(jnp.concatenate([ks[h][rsl(c)], qs[h][rsl(c)]], axis=0), ks[h][rsl(c)], NT)
          for c, h in combos}
    lows, qks = {}, {}
    for c, h in combos:
        grow = g_t[4 + h:5 + h, c * CHUNK:(c + 1) * CHUNK]
        gamma = jnp.exp(jnp.where(causal, gcol[c, h] - grow, -jnp.inf))
        lows[c, h] = jnp.where(strict, kq[c, h][:CHUNK] * gamma * bcol[c, h], 0.0)
        qks[c, h] = kq[c, h][CHUNK:] * gamma
    pairs = [(c, p) for c in range(nch) for p in range(GDN_HEADS // 2)]
    tps = pair.inverse([jnp.concatenate([lows[c, 2 * p], lows[c, 2 * p + 1]], axis=1) for c, p in pairs])
    tinv = {}
    for (c, p), tp in zip(pairs, tps):
        tinv[c, 2 * p] = tp[:, :CHUNK]
        tinv[c, 2 * p + 1] = tp[:, CHUNK:]
    egc = {ch: jnp.exp(gcol[ch]) for ch in combos}
    glast = {ch: gcol[ch][CHUNK - 1:CHUNK, :] for ch in combos}
    wu = {(c, h): _dot1(tinv[c, h], jnp.concatenate(
        [ks[h][rsl(c)] * (bcol[c, h] * egc[c, h]), vs[h][rsl(c)] * bcol[c, h]], axis=1)) for c, h in combos}
    qwu = {ch: _dot1(qks[ch], wu[ch]) for ch in combos}
    kwu = {(c, h): _dot1(ks[h][rsl(c)] * jnp.exp(glast[c, h] - gcol[c, h]), wu[c, h], TN) for c, h in combos}
    for c in range(nch):
        rs = rsl(c)
        s0 = [s_ref[h] if c == 0 else s_new[h] for h in range(GDN_HEADS)]
        s_new = []
        for h in range(GDN_HEADS):
            sl = slice(h * GDN_DV, (h + 1) * GDN_DV)
            q_eff = qs[h][rs] * egc[c, h] - qwu[c, h][:, :GDN_DK]
            o = _dot1(q_eff, s0[h]) + qwu[c, h][:, GDN_DK:]
            s_new.append(s0[h] * jnp.exp(glast[c, h]) - _dot1(kwu[c, h][:, :GDN_DK], s0[h])
                         + kwu[c, h][:, GDN_DK:])
            o = o * lax.rsqrt(jnp.mean(o * o, axis=-1, keepdims=True) + EPS) * nw
            o_ref[rs, sl] = o * _silu(z[rs, sl])
    for h in range(GDN_HEADS):
        s_ref[h] = s_new[h]


def _gdn(proj, par, l, nch):
    t = proj.shape[0]
    rows = nch * CHUNK
    return pl.pallas_call(
        functools.partial(_gdn_kernel, nch=nch),
        out_shape=jax.ShapeDtypeStruct((t, MIX_W), F32),
        grid=(t // rows,),
        in_specs=[pl.BlockSpec((rows, 1536), lambda i: (i, 0)),
                  pl.BlockSpec((rows, 512), lambda i: (i, 8)),
                  pl.BlockSpec((rows, 128), lambda i: (i, 52)),
                  _layer_spec((PAR_ROWS, PAR_W), l),
                  _const_spec((rows, rows)), _const_spec((rows, rows))],
        out_specs=pl.BlockSpec((rows, MIX_W), lambda i: (i, 0)),
        scratch_shapes=[pltpu.VMEM((rows + 8, 1536), F32),
                        pltpu.VMEM((GDN_HEADS, GDN_DK, GDN_DV), F32)],
        compiler_params=_cparams(1),
    )(proj, proj, proj, par, _block_diag_ones(rows, CHUNK, lower=True, dtype=BF16),
      jnp.eye(rows, dtype=BF16))


def _ret_kernel(qk_ref, v_ref, g_ref, cos_ref, sin_ref, par_ref, o_ref, r_ref, *, nch):
    @pl.when(pl.program_id(0) == 0)
    def _():
        r_ref[...] = jnp.zeros_like(r_ref)

    rows = nch * CHUNK
    qk = qk_ref[...]
    cos = cos_ref[...]
    sin = sin_ref[...]
    even = (_iota2((rows, 256), 1) % 2) == 0

    def rot(x):
        partner = jnp.where(even, pltpu.roll(x, 255, 1), pltpu.roll(x, 1, 1))
        return x * cos + partner * sin

    q = rot(qk[:, 0:256])
    k = rot(qk[:, 256:512]) * (RET_DK ** -0.5)
    v = v_ref[...]
    gate = g_ref[...]
    nw = _par(par_ref, R_RET_NW, MIX_W)

    pair = _Pair()
    pos =_iota2((CHUNK, 2 * CHUNK), 0).astype(F32)
    zeros_v = jnp.zeros((CHUNK, RET_DV), F32)
    r2 = _iota2((2 * RET_DK, 2 * RET_DV), 0)
    c2 = _iota2((2 * RET_DK, 2 * RET_DV), 1)
    state_block = (r2 < RET_DK) == (c2 < RET_DV)

    for p in range(RET_HEADS // 2):
        lg0 = math.log1p(-(2.0 ** (-5.0 - 2 * p)))
        lg1 = math.log1p(-(2.0 ** (-5.0 - (2 * p + 1))))
        lg = jnp.where(pair.lo, lg0, lg1)
        dmask = jnp.exp(jnp.where(pair.causal, pair.dist * lg, -jnp.inf))
        k_dec = jnp.exp((CHUNK - 1.0 - pos) * lg)
        q_dec = jnp.exp((pos + 1.0) * lg)
        lg_v = jnp.where(_iota2((1, 2 * RET_DV), 1) < RET_DV, lg0, lg1)
        chunk_decay = jnp.exp(CHUNK * lg_v)
        ls = slice(p * 2 * RET_DK, (p + 1) * 2 * RET_DK)
        vsl = slice(p * 2 * RET_DV, (p + 1) * 2 * RET_DV)
        rsl = lambda c: slice(c * CHUNK, (c + 1) * CHUNK)
        sc = [_dot1(q[rsl(c), ls], pair.bd(k[rsl(c), ls]), NT) * dmask for c in range(nch)]
        upd = [_dot1(k[rsl(c), ls] * k_dec, v[rsl(c), vsl], TN) for c in range(nch)]
        r_start = []
        r_cur = r_ref[p]
        for c in range(nch):
            r_start.append(r_cur)
            r_cur = r_cur * chunk_decay + jnp.where(state_block, upd[c], 0.0)
        r_ref[p] = r_cur
        for c in range(nch):
            rs = rsl(c)
            vp = v[rs, vsl]
            v_bd = jnp.concatenate(
                [jnp.concatenate([vp[:, :RET_DV], zeros_v], axis=1),
                 jnp.concatenate([zeros_v, vp[:, RET_DV:]], axis=1)], axis=0)
            o = _dot1(sc[c], v_bd) + _dot1(q[rs, ls] * q_dec, r_start[c])
            for hh in range(2):
                sl = slice(vsl.start + hh * RET_DV, vsl.start + (hh + 1) * RET_DV)
                oh = o[:, hh * RET_DV:(hh + 1) * RET_DV]
                mu = jnp.mean(oh, axis=-1, keepdims=True)
                oc = oh - mu
                oh = oc * lax.rsqrt(jnp.mean(oc * oc, axis=-1, keepdims=True) + EPS) * nw[:, sl]
                o_ref[rs, sl] = oh * _silu(gate[rs, sl])


def _retention(proj, cos, sin, par, l, nch):
    t = proj.shape[0]
    rows = nch * CHUNK
    return pl.pallas_call(
        functools.partial(_ret_kernel, nch=nch),
        out_shape=jax.ShapeDtypeStruct((t, MIX_W), F32),
        grid=(t // rows,),
        in_specs=[pl.BlockSpec((rows, 512), lambda i: (i, 9)),
                  pl.BlockSpec((rows, 512), lambda i: (i, 10)),
                  pl.BlockSpec((rows, 512), lambda i: (i, 11)),
                  pl.BlockSpec((rows, 256), lambda i: (i, 0)),
                  pl.BlockSpec((rows, 256), lambda i: (i, 0)),
                  _layer_spec((PAR_ROWS, PAR_W), l)],
        out_specs=pl.BlockSpec((rows, MIX_W), lambda i: (i, 0)),
        scratch_shapes=[pltpu.VMEM((RET_HEADS // 2, 2 * RET_DK, 2 * RET_DV), F32)],
        compiler_params=_cparams(1),
    )(proj, proj, proj, cos, sin, par)


def _ssd_kernel(z_ref, xbc_ref, dt_ref, par_ref, btri_ref, eye_ref, expand_ref, o_ref, ext_ref, h_ref,
                *, nch):
    rows = nch * CHUNK

    @pl.when(pl.program_id(0) == 0)
    def _():
        h_ref[...] = jnp.zeros_like(h_ref)

    _stage_with_halo(ext_ref, (xbc_ref,), rows)
    xbc = _silu(_causal_conv(ext_ref, _par(par_ref, R_SSD_CONV, 1024, CONV_W), rows)
                + _par(par_ref, R_SSD_CB, 1024))
    _keep_halo(ext_ref, rows)
    x = xbc[:, 0:512]

    pair = _Pair()
    dt = _softplus(dt_ref[...] + _par(par_ref, R_SSD_DTB, 128))
    g_all = _dot_sel(btri_ref[...], dt * (-jnp.exp(_par(par_ref, R_SSD_ALOG, 128))))
    g_t = _dot_sel(g_all, eye_ref[...], TN)
    wide = _dot_sel(jnp.concatenate([g_all, dt], axis=0), expand_ref[...])
    g_w, dt_w = wide[:rows], wide[rows:]
    gtot_w = _rows_bcast(g_w, nch)
    xdt = x * dt_w
    xtail = xdt * jnp.exp(gtot_w - g_w)
    eg_w = jnp.exp(g_w)
    etot_w = jnp.exp(gtot_w)
    z = z_ref[...]
    d_w = _par(par_ref, R_SSD_D, MIX_W)
    nw = _par(par_ref, R_SSD_NW, MIX_W)
    gw = M2_HEADS // M2_GROUPS * M2_HEADDIM

    combos = [(c, gi) for c in range(nch) for gi in range(M2_GROUPS)]
    rsl = lambda c: slice(c * CHUNK, (c + 1) * CHUNK)
    bg = {(c, gi): xbc[rsl(c), 512 + gi * M2_STATE:512 + (gi + 1) * M2_STATE] for c, gi in combos}
    cg = {(c, gi): xbc[rsl(c), 768 + gi * M2_STATE:768 + (gi + 1) * M2_STATE] for c, gi in combos}
    cb2 = {k: _dot1(cg[k], jnp.concatenate([bg[k], bg[k]], axis=0), NT) for k in combos}
    upd = {(c, gi): _dot1(bg[c, gi], xtail[rsl(c), gi * gw:(gi + 1) * gw], TN) for c, gi in combos}
    y_in = {}
    for c, gi in combos:
        for pp in range(2):
            ha = gi * 4 + 2 * pp
            gcol = jnp.where(pair.lo, g_all[rsl(c), ha:ha + 1], g_all[rsl(c), ha + 1:ha + 2])
            grow = jnp.concatenate([g_t[ha:ha + 1, rsl(c)], g_t[ha + 1:ha + 2, rsl(c)]], axis=1)
            decay = jnp.exp(jnp.where(pair.causal, gcol - grow, -jnp.inf))
            y_in[c, gi, pp] = _dot1(cb2[c, gi] * decay,
                                    pair.bd(xdt[rsl(c), ha * M2_HEADDIM:(ha + 2) * M2_HEADDIM]))
    h_start = {}
    for gi in range(M2_GROUPS):
        h = h_ref[gi]
        for c in range(nch):
            h_start[c, gi] = h
            h = h * etot_w[c * CHUNK:c * CHUNK + 1, gi * gw:(gi + 1) * gw] + upd[c, gi]
        h_ref[gi] = h
    for c, gi in combos:
        rs, gs = rsl(c), slice(gi * gw, (gi + 1) * gw)
        y = jnp.concatenate([y_in[c, gi, 0], y_in[c, gi, 1]], axis=1)
        y = y + _dot1(cg[c, gi], h_start[c, gi]) * eg_w[rs, gs]
        y = (y + d_w[:, gs] * x[rs, gs]) * _silu(z[rs, gs])
        y = y * lax.rsqrt(jnp.mean(y * y, axis=-1, keepdims=True) + EPS) * nw[:, gs]
        o_ref[rs, gs] = y


def _head_expand_matrix(n_heads, width):
    r = jnp.arange(128)[:, None]
    c = jnp.arange(n_heads * width)[None, :]
    return (r == c // width).astype(BF16)


def _ssd(proj, par, l, nch):
    t = proj.shape[0]
    rows = nch * CHUNK
    return pl.pallas_call(
        functools.partial(_ssd_kernel, nch=nch),
        out_shape=jax.ShapeDtypeStruct((t, MIX_W), F32),
        grid=(t // rows,),
        in_specs=[pl.BlockSpec((rows, 512), lambda i: (i, 12)),
                  pl.BlockSpec((rows, 1024), lambda i: (i, 3)),
                  pl.BlockSpec((rows, 128), lambda i: (i, 53)),
                  _layer_spec((PAR_ROWS, PAR_W), l),
                  _const_spec((rows, rows)), _const_spec((rows, rows)), _const_spec((128, 512))],
        out_specs=pl.BlockSpec((rows, MIX_W), lambda i: (i, 0)),
        scratch_shapes=[pltpu.VMEM((rows + 8, 1024), F32),
                        pltpu.VMEM((M2_GROUPS, M2_STATE, 256), F32)],
        compiler_params=_cparams(1),
    )(proj, proj, proj, par,
      _block_diag_ones(rows, CHUNK, lower=True, dtype=BF16), jnp.eye(rows, dtype=BF16),
      _head_expand_matrix(M2_HEADS, M2_HEADDIM))


def _rwkv_kernel(rkv_ref, lora_ref, par_ref, lmat_ref, btri_ref, hblk_ref, o_ref, ext_ref, s_ref, *, nch):
    rows = nch * CHUNK

    @pl.when(pl.program_id(0) == 0)
    def _():
        s_ref[...] = jnp.zeros_like(s_ref)

    _stage_with_halo(ext_ref, (rkv_ref, lora_ref), rows)
    cur = ext_ref[pl.ds(8, rows), :]
    prev = ext_ref[pl.ds(7, rows), :]
    _keep_halo(ext_ref, rows)
    mixed = cur + (prev - cur) * _par(par_ref, R_RW_MU, PAR_W)
    r = mixed[:, 0:512]
    k = mixed[:, 512:1024]
    v = mixed[:, 1024:1536]
    lora = mixed[:, 1536:1792]

    w_raw = -_softplus(-(_par(par_ref, R_RW_W0, MIX_W) + _dot1(jnp.tanh(lora), lmat_ref[0:256, :]))) - 0.5
    log_d = -jnp.exp(w_raw)
    a = _sigmoid(_par(par_ref, R_RW_A0, MIX_W) + _dot1(lora, lmat_ref[256:512, :]))
    gate = _dot1(_sigmoid(lora), lmat_ref[512:768, :])
    hblk = hblk_ref[...]

    g_in = _dot_sel(btri_ref[...], log_d)
    g_tot = _rows_bcast(g_in, nch)
    e_in = jnp.exp(g_in)
    e_neg = jnp.exp(-g_in)
    e_ex = jnp.exp(g_in - log_d)
    e_tail = jnp.exp(g_tot - g_in)
    e_end = jnp.exp(g_tot)

    kk = k * _par(par_ref, R_RW_KK, MIX_W)
    kk = kk * lax.rsqrt(_sum_bcast(kk * kk, hblk) + EPS)
    k_mod = k * (1.0 + (a - 1.0) * _par(par_ref, R_RW_KA, MIX_W))
    a_vec = -(a * kk)
    r_t = r * e_in
    b_t = kk * e_ex
    k_t = k_mod * e_neg
    a_t = a_vec * e_neg
    k_c = k_mod * e_tail
    a_c = a_vec * e_tail

    pair = _Pair()
    npair = RW_HEADS // 2
    combos = [(c, j) for c in range(nch) for j in range(npair)]
    cut = lambda arr, cj: arr[cj[0] * CHUNK:(cj[0] + 1) * CHUNK, cj[1] * 2 * RW_N:(cj[1] + 1) * 2 * RW_N]
    lhs = {cj: jnp.concatenate([cut(b_t, cj), cut(r_t, cj)], axis=0) for cj in combos}
    x1 = {cj: _dot1(lhs[cj], pair.bd(cut(k_t, cj)), NT) for cj in combos}
    x2 = {cj: _dot1(lhs[cj], pair.bd(cut(a_t, cj)), NT) for cj in combos}
    a_ra = {cj: jnp.where(pair.causal, x2[cj][CHUNK:], 0.0) for cj in combos}
    tinv = dict(zip(combos, pair.inverse([-jnp.where(pair.strict, x2[cj][:CHUNK], 0.0) for cj in combos])))
    av = {cj: _dot1(jnp.concatenate([jnp.where(pair.strict, x1[cj][:CHUNK], 0.0),
                                     jnp.where(pair.causal, x1[cj][CHUNK:], 0.0)], axis=0),
                    pair.bd(cut(v, cj))) for cj in combos}
    tz = {cj: _dot1(tinv[cj], jnp.concatenate([pair.bd(cut(b_t, cj)), pair.bd(av[cj][:CHUNK])], axis=1))
          for cj in combos}
    az = {cj: _dot1(a_ra[cj], jnp.concatenate([pair.bd(tz[cj][:, :2 * RW_N]), pair.bd(tz[cj][:, 2 * RW_N:])],
                                              axis=1)) for cj in combos}
    r_eff = {cj: cut(r_t, cj) + az[cj][:, :2 * RW_N] for cj in combos}
    y0 = {cj: av[cj][CHUNK:] + az[cj][:, 2 * RW_N:] for cj in combos}
    p_low = {cj: jnp.where(pair.same_block, _dot1(tz[cj][:, :2 * RW_N], cut(a_c, cj), TN), 0.0) for cj in combos}
    q_mat = {cj: jnp.where(pair.same_block,
                           _dot1(jnp.concatenate([cut(v, cj), tz[cj][:, 2 * RW_N:]], axis=0),
                                 jnp.concatenate([cut(k_c, cj), cut(a_c, cj)], axis=0), TN), 0.0)
             for cj in combos}
    y_rows = []
    for c in range(nch):
        s0 = [s_ref[j] if c == 0 else s_new[j] for j in range(npair)]
        y_rows.append(jnp.concatenate(
            [_dot1(r_eff[c, j], s0[j], NT) + y0[c, j] for j in range(npair)], axis=1))
        s_new = [s0[j] * e_end[c * CHUNK:c * CHUNK + 1, j * 2 * RW_N:(j + 1) * 2 * RW_N]
                 + _dot1(s0[j], p_low[c, j]) + q_mat[c, j] for j in range(npair)]
    for j in range(npair):
        s_ref[j] = s_new[j]
    y = jnp.concatenate(y_rows, axis=0)

    inv_n = 1.0 / RW_N
    mu = _sum_bcast(y, hblk, pieces=2) * inv_n
    yc = y - mu
    var = _sum_bcast(yc * yc, hblk) * inv_n
    y = yc * lax.rsqrt(var + RWKV_LN_EPS) * _par(par_ref, R_RW_LNW, MIX_W) + _par(par_ref, R_RW_LNB, MIX_W)
    y = y + _sum_bcast(r * k_mod * _par(par_ref, R_RW_RK, MIX_W), hblk) * v
    o_ref[...] = y * gate


def _rwkv(proj, par, lmat, l, nch):
    t = proj.shape[0]
    rows = nch * CHUNK
    return pl.pallas_call(
        functools.partial(_rwkv_kernel, nch=nch),
        out_shape=jax.ShapeDtypeStruct((t, MIX_W), F32),
        grid=(t // rows,),
        in_specs=[pl.BlockSpec((rows, 1536), lambda i: (i, 1)),
                  pl.BlockSpec((rows, 256), lambda i: (i, 27)),
                  _layer_spec((PAR_ROWS, PAR_W), l), _layer_spec((768, MIX_W), l),
                  _const_spec((rows, rows)), _const_spec((512, 512))],
        out_specs=pl.BlockSpec((rows, MIX_W), lambda i: (i, 0)),
        scratch_shapes=[pltpu.VMEM((rows + 8, 1792), F32),
                        pltpu.VMEM((RW_HEADS // 2, 2 * RW_N, 2 * RW_N), F32)],
        compiler_params=_cparams(1),
    )(proj, proj, par, lmat,
      _block_diag_ones(rows, CHUNK, lower=True, dtype=BF16), _block_diag_ones(MIX_W, RW_N, dtype=BF16))


def _layout_w_in(w):
    cols = [jnp.pad(w[:, :, s:s + n], ((0, 0), (0, 0), (0, p - n))) for s, n, p in _SRC_PIECES]
    return jnp.concatenate(cols, axis=2).astype(BF16)


def _param_slab(p):
    depth = p['w_in'].shape[0]

    def rows(a, lane=0):
        a = a.astype(F32).reshape(depth, -1, a.shape[-1])
        return jnp.pad(a, ((0, 0), (0, 0), (lane, PAR_W - lane - a.shape[-1])))

    pieces = [rows(p['gdn_conv_w']), rows(p['gdn_a_log'], 4), rows(p['gdn_dt_bias'], 4),
              rows(p['gdn_norm_w']), rows(p['ret_norm_w']),
              rows(p['m2_conv_w']), rows(p['m2_conv_b']), rows(p['m2_a_log']), rows(p['m2_dt_bias']),
              rows(jnp.repeat(p['m2_d'], M2_HEADDIM, axis=1)), rows(p['m2_norm_w']),
              rows(p['rw_mu']), rows(p['rw_w0']), rows(p['rw_a0']), rows(p['rw_k_k']), rows(p['rw_k_a']),
              rows(p['rw_r_k'].reshape(depth, -1)), rows(p['rw_ln_w']), rows(p['rw_ln_b'])]
    slab = jnp.concatenate(pieces, axis=1)
    return jnp.pad(slab, ((0, 0), (0, PAR_ROWS - slab.shape[1]), (0, 0)))


def _lora_mats(p):
    def at(m, offset):
        return jnp.pad(m, ((0, 0), (offset, 256 - offset - m.shape[1]), (0, 0)))
    return jnp.concatenate([at(p['rw_w_up'], 0), at(p['rw_a_up'], RW_W_LORA),
                            at(p['rw_g_up'], RW_W_LORA + RW_A_LORA)], axis=1).astype(BF16)


def _rotary_tables(t):
    theta = 1.0 / (ROPE_BASE ** jnp.linspace(0.0, 1.0, RET_DK // 2, dtype=F32))
    ang = jnp.arange(t, dtype=F32)[:, None] * theta
    cos = jnp.repeat(jnp.cos(ang), 2, axis=1)
    sin = jnp.stack([-jnp.sin(ang), jnp.sin(ang)], axis=-1).reshape(t, RET_DK)
    return jnp.tile(cos, (1, RET_HEADS)), jnp.tile(sin, (1, RET_HEADS))


NCH_GDN, NCH_RET, NCH_SSD, NCH_RWKV = 4, 4, 4, 4


def _prepare(p, t):
    cos, sin = _rotary_tables(t)
    return dict(norm1=p['norm1_w'].astype(F32)[:, None, :], w_in=_layout_w_in(p['w_in']),
                par=_param_slab(p), lmat=_lora_mats(p), w_out=p['w_out'].astype(BF16), cos=cos, sin=sin)


def _token_mix(h, l, q):
    t = h.shape[0]
    nch = lambda n: min(n, t // CHUNK)
    proj = _norm_matmul(h, q['norm1'], q['w_in'], l)
    o_a = _gdn(proj, q['par'], l, nch(NCH_GDN))
    o_b = _retention(proj, q['cos'], q['sin'], q['par'], l, nch(NCH_RET))
    o_c = _ssd(proj, q['par'], l, nch(NCH_SSD))
    o_d = _rwkv(proj, q['par'], q['lmat'], l, nch(NCH_RWKV))
    return _out_proj(h, (o_a, o_b, o_c, o_d), q['w_out'], l)


def kernel(x, norm1_w, w_in, gdn_conv_w, gdn_a_log, gdn_dt_bias, gdn_norm_w, ret_norm_w, m2_conv_w, m2_conv_b, m2_a_log, m2_dt_bias, m2_d, m2_norm_w, rw_mu, rw_w0, rw_w_up, rw_a0, rw_a_up, rw_g_up, rw_k_k, rw_k_a, rw_r_k, rw_ln_w, rw_ln_b, w_out, norm2_w, w_ffn_up, w_ffn_down, final_norm_w):
    p = dict(norm1_w=norm1_w, w_in=w_in, gdn_conv_w=gdn_conv_w, gdn_a_log=gdn_a_log,
             gdn_dt_bias=gdn_dt_bias, gdn_norm_w=gdn_norm_w, ret_norm_w=ret_norm_w,
             m2_conv_w=m2_conv_w, m2_conv_b=m2_conv_b, m2_a_log=m2_a_log, m2_dt_bias=m2_dt_bias,
             m2_d=m2_d, m2_norm_w=m2_norm_w, rw_mu=rw_mu, rw_w0=rw_w0, rw_w_up=rw_w_up,
             rw_a0=rw_a0, rw_a_up=rw_a_up, rw_g_up=rw_g_up, rw_k_k=rw_k_k, rw_k_a=rw_k_a,
             rw_r_k=rw_r_k, rw_ln_w=rw_ln_w, rw_ln_b=rw_ln_b, w_out=w_out)
    bsz, t, d = x.shape
    depth = w_in.shape[0]
    q = _prepare(p, t)
    norm2 = norm2_w.astype(F32)[:, None, :]
    w_up = w_ffn_up.astype(BF16)
    w_down = w_ffn_down.astype(BF16)
    final_w = final_norm_w.astype(F32).reshape(1, d)
    outs = []
    for b in range(bsz):
        h = x[b]
        for l in range(depth):
            h = _token_mix(h, l, q)
            h = _ffn(h, norm2, w_up, w_down, final_w, l, final_norm=(l == depth - 1))
        outs.append(h)
    return outs[0].reshape(1, t, d) if bsz == 1 else jnp.stack(outs, axis=0)
```

```python
import functools
import math

import jax
import jax.numpy as jnp
from jax import lax
from jax.experimental import pallas as pl
from jax.experimental.pallas import tpu as pltpu

F32 = jnp.float32
BF16 = jnp.bfloat16

D_MODEL = 2048
D_FF = 4 * D_MODEL
CONV_W = 4
CHUNK = 64
EPS = 1e-6
ROPE_BASE = 10000.0
RWKV_LN_EPS = 64e-5

GDN_HEADS, GDN_DK, GDN_DV = 4, 128, 128
RET_HEADS, RET_DK, RET_DV = 4, 64, 128
M2_HEADS, M2_HEADDIM, M2_GROUPS, M2_STATE = 8, 64, 2, 128
RW_HEADS, RW_N = 8, 64
RW_W_LORA, RW_A_LORA, RW_G_LORA = 32, 32, 96
MIX_W = 512

_GDN0, _RET0, _M20, _RW0 = 0, 2056, 3592, 5136

_W_PIECES = {
    'gdn': ((_GDN0, 1536, 1536), (_GDN0 + 1536, 512, 512), (_GDN0 + 2048, 8, 128)),
    'ret': ((_RET0, 1536, 1536),),
    'ssd': ((_M20 + 512, 1024, 1024), (_M20, 512, 512), (_M20 + 1536, 8, 128)),
    'rwkv': ((_RW0, 1536, 1536), (_RW0 + 1536, 160, 256)),
}
PROJ_CHUNK = 256

VMEM_LIMIT = 56 * 1024 * 1024

NN = (((1,), (0,)), ((), ()))
NT = (((1,), (1,)), ((), ()))
TN = (((0,), (0,)), ((), ()))


def _cparams(n_axes):
    return pltpu.CompilerParams(dimension_semantics=("arbitrary",) * n_axes,
                                vmem_limit_bytes=VMEM_LIMIT)


def _dg(a, b, dims=NN, prec=None):
    return lax.dot_general(a, b, dims, preferred_element_type=F32, precision=prec)


def _split(a):
    hi = a.astype(BF16)
    lo = (a - hi.astype(F32)).astype(BF16)
    return hi, lo


def _dot1(a, b, dims=NN):
    return _dg(a.astype(BF16), b.astype(BF16), dims)


def _split3(a):
    hi = a.astype(BF16)
    r1 = a - hi.astype(F32)
    mid = r1.astype(BF16)
    lo = (r1 - mid.astype(F32)).astype(BF16)
    return hi, mid, lo


def _dot_sel(a, b, dims=NN):
    if a.dtype == BF16:
        return sum(_dg(a, piece, dims) for piece in _split3(b))
    return sum(_dg(piece, b, dims) for piece in _split3(a))


def _sum_bcast(x, blk, pieces=1):
    outs = []
    for s in range(0, x.shape[1], 256):
        xs, bs = x[:, s:s + 256], blk[s:s + 256, s:s + 256]
        if pieces == 1:
            outs.append(_dg(xs.astype(BF16), bs))
        else:
            xh, xl = _split(xs)
            outs.append(_dg(xh, bs) + _dg(xl, bs))
    return jnp.concatenate(outs, axis=1)


def _rows_bcast(x, nch):
    return jnp.concatenate(
        [jnp.broadcast_to(x[(c + 1) * CHUNK - 1:(c + 1) * CHUNK, :], (CHUNK, x.shape[1])) for c in range(nch)],
        axis=0)


def _sigmoid(x):
    return 1.0 / (1.0 + jnp.exp(-x))


def _silu(x):
    return x * _sigmoid(x)


def _softplus(x):
    return jnp.maximum(x, 0.0) + jnp.log1p(jnp.exp(-jnp.abs(x)))


def _iota2(shape, axis):
    return lax.broadcasted_iota(jnp.int32, shape, axis)


def _chunk_masks():
    r = _iota2((CHUNK, CHUNK), 0)
    c = _iota2((CHUNK, CHUNK), 1)
    return r >= c, r > c, r == c


class _Pair:
    def __init__(self):
        lane = _iota2((CHUNK, 2 * CHUNK), 1)
        row = _iota2((CHUNK, 2 * CHUNK), 0)
        self.lo = lane < CHUNK
        col = jnp.where(self.lo, lane, lane - CHUNK)
        self.dist = (row - col).astype(F32)
        self.causal = row >= col
        self.strict = row > col
        self.eye = jnp.where(row == col, 1.0, 0.0).astype(F32)
        r2 = _iota2((2 * CHUNK, 2 * CHUNK), 0)
        c2 = _iota2((2 * CHUNK, 2 * CHUNK), 1)
        self.same_block = (r2 < CHUNK) == (c2 < CHUNK)

    def bd(self, x):
        z = jnp.zeros_like(x)
        return jnp.concatenate([jnp.where(self.lo, x, z), jnp.where(self.lo, z, x)], axis=0)

    def inverse(self, lows):
        ps = [self.eye - low for low in lows]
        curs = list(lows)
        n = 1
        while n < CHUNK:
            rhss = [self.bd(cur) for cur in curs]
            if n == 1:
                curs = [_dot1(cur, rhs) for cur, rhs in zip(curs, rhss)]
            elif 2 * n < CHUNK:
                outs = [_dot1(jnp.concatenate([cur, p], axis=0), rhs)
                        for cur, p, rhs in zip(curs, ps, rhss)]
                curs = [out[:CHUNK] for out in outs]
                ps = [p + out[CHUNK:] for p, out in zip(ps, outs)]
            else:
                ps = [p + _dot1(p, rhs) for p, rhs in zip(ps, rhss)]
            n *= 2
        return ps


def _norm_cast_kernel(x_ref, nw_ref, o_ref):
    x = x_ref[...]
    y = x * lax.rsqrt(jnp.mean(x * x, axis=-1, keepdims=True) + EPS) * nw_ref[...]
    o_ref[...] = y.astype(BF16)


def _norm_cast(x, nw, l, tm=1024):
    t, d = x.shape
    tm = min(tm, t)
    return pl.pallas_call(
        _norm_cast_kernel,
        out_shape=jax.ShapeDtypeStruct((t, d), BF16),
        grid=(t // tm,),
        in_specs=[pl.BlockSpec((tm, d), lambda i: (i, 0)),
                  pl.BlockSpec((None, 1, d), lambda i: (l, 0, 0))],
        out_specs=pl.BlockSpec((tm, d), lambda i: (i, 0)),
        compiler_params=_cparams(1),
    )(x, nw)


def _out_proj_kernel(h_ref, oa_ref, ob_ref, oc_ref, od_ref, w_ref, o_ref):
    acc = h_ref[...]
    for idx, part in enumerate((oa_ref, ob_ref, oc_ref, od_ref)):
        acc = acc + jnp.dot(part[...].astype(BF16), w_ref[idx * MIX_W:(idx + 1) * MIX_W, :],
                            preferred_element_type=F32)
    o_ref[...] = acc


def _out_proj(h, parts, w, l, tm=512):
    t, d = h.shape
    tm = min(tm, t)
    part_spec = pl.BlockSpec((tm, MIX_W), lambda i: (i, 0))
    return pl.pallas_call(
        _out_proj_kernel,
        out_shape=jax.ShapeDtypeStruct((t, d), F32),
        grid=(t // tm,),
        in_specs=[pl.BlockSpec((tm, d), lambda i: (i, 0)),
                  part_spec, part_spec, part_spec, part_spec,
                  pl.BlockSpec((None, 4 * MIX_W, d), lambda i: (l, 0, 0))],
        out_specs=pl.BlockSpec((tm, d), lambda i: (i, 0)),
        compiler_params=_cparams(1),
    )(h, *parts, w)


def _ffn_kernel(h_ref, nw_ref, wu_ref, wd_ref, fw_ref, o_ref, a_ref, *, final_norm):
    f = pl.program_id(1)

    @pl.when(f == 0)
    def _():
        x = h_ref[...]
        y = x * lax.rsqrt(jnp.mean(x * x, axis=-1, keepdims=True) + EPS) * nw_ref[...]
        a_ref[...] = y.astype(BF16)
        o_ref[...] = x

    u = jnp.dot(a_ref[...], wu_ref[...], preferred_element_type=F32)
    s = jnp.square(jnp.maximum(u, 0.0)).astype(BF16)
    o_ref[...] += jnp.dot(s, wd_ref[...], preferred_element_type=F32)

    if final_norm:
        @pl.when(f == pl.num_programs(1) - 1)
        def _():
            y = o_ref[...]
            o_ref[...] = y * lax.rsqrt(jnp.mean(y * y, axis=-1, keepdims=True) + EPS) * fw_ref[...]


def _ffn(h, nw, wu, wd, fw, l, final_norm, tm=1024, tf=512):
    t, d = h.shape
    ff = wu.shape[2]
    tm = min(tm, t)
    return pl.pallas_call(
        functools.partial(_ffn_kernel, final_norm=final_norm),
        out_shape=jax.ShapeDtypeStruct((t, d), F32),
        grid=(t // tm, ff // tf),
        in_specs=[pl.BlockSpec((tm, d), lambda i, f: (i, 0)),
                  pl.BlockSpec((None, 1, d), lambda i, f: (l, 0, 0)),
                  pl.BlockSpec((None, d, tf), lambda i, f: (l, 0, f)),
                  pl.BlockSpec((None, tf, d), lambda i, f: (l, f, 0)),
                  pl.BlockSpec((1, d), lambda i, f: (0, 0))],
        out_specs=pl.BlockSpec((tm, d), lambda i, f: (i, 0)),
        scratch_shapes=[pltpu.VMEM((tm, d), BF16)],
        compiler_params=_cparams(2),
    )(h, nw, wu, wd, fw)


def _stage_with_halo(ext_ref, pieces, rows):
    @pl.when(pl.program_id(0) == 0)
    def _():
        ext_ref[0:8, :] = jnp.zeros((8, ext_ref.shape[1]), F32)

    col = 0
    for ref in pieces:
        w = ref.shape[1]
        ext_ref[8:8 + rows, col:col + w] = ref[...]
        col += w


def _keep_halo(ext_ref, rows):
    ext_ref[0:8, :] = ext_ref[rows:rows + 8, :]


def _causal_conv(ext_ref, cw, rows):
    acc = ext_ref[pl.ds(8, rows), :] * cw[3:4, :]
    for i in range(CONV_W - 1):
        acc = acc + ext_ref[pl.ds(5 + i, rows), :] * cw[i:i + 1, :]
    return acc


def _const_spec(shape):
    return pl.BlockSpec(shape, lambda i: (0,) * len(shape))


PAR_ROWS, PAR_W = 32, 1792
(R_GDN_CONV, R_GDN_ALOG, R_GDN_DTB, R_GDN_NW, R_RET_NW, R_SSD_CONV, R_SSD_CB, R_SSD_ALOG, R_SSD_DTB,
 R_SSD_D, R_SSD_NW, R_RW_MU, R_RW_W0, R_RW_A0, R_RW_KK, R_RW_KA, R_RW_RK, R_RW_LNW, R_RW_LNB) = (
    0, 4, 5, 6, 7, 8, 12, 13, 14, 15, 16, 17, 18, 19, 20, 21, 22, 23, 24)


def _layer_spec(shape, l):
    return pl.BlockSpec((None,) + tuple(shape), lambda i: (l,) + (0,) * len(shape))


def _par(par_ref, row, width, nrows=1):
    return par_ref[row:row + nrows, 0:width]


class _NextProjection:
    def __init__(self, a_next_ref, w_ref, pj_ref):
        self.a_ref, self.w_ref, self.pj_ref = a_next_ref, w_ref, pj_ref
        width = pj_ref.shape[1]
        self.todo = [(s, min(s + PROJ_CHUNK, width)) for s in range(0, width, PROJ_CHUNK)]

    def emit(self, n=1):
        for _ in range(min(n, len(self.todo))):
            s, e = self.todo.pop(0)
            self.pj_ref[:, s:e] = jnp.dot(self.a_ref[...], self.w_ref[:, s:e], preferred_element_type=F32)


def _mixer_kernel(*refs, body, views, n_extra, nch):
    a_cur_ref, a_next_ref, w_ref = refs[:3]
    extra = refs[3:3 + n_extra]
    o_ref, pj_ref = refs[3 + n_extra], refs[4 + n_extra]
    scratch = refs[5 + n_extra:]

    @pl.when(pl.program_id(0) == 0)
    def _():
        pj_ref[...] = jnp.dot(a_cur_ref[...], w_ref[...], preferred_element_type=F32)

    proj = _NextProjection(a_next_ref, w_ref, pj_ref)
    body(*[pj_ref.at[:, s:e] for s, e in views], *extra, o_ref, *scratch, nch=nch, emit=proj.emit)
    proj.emit(len(proj.todo))


def _mixer_call(body, a, w, l, views, extra, extra_specs, scratch_shapes, nch):
    t, d = a.shape
    rows = nch * CHUNK
    nsteps = t // rows
    width = w.shape[2]
    return pl.pallas_call(
        functools.partial(_mixer_kernel, body=body, views=views, n_extra=len(extra), nch=nch),
        out_shape=jax.ShapeDtypeStruct((t, MIX_W), F32),
        grid=(nsteps,),
        in_specs=[pl.BlockSpec((rows, d), lambda i: (i, 0)),
                  pl.BlockSpec((rows, d), lambda i: (jnp.minimum(i + 1, nsteps - 1), 0)),
                  _layer_spec((d, width), l)] + list(extra_specs),
        out_specs=pl.BlockSpec((rows, MIX_W), lambda i: (i, 0)),
        scratch_shapes=[pltpu.VMEM((rows, width), F32)] + list(scratch_shapes),
        compiler_params=_cparams(1),
    )(a, a, w, *extra)


def _block_diag_ones(n, block, lower=False, dtype=F32):
    r = jnp.arange(n)[:, None]
    c = jnp.arange(n)[None, :]
    m = (r // block) == (c // block)
    if lower:
        m = m & (r >= c)
    return m.astype(dtype)


def _gdn_body(qkv_ref, z_ref, gate_ref, par_ref, btri_ref, eye_ref, o_ref, ext_ref, s_ref, *, nch, emit):
    rows = nch * CHUNK

    @pl.when(pl.program_id(0) == 0)
    def _():
        s_ref[...] = jnp.zeros_like(s_ref)

    _stage_with_halo(ext_ref, (qkv_ref,), rows)
    gt = gate_ref[...]
    z = z_ref[...]
    emit(2)
    qkv = _silu(_causal_conv(ext_ref, _par(par_ref, R_GDN_CONV, 1536, CONV_W), rows))
    _keep_halo(ext_ref, rows)
    emit(2)

    causal, strict, _ = _chunk_masks()
    pair = _Pair()

    beta = _sigmoid(gt)
    log_a = -jnp.exp(_par(par_ref, R_GDN_ALOG, 128)) * _softplus(gt + _par(par_ref, R_GDN_DTB, 128))
    g_all = _dot_sel(btri_ref[...], log_a)
    g_t = _dot_sel(g_all, eye_ref[...], TN)
    nw = _par(par_ref, R_GDN_NW, GDN_DV)

    qs, ks, vs = [], [], []
    for h in range(GDN_HEADS):
        qh = qkv[:, h * GDN_DK:(h + 1) * GDN_DK]
        kh = qkv[:, 512 + h * GDN_DK:512 + (h + 1) * GDN_DK]
        qs.append(qh * lax.rsqrt(jnp.sum(qh * qh, axis=-1, keepdims=True) + EPS) * (GDN_DK ** -0.5))
        ks.append(kh * lax.rsqrt(jnp.sum(kh * kh, axis=-1, keepdims=True) + EPS))
        vs.append(qkv[:, 1024 + h * GDN_DV:1024 + (h + 1) * GDN_DV])
        emit(1)

    combos = [(c, h) for c in range(nch) for h in range(GDN_HEADS)]
    rsl = lambda c: slice(c * CHUNK, (c + 1) * CHUNK)
    bcol = {(c, h): beta[rsl(c), h:h + 1] for c, h in combos}
    gcol = {(c, h): g_all[rsl(c), 4 + h:5 + h] for c, h in combos}
    kq = {(c, h): _dot1(jnp.concatenate([ks[h][rsl(c)], qs[h][rsl(c)]], axis=0), ks[h][rsl(c)], NT)
          for c, h in combos}
    lows, qks = {}, {}
    for c, h in combos:
        grow = g_t[4 + h:5 + h, c * CHUNK:(c + 1) * CHUNK]
        gamma = jnp.exp(jnp.where(causal, gcol[c, h] - grow, -jnp.inf))
        lows[c, h] = jnp.where(strict, kq[c, h][:CHUNK] * gamma * bcol[c, h], 0.0)
        qks[c, h] = kq[c, h][CHUNK:] * gamma
    emit(1)
    pairs = [(c, p) for c in range(nch) for p in range(GDN_HEADS // 2)]
    tps = pair.inverse([jnp.concatenate([lows[c, 2 * p], lows[c, 2 * p + 1]], axis=1) for c, p in pairs])
    tinv = {}
    for (c, p), tp in zip(pairs, tps):
        tinv[c, 2 * p] = tp[:, :CHUNK]
        tinv[c, 2 * p + 1] = tp[:, CHUNK:]
    egc = {ch: jnp.exp(gcol[ch]) for ch in combos}
    glast = {ch: gcol[ch][CHUNK - 1:CHUNK, :] for ch in combos}
    wu = {(c, h): _dot1(tinv[c, h], jnp.concatenate(
        [ks[h][rsl(c)] * (bcol[c, h] * egc[c, h]), vs[h][rsl(c)] * bcol[c, h]], axis=1)) for c, h in combos}
    qwu = {ch: _dot1(qks[ch], wu[ch]) for ch in combos}
    kwu = {(c, h): _dot1(ks[h][rsl(c)] * jnp.exp(glast[c, h] - gcol[c, h]), wu[c, h], TN) for c, h in combos}
    for c in range(nch):
        rs = rsl(c)
        s0 = [s_ref[h] if c == 0 else s_new[h] for h in range(GDN_HEADS)]
        s_new = []
        for h in range(GDN_HEADS):
            sl = slice(h * GDN_DV, (h + 1) * GDN_DV)
            q_eff = qs[h][rs] * egc[c, h] - qwu[c, h][:, :GDN_DK]
            o = _dot1(q_eff, s0[h]) + qwu[c, h][:, GDN_DK:]
            s_new.append(s0[h] * jnp.exp(glast[c, h]) - _dot1(kwu[c, h][:, :GDN_DK], s0[h])
                         + kwu[c, h][:, GDN_DK:])
            o = o * lax.rsqrt(jnp.mean(o * o, axis=-1, keepdims=True) + EPS) * nw
            o_ref[rs, sl] = o * _silu(z[rs, sl])
    for h in range(GDN_HEADS):
        s_ref[h] = s_new[h]


def _gdn(a, w, par, l, nch):
    rows = nch * CHUNK
    return _mixer_call(
        _gdn_body, a, w, l, views=((0, 1536), (1536, 2048), (2048, 2176)),
        extra=(par, _block_diag_ones(rows, CHUNK, lower=True, dtype=BF16), jnp.eye(rows, dtype=BF16)),
        extra_specs=(_layer_spec((PAR_ROWS, PAR_W), l), _const_spec((rows, rows)), _const_spec((rows, rows))),
        scratch_shapes=(pltpu.VMEM((rows + 8, 1536), F32), pltpu.VMEM((GDN_HEADS, GDN_DK, GDN_DV), F32)),
        nch=nch)


def _ret_body(qk_ref, v_ref, g_ref, cos_ref, sin_ref, par_ref, o_ref, r_ref, *, nch, emit):
    @pl.when(pl.program_id(0) == 0)
    def _():
        r_ref[...] = jnp.zeros_like(r_ref)

    rows = nch * CHUNK
    qk = qk_ref[...]
    v = v_ref[...]
    gate = g_ref[...]
    emit(1)
    cos = cos_ref[...]
    sin = sin_ref[...]
    even = (_iota2((rows, 256), 1) % 2) == 0

    def rot(x):
        partner = jnp.where(even, pltpu.roll(x, 255, 1), pltpu.roll(x, 1, 1))
        return x * cos + partner * sin

    q = rot(qk[:, 0:256])
    emit(1)
    k = rot(qk[:, 256:512]) * (RET_DK ** -0.5)
    emit(1)
    nw = _par(par_ref, R_RET_NW, MIX_W)

    pair = _Pair()
    pos =_iota2((CHUNK, 2 * CHUNK), 0).astype(F32)
    zeros_v = jnp.zeros((CHUNK, RET_DV), F32)
    r2 = _iota2((2 * RET_DK, 2 * RET_DV), 0)
    c2 = _iota2((2 * RET_DK, 2 * RET_DV), 1)
    state_block = (r2 < RET_DK) == (c2 < RET_DV)

    for p in range(RET_HEADS // 2):
        lg0 = math.log1p(-(2.0 ** (-5.0 - 2 * p)))
        lg1 = math.log1p(-(2.0 ** (-5.0 - (2 * p + 1))))
        lg = jnp.where(pair.lo, lg0, lg1)
        dmask = jnp.exp(jnp.where(pair.causal, pair.dist * lg, -jnp.inf))
        k_dec = jnp.exp((CHUNK - 1.0 - pos) * lg)
        q_dec = jnp.exp((pos + 1.0) * lg)
        lg_v = jnp.where(_iota2((1, 2 * RET_DV), 1) < RET_DV, lg0, lg1)
        chunk_decay = jnp.exp(CHUNK * lg_v)
        ls = slice(p * 2 * RET_DK, (p + 1) * 2 * RET_DK)
        vsl = slice(p * 2 * RET_DV, (p + 1) * 2 * RET_DV)
        rsl = lambda c: slice(c * CHUNK, (c + 1) * CHUNK)
        emit(1)
        sc =[_dot1(q[rsl(c), ls], pair.bd(k[rsl(c), ls]), NT) * dmask for c in range(nch)]
        upd = [_dot1(k[rsl(c), ls] * k_dec, v[rsl(c), vsl], TN) for c in range(nch)]
        r_start = []
        r_cur = r_ref[p]
        for c in range(nch):
            r_start.append(r_cur)
            r_cur = r_cur * chunk_decay + jnp.where(state_block, upd[c], 0.0)
        r_ref[p] = r_cur
        for c in range(nch):
            rs = rsl(c)
            vp = v[rs, vsl]
            v_bd = jnp.concatenate(
                [jnp.concatenate([vp[:, :RET_DV], zeros_v], axis=1),
                 jnp.concatenate([zeros_v, vp[:, RET_DV:]], axis=1)], axis=0)
            o = _dot1(sc[c], v_bd) + _dot1(q[rs, ls] * q_dec, r_start[c])
            for hh in range(2):
                sl = slice(vsl.start + hh * RET_DV, vsl.start + (hh + 1) * RET_DV)
                oh = o[:, hh * RET_DV:(hh + 1) * RET_DV]
                mu = jnp.mean(oh, axis=-1, keepdims=True)
                oc = oh - mu
                oh = oc * lax.rsqrt(jnp.mean(oc * oc, axis=-1, keepdims=True) + EPS) * nw[:, sl]
                o_ref[rs, sl] = oh * _silu(gate[rs, sl])


def _retention(a, w, cos, sin, par, l, nch):
    rows = nch * CHUNK
    table_spec = pl.BlockSpec((rows, 256), lambda i: (i, 0))
    return _mixer_call(
        _ret_body, a, w, l, views=((0, 512), (512, 1024), (1024, 1536)),
        extra=(cos, sin, par), extra_specs=(table_spec, table_spec, _layer_spec((PAR_ROWS, PAR_W), l)),
        scratch_shapes=(pltpu.VMEM((RET_HEADS // 2, 2 * RET_DK, 2 * RET_DV), F32),),
        nch=nch)


def _ssd_body(z_ref, xbc_ref, dt_ref, par_ref, btri_ref, eye_ref, expand_ref, o_ref, ext_ref, h_ref,
              *, nch, emit):
    rows = nch * CHUNK

    @pl.when(pl.program_id(0) == 0)
    def _():
        h_ref[...] = jnp.zeros_like(h_ref)

    _stage_with_halo(ext_ref, (xbc_ref,), rows)
    z = z_ref[...]
    dt_in = dt_ref[...]
    emit(2)
    xbc = _silu(_causal_conv(ext_ref, _par(par_ref, R_SSD_CONV, 1024, CONV_W), rows)
                + _par(par_ref, R_SSD_CB, 1024))
    _keep_halo(ext_ref, rows)
    x = xbc[:, 0:512]
    emit(2)

    pair = _Pair()
    dt = _softplus(dt_in + _par(par_ref, R_SSD_DTB, 128))
    g_all = _dot_sel(btri_ref[...], dt * (-jnp.exp(_par(par_ref, R_SSD_ALOG, 128))))
    g_t = _dot_sel(g_all, eye_ref[...], TN)
    wide = _dot_sel(jnp.concatenate([g_all, dt], axis=0), expand_ref[...])
    g_w, dt_w = wide[:rows], wide[rows:]
    gtot_w = _rows_bcast(g_w, nch)
    xdt = x * dt_w
    emit(1)
    xtail = xdt * jnp.exp(gtot_w - g_w)
    eg_w = jnp.exp(g_w)
    etot_w = jnp.exp(gtot_w)
    emit(1)
    d_w =_par(par_ref, R_SSD_D, MIX_W)
    nw = _par(par_ref, R_SSD_NW, MIX_W)
    gw = M2_HEADS // M2_GROUPS * M2_HEADDIM

    combos = [(c, gi) for c in range(nch) for gi in range(M2_GROUPS)]
    rsl = lambda c: slice(c * CHUNK, (c + 1) * CHUNK)
    bg = {(c, gi): xbc[rsl(c), 512 + gi * M2_STATE:512 + (gi + 1) * M2_STATE] for c, gi in combos}
    cg = {(c, gi): xbc[rsl(c), 768 + gi * M2_STATE:768 + (gi + 1) * M2_STATE] for c, gi in combos}
    cb2 = {k: _dot1(cg[k], jnp.concatenate([bg[k], bg[k]], axis=0), NT) for k in combos}
    upd = {(c, gi): _dot1(bg[c, gi], xtail[rsl(c), gi * gw:(gi + 1) * gw], TN) for c, gi in combos}
    y_in = {}
    for c, gi in combos:
        for pp in range(2):
            ha = gi * 4 + 2 * pp
            gcol = jnp.where(pair.lo, g_all[rsl(c), ha:ha + 1], g_all[rsl(c), ha + 1:ha + 2])
            grow = jnp.concatenate([g_t[ha:ha + 1, rsl(c)], g_t[ha + 1:ha + 2, rsl(c)]], axis=1)
            decay = jnp.exp(jnp.where(pair.causal, gcol - grow, -jnp.inf))
            y_in[c, gi, pp] = _dot1(cb2[c, gi] * decay,
                                    pair.bd(xdt[rsl(c), ha * M2_HEADDIM:(ha + 2) * M2_HEADDIM]))
    h_start = {}
    for gi in range(M2_GROUPS):
        h = h_ref[gi]
        for c in range(nch):
            h_start[c, gi] = h
            h = h * etot_w[c * CHUNK:c * CHUNK + 1, gi * gw:(gi + 1) * gw] + upd[c, gi]
        h_ref[gi] = h
    for c, gi in combos:
        rs, gs = rsl(c), slice(gi * gw, (gi + 1) * gw)
        y = jnp.concatenate([y_in[c, gi, 0], y_in[c, gi, 1]], axis=1)
        y = y + _dot1(cg[c, gi], h_start[c, gi]) * eg_w[rs, gs]
        y = (y + d_w[:, gs] * x[rs, gs]) * _silu(z[rs, gs])
        y = y * lax.rsqrt(jnp.mean(y * y, axis=-1, keepdims=True) + EPS) * nw[:, gs]
        o_ref[rs, gs] = y


def _head_expand_matrix(n_heads, width):
    r = jnp.arange(128)[:, None]
    c = jnp.arange(n_heads * width)[None, :]
    return (r == c // width).astype(BF16)


def _ssd(a, w, par, l, nch):
    rows = nch * CHUNK
    return _mixer_call(
        _ssd_body, a, w, l, views=((1024, 1536), (0, 1024), (1536, 1664)),
        extra=(par, _block_diag_ones(rows, CHUNK, lower=True, dtype=BF16), jnp.eye(rows, dtype=BF16),
               _head_expand_matrix(M2_HEADS, M2_HEADDIM)),
        extra_specs=(_layer_spec((PAR_ROWS, PAR_W), l), _const_spec((rows, rows)), _const_spec((rows, rows)),
                     _const_spec((128, 512))),
        scratch_shapes=(pltpu.VMEM((rows + 8, 1024), F32), pltpu.VMEM((M2_GROUPS, M2_STATE, 256), F32)),
        nch=nch)


def _rwkv_body(rkv_ref, lora_ref, par_ref, lmat_ref, btri_ref, hblk_ref, o_ref, ext_ref, s_ref, *, nch, emit):
    rows = nch * CHUNK

    @pl.when(pl.program_id(0) == 0)
    def _():
        s_ref[...] = jnp.zeros_like(s_ref)

    _stage_with_halo(ext_ref, (rkv_ref, lora_ref), rows)
    emit(1)
    cur = ext_ref[pl.ds(8, rows), :]
    prev = ext_ref[pl.ds(7, rows), :]
    _keep_halo(ext_ref, rows)
    mixed = cur + (prev - cur) * _par(par_ref, R_RW_MU, PAR_W)
    r = mixed[:, 0:512]
    k = mixed[:, 512:1024]
    v = mixed[:, 1024:1536]
    lora = mixed[:, 1536:1792]
    emit(1)

    w_raw = -_softplus(-(_par(par_ref, R_RW_W0, MIX_W) + _dot1(jnp.tanh(lora), lmat_ref[0:256, :]))) - 0.5
    log_d = -jnp.exp(w_raw)
    emit(1)
    a = _sigmoid(_par(par_ref, R_RW_A0, MIX_W) + _dot1(lora, lmat_ref[256:512, :]))
    gate = _dot1(_sigmoid(lora), lmat_ref[512:768, :])
    hblk = hblk_ref[...]
    emit(1)

    g_in = _dot_sel(btri_ref[...], log_d)
    g_tot = _rows_bcast(g_in, nch)
    e_in = jnp.exp(g_in)
    e_neg = jnp.exp(-g_in)
    e_ex = jnp.exp(g_in - log_d)
    e_tail = jnp.exp(g_tot - g_in)
    e_end = jnp.exp(g_tot)
    emit(1)

    kk = k * _par(par_ref, R_RW_KK, MIX_W)
    kk = kk * lax.rsqrt(_sum_bcast(kk * kk, hblk) + EPS)
    k_mod = k * (1.0 + (a - 1.0) * _par(par_ref, R_RW_KA, MIX_W))
    a_vec = -(a * kk)
    emit(1)
    r_t = r * e_in
    b_t = kk * e_ex
    k_t = k_mod * e_neg
    a_t = a_vec * e_neg
    k_c = k_mod * e_tail
    a_c = a_vec * e_tail
    emit(1)

    pair = _Pair()
    npair = RW_HEADS // 2
    combos = [(c, j) for c in range(nch) for j in range(npair)]
    cut = lambda arr, cj: arr[cj[0] * CHUNK:(cj[0] + 1) * CHUNK, cj[1] * 2 * RW_N:(cj[1] + 1) * 2 * RW_N]
    lhs = {cj: jnp.concatenate([cut(b_t, cj), cut(r_t, cj)], axis=0) for cj in combos}
    x1 = {cj: _dot1(lhs[cj], pair.bd(cut(k_t, cj)), NT) for cj in combos}
    x2 = {cj: _dot1(lhs[cj], pair.bd(cut(a_t, cj)), NT) for cj in combos}
    a_ra = {cj: jnp.where(pair.causal, x2[cj][CHUNK:], 0.0) for cj in combos}
    tinv = dict(zip(combos, pair.inverse([-jnp.where(pair.strict, x2[cj][:CHUNK], 0.0) for cj in combos])))
    av = {cj: _dot1(jnp.concatenate([jnp.where(pair.strict, x1[cj][:CHUNK], 0.0),
                                     jnp.where(pair.causal, x1[cj][CHUNK:], 0.0)], axis=0),
                    pair.bd(cut(v, cj))) for cj in combos}
    tz = {cj: _dot1(tinv[cj], jnp.concatenate([pair.bd(cut(b_t, cj)), pair.bd(av[cj][:CHUNK])], axis=1))
          for cj in combos}
    az = {cj: _dot1(a_ra[cj], jnp.concatenate([pair.bd(tz[cj][:, :2 * RW_N]), pair.bd(tz[cj][:, 2 * RW_N:])],
                                              axis=1)) for cj in combos}
    r_eff = {cj: cut(r_t, cj) + az[cj][:, :2 * RW_N] for cj in combos}
    y0 = {cj: av[cj][CHUNK:] + az[cj][:, 2 * RW_N:] for cj in combos}
    p_low = {cj: jnp.where(pair.same_block, _dot1(tz[cj][:, :2 * RW_N], cut(a_c, cj), TN), 0.0) for cj in combos}
    q_mat = {cj: jnp.where(pair.same_block,
                           _dot1(jnp.concatenate([cut(v, cj), tz[cj][:, 2 * RW_N:]], axis=0),
                                 jnp.concatenate([cut(k_c, cj), cut(a_c, cj)], axis=0), TN), 0.0)
             for cj in combos}
    y_rows = []
    for c in range(nch):
        s0 = [s_ref[j] if c == 0 else s_new[j] for j in range(npair)]
        y_rows.append(jnp.concatenate(
            [_dot1(r_eff[c, j], s0[j], NT) + y0[c, j] for j in range(npair)], axis=1))
        s_new = [s0[j] * e_end[c * CHUNK:c * CHUNK + 1, j * 2 * RW_N:(j + 1) * 2 * RW_N]
                 + _dot1(s0[j], p_low[c, j]) + q_mat[c, j] for j in range(npair)]
    for j in range(npair):
        s_ref[j] = s_new[j]
    y = jnp.concatenate(y_rows, axis=0)

    inv_n = 1.0 / RW_N
    mu = _sum_bcast(y, hblk, pieces=2) * inv_n
    yc = y - mu
    var = _sum_bcast(yc * yc, hblk) * inv_n
    y = yc * lax.rsqrt(var + RWKV_LN_EPS) * _par(par_ref, R_RW_LNW, MIX_W) + _par(par_ref, R_RW_LNB, MIX_W)
    y = y + _sum_bcast(r * k_mod * _par(par_ref, R_RW_RK, MIX_W), hblk) * v
    o_ref[...] = y * gate


def _rwkv(a, w, par, lmat, l, nch):
    rows = nch * CHUNK
    return _mixer_call(
        _rwkv_body, a, w, l, views=((0, 1536), (1536, 1792)),
        extra=(par, lmat, _block_diag_ones(rows, CHUNK, lower=True, dtype=BF16),
               _block_diag_ones(MIX_W, RW_N, dtype=BF16)),
        extra_specs=(_layer_spec((PAR_ROWS, PAR_W), l), _layer_spec((768, MIX_W), l),
                     _const_spec((rows, rows)), _const_spec((512, 512))),
        scratch_shapes=(pltpu.VMEM((rows + 8, 1792), F32),
                        pltpu.VMEM((RW_HEADS // 2, 2 * RW_N, 2 * RW_N), F32)),
        nch=nch)


def _layout_w_in(w):
    wb = w.astype(BF16)
    out = {}
    for name, pieces in _W_PIECES.items():
        cols = [jnp.pad(wb[:, :, s:s + n], ((0, 0), (0, 0), (0, p - n))) for s, n, p in pieces]
        out[name] = cols[0] if len(cols) == 1 else jnp.concatenate(cols, axis=2)
    return out


def _param_slab(p):
    depth = p['w_in'].shape[0]

    def rows(a, lane=0):
        a = a.astype(F32).reshape(depth, -1, a.shape[-1])
        return jnp.pad(a, ((0, 0), (0, 0), (lane, PAR_W - lane - a.shape[-1])))

    pieces = [rows(p['gdn_conv_w']), rows(p['gdn_a_log'], 4), rows(p['gdn_dt_bias'], 4),
              rows(p['gdn_norm_w']), rows(p['ret_norm_w']),
              rows(p['m2_conv_w']), rows(p['m2_conv_b']), rows(p['m2_a_log']), rows(p['m2_dt_bias']),
              rows(jnp.repeat(p['m2_d'], M2_HEADDIM, axis=1)), rows(p['m2_norm_w']),
              rows(p['rw_mu']), rows(p['rw_w0']), rows(p['rw_a0']), rows(p['rw_k_k']), rows(p['rw_k_a']),
              rows(p['rw_r_k'].reshape(depth, -1)), rows(p['rw_ln_w']), rows(p['rw_ln_b'])]
    slab = jnp.concatenate(pieces, axis=1)
    return jnp.pad(slab, ((0, 0), (0, PAR_ROWS - slab.shape[1]), (0, 0)))


def _lora_mats(p):
    def at(m, offset):
        return jnp.pad(m, ((0, 0), (offset, 256 - offset - m.shape[1]), (0, 0)))
    return jnp.concatenate([at(p['rw_w_up'], 0), at(p['rw_a_up'], RW_W_LORA),
                            at(p['rw_g_up'], RW_W_LORA + RW_A_LORA)], axis=1).astype(BF16)


def _rotary_tables(t):
    theta = 1.0 / (ROPE_BASE ** jnp.linspace(0.0, 1.0, RET_DK // 2, dtype=F32))
    ang = jnp.arange(t, dtype=F32)[:, None] * theta
    cos = jnp.repeat(jnp.cos(ang), 2, axis=1)
    sin = jnp.stack([-jnp.sin(ang), jnp.sin(ang)], axis=-1).reshape(t, RET_DK)
    return jnp.tile(cos, (1, RET_HEADS)), jnp.tile(sin, (1, RET_HEADS))


NCH_GDN, NCH_RET, NCH_SSD, NCH_RWKV = 4, 4, 4, 4


def _prepare(p, t):
    cos, sin = _rotary_tables(t)
    return dict(norm1=p['norm1_w'].astype(F32)[:, None, :], w_in=_layout_w_in(p['w_in']),
                par=_param_slab(p), lmat=_lora_mats(p), w_out=p['w_out'].astype(BF16), cos=cos, sin=sin)


def _token_mix(h, l, q):
    t = h.shape[0]
    nch = lambda n: min(n, t // CHUNK)
    a = _norm_cast(h, q['norm1'], l)
    w = q['w_in']
    o_a = _gdn(a, w['gdn'], q['par'], l, nch(NCH_GDN))
    o_b = _retention(a, w['ret'], q['cos'], q['sin'], q['par'], l, nch(NCH_RET))
    o_c = _ssd(a, w['ssd'], q['par'], l, nch(NCH_SSD))
    o_d = _rwkv(a, w['rwkv'], q['par'], q['lmat'], l, nch(NCH_RWKV))
    return _out_proj(h, (o_a, o_b, o_c, o_d), q['w_out'], l)


def kernel(x, norm1_w, w_in, gdn_conv_w, gdn_a_log, gdn_dt_bias, gdn_norm_w, ret_norm_w, m2_conv_w, m2_conv_b, m2_a_log, m2_dt_bias, m2_d, m2_norm_w, rw_mu, rw_w0, rw_w_up, rw_a0, rw_a_up, rw_g_up, rw_k_k, rw_k_a, rw_r_k, rw_ln_w, rw_ln_b, w_out, norm2_w, w_ffn_up, w_ffn_down, final_norm_w):
    p = dict(norm1_w=norm1_w, w_in=w_in, gdn_conv_w=gdn_conv_w, gdn_a_log=gdn_a_log,
             gdn_dt_bias=gdn_dt_bias, gdn_norm_w=gdn_norm_w, ret_norm_w=ret_norm_w,
             m2_conv_w=m2_conv_w, m2_conv_b=m2_conv_b, m2_a_log=m2_a_log, m2_dt_bias=m2_dt_bias,
             m2_d=m2_d, m2_norm_w=m2_norm_w, rw_mu=rw_mu, rw_w0=rw_w0, rw_w_up=rw_w_up,
             rw_a0=rw_a0, rw_a_up=rw_a_up, rw_g_up=rw_g_up, rw_k_k=rw_k_k, rw_k_a=rw_k_a,
             rw_r_k=rw_r_k, rw_ln_w=rw_ln_w, rw_ln_b=rw_ln_b, w_out=w_out)
    bsz, t, d = x.shape
    depth = w_in.shape[0]
    q = _prepare(p, t)
    norm2 = norm2_w.astype(F32)[:, None, :]
    w_up = w_ffn_up.astype(BF16)
    w_down = w_ffn_down.astype(BF16)
    final_w = final_norm_w.astype(F32).reshape(1, d)
    outs = []
    for b in range(bsz):
        h = x[b]
        for l in range(depth):
            h = _token_mix(h, l, q)
            h = _ffn(h, norm2, w_up, w_down, final_w, l, final_norm=(l == depth - 1))
        outs.append(h)
    return outs[0].reshape(1, t, d) if bsz == 1 else jnp.stack(outs, axis=0)
```

```python
import functools
import math

import jax
import jax.numpy as jnp
from jax import lax
from jax.experimental import pallas as pl
from jax.experimental.pallas import tpu as pltpu

F32 = jnp.float32
BF16 = jnp.bfloat16

D_MODEL = 2048
D_FF = 4 * D_MODEL
CONV_W = 4
CHUNK = 64
EPS = 1e-6
ROPE_BASE = 10000.0
RWKV_LN_EPS = 64e-5

GDN_HEADS, GDN_DK, GDN_DV = 4, 128, 128
RET_HEADS, RET_DK, RET_DV = 4, 64, 128
M2_HEADS, M2_HEADDIM, M2_GROUPS, M2_STATE = 8, 64, 2, 128
RW_HEADS, RW_N = 8, 64
RW_W_LORA, RW_A_LORA, RW_G_LORA = 32, 32, 96
MIX_W = 512

_GDN0, _RET0, _M20, _RW0 = 0, 2056, 3592, 5136

P_PAD = 7168
_SRC_PIECES = (
    (_GDN0, 1536, 1536),
    (_RW0, 1536, 1536),
    (_M20 + 512, 1024, 1024),
    (_GDN0 + 1536, 512, 512),
    (_RET0, 512, 512),
    (_RET0 + 512, 512, 512),
    (_RET0 + 1024, 512, 512),
    (_M20, 512, 512),
    (_GDN0 + 2048, 8, 128),
    (_M20 + 1536, 8, 128),
    (_RW0 + 1536, 160, 256),
)

VMEM_LIMIT = 56 * 1024 * 1024

NN = (((1,), (0,)), ((), ()))
NT = (((1,), (1,)), ((), ()))
TN = (((0,), (0,)), ((), ()))


def _cparams(n_axes):
    return pltpu.CompilerParams(dimension_semantics=("arbitrary",) * n_axes,
                                vmem_limit_bytes=VMEM_LIMIT)


def _dg(a, b, dims=NN, prec=None):
    return lax.dot_general(a, b, dims, preferred_element_type=F32, precision=prec)


def _split(a):
    hi = a.astype(BF16)
    lo = (a - hi.astype(F32)).astype(BF16)
    return hi, lo


def _dot1(a, b, dims=NN):
    return _dg(a.astype(BF16), b.astype(BF16), dims)


def _split3(a):
    hi = a.astype(BF16)
    r1 = a - hi.astype(F32)
    mid = r1.astype(BF16)
    lo = (r1 - mid.astype(F32)).astype(BF16)
    return hi, mid, lo


def _dot_sel(a, b, dims=NN):
    if a.dtype == BF16:
        return sum(_dg(a, piece, dims) for piece in _split3(b))
    return sum(_dg(piece, b, dims) for piece in _split3(a))


def _sum_bcast(x, blk, pieces=1):
    outs = []
    for s in range(0, x.shape[1], 256):
        xs, bs = x[:, s:s + 256], blk[s:s + 256, s:s + 256]
        if pieces == 1:
            outs.append(_dg(xs.astype(BF16), bs))
        else:
            xh, xl = _split(xs)
            outs.append(_dg(xh, bs) + _dg(xl, bs))
    return jnp.concatenate(outs, axis=1)


def _rows_bcast(x, nch):
    return jnp.concatenate(
        [jnp.broadcast_to(x[(c + 1) * CHUNK - 1:(c + 1) * CHUNK, :], (CHUNK, x.shape[1])) for c in range(nch)],
        axis=0)


def _sigmoid(x):
    return 1.0 / (1.0 + jnp.exp(-x))


def _silu(x):
    return x * _sigmoid(x)


def _softplus(x):
    return jnp.maximum(x, 0.0) + jnp.log1p(jnp.exp(-jnp.abs(x)))


def _iota2(shape, axis):
    return lax.broadcasted_iota(jnp.int32, shape, axis)


def _chunk_masks():
    r = _iota2((CHUNK, CHUNK), 0)
    c = _iota2((CHUNK, CHUNK), 1)
    return r >= c, r > c, r == c


class _Pair:
    def __init__(self):
        lane = _iota2((CHUNK, 2 * CHUNK), 1)
        row = _iota2((CHUNK, 2 * CHUNK), 0)
        self.lo = lane < CHUNK
        col = jnp.where(self.lo, lane, lane - CHUNK)
        self.dist = (row - col).astype(F32)
        self.causal = row >= col
        self.strict = row > col
        self.eye = jnp.where(row == col, 1.0, 0.0).astype(F32)
        r2 = _iota2((2 * CHUNK, 2 * CHUNK), 0)
        c2 = _iota2((2 * CHUNK, 2 * CHUNK), 1)
        self.same_block = (r2 < CHUNK) == (c2 < CHUNK)

    def bd(self, x):
        z = jnp.zeros_like(x)
        return jnp.concatenate([jnp.where(self.lo, x, z), jnp.where(self.lo, z, x)], axis=0)

    def inverse(self, lows):
        ps = [self.eye - low for low in lows]
        curs = list(lows)
        n = 1
        while n < CHUNK:
            rhss = [self.bd(cur) for cur in curs]
            if n == 1:
                curs = [_dot1(cur, rhs) for cur, rhs in zip(curs, rhss)]
            elif 2 * n < CHUNK:
                outs = [_dot1(jnp.concatenate([cur, p], axis=0), rhs)
                        for cur, p, rhs in zip(curs, ps, rhss)]
                curs = [out[:CHUNK] for out in outs]
                ps = [p + out[CHUNK:] for p, out in zip(ps, outs)]
            else:
                ps = [p + _dot1(p, rhs) for p, rhs in zip(ps, rhss)]
            n *= 2
        return ps


def _norm_matmul_kernel(x_ref, nw_ref, w_ref, o_ref, a_ref):
    @pl.when(pl.program_id(1) == 0)
    def _():
        x = x_ref[...]
        y = x * lax.rsqrt(jnp.mean(x * x, axis=-1, keepdims=True) + EPS) * nw_ref[...]
        a_ref[...] = y.astype(BF16)

    o_ref[...] = jnp.dot(a_ref[...], w_ref[...], preferred_element_type=F32)


def _norm_matmul(x, nw, w, l, tm=1024, tn=1792):
    t, d = x.shape
    n = w.shape[2]
    tm = min(tm, t)
    return pl.pallas_call(
        _norm_matmul_kernel,
        out_shape=jax.ShapeDtypeStruct((t, n), F32),
        grid=(t // tm, n // tn),
        in_specs=[pl.BlockSpec((tm, d), lambda i, j: (i, 0)),
                  pl.BlockSpec((None, 1, d), lambda i, j: (l, 0, 0)),
                  pl.BlockSpec((None, d, tn), lambda i, j: (l, 0, j))],
        out_specs=pl.BlockSpec((tm, tn), lambda i, j: (i, j)),
        scratch_shapes=[pltpu.VMEM((tm, d), BF16)],
        compiler_params=_cparams(2),
    )(x, nw, w)


def _out_proj_kernel(h_ref, oa_ref, ob_ref, oc_ref, od_ref, w_ref, o_ref):
    acc = h_ref[...]
    for idx, part in enumerate((oa_ref, ob_ref, oc_ref, od_ref)):
        acc = acc + jnp.dot(part[...].astype(BF16), w_ref[idx * MIX_W:(idx + 1) * MIX_W, :],
                            preferred_element_type=F32)
    o_ref[...] = acc


def _out_proj(h, parts, w, l, tm=512):
    t, d = h.shape
    tm = min(tm, t)
    part_spec = pl.BlockSpec((tm, MIX_W), lambda i: (i, 0))
    return pl.pallas_call(
        _out_proj_kernel,
        out_shape=jax.ShapeDtypeStruct((t, d), F32),
        grid=(t // tm,),
        in_specs=[pl.BlockSpec((tm, d), lambda i: (i, 0)),
                  part_spec, part_spec, part_spec, part_spec,
                  pl.BlockSpec((None, 4 * MIX_W, d), lambda i: (l, 0, 0))],
        out_specs=pl.BlockSpec((tm, d), lambda i: (i, 0)),
        compiler_params=_cparams(1),
    )(h, *parts, w)


def _ffn_kernel(h_ref, nw_ref, wu_ref, wd_ref, fw_ref, o_ref, a_ref, *, final_norm):
    f = pl.program_id(1)

    @pl.when(f == 0)
    def _():
        x = h_ref[...]
        y = x * lax.rsqrt(jnp.mean(x * x, axis=-1, keepdims=True) + EPS) * nw_ref[...]
        a_ref[...] = y.astype(BF16)
        o_ref[...] = x

    u = jnp.dot(a_ref[...], wu_ref[...], preferred_element_type=F32)
    s = jnp.square(jnp.maximum(u, 0.0)).astype(BF16)
    o_ref[...] += jnp.dot(s, wd_ref[...], preferred_element_type=F32)

    if final_norm:
        @pl.when(f == pl.num_programs(1) - 1)
        def _():
            y = o_ref[...]
            o_ref[...] = y * lax.rsqrt(jnp.mean(y * y, axis=-1, keepdims=True) + EPS) * fw_ref[...]


def _ffn(h, nw, wu, wd, fw, l, final_norm, tm=1024, tf=512):
    t, d = h.shape
    ff = wu.shape[2]
    tm = min(tm, t)
    return pl.pallas_call(
        functools.partial(_ffn_kernel, final_norm=final_norm),
        out_shape=jax.ShapeDtypeStruct((t, d), F32),
        grid=(t // tm, ff // tf),
        in_specs=[pl.BlockSpec((tm, d), lambda i, f: (i, 0)),
                  pl.BlockSpec((None, 1, d), lambda i, f: (l, 0, 0)),
                  pl.BlockSpec((None, d, tf), lambda i, f: (l, 0, f)),
                  pl.BlockSpec((None, tf, d), lambda i, f: (l, f, 0)),
                  pl.BlockSpec((1, d), lambda i, f: (0, 0))],
        out_specs=pl.BlockSpec((tm, d), lambda i, f: (i, 0)),
        scratch_shapes=[pltpu.VMEM((tm, d), BF16)],
        compiler_params=_cparams(2),
    )(h, nw, wu, wd, fw)


def _stage_with_halo(ext_ref, pieces, rows):
    @pl.when(pl.program_id(0) == 0)
    def _():
        ext_ref[0:8, :] = jnp.zeros((8, ext_ref.shape[1]), F32)

    col = 0
    for ref in pieces:
        w = ref.shape[1]
        ext_ref[8:8 + rows, col:col + w] = ref[...]
        col += w


def _keep_halo(ext_ref, rows):
    ext_ref[0:8, :] = ext_ref[rows:rows + 8, :]


def _causal_conv(ext_ref, cw, rows):
    acc = ext_ref[pl.ds(8, rows), :] * cw[3:4, :]
    for i in range(CONV_W - 1):
        acc = acc + ext_ref[pl.ds(5 + i, rows), :] * cw[i:i + 1, :]
    return acc


def _const_spec(shape):
    return pl.BlockSpec(shape, lambda i: (0,) * len(shape))


PAR_ROWS, PAR_W = 32, 1792
(R_GDN_CONV, R_GDN_ALOG, R_GDN_DTB, R_GDN_NW, R_RET_NW, R_SSD_CONV, R_SSD_CB, R_SSD_ALOG, R_SSD_DTB,
 R_SSD_D, R_SSD_NW, R_RW_MU, R_RW_W0, R_RW_A0, R_RW_KK, R_RW_KA, R_RW_RK, R_RW_LNW, R_RW_LNB) = (
    0, 4, 5, 6, 7, 8, 12, 13, 14, 15, 16, 17, 18, 19, 20, 21, 22, 23, 24)


def _layer_spec(shape, l):
    return pl.BlockSpec((None,) + tuple(shape), lambda i: (l,) + (0,) * len(shape))


def _par(par_ref, row, width, nrows=1):
    return par_ref[row:row + nrows, 0:width]


def _block_diag_ones(n, block, lower=False, dtype=F32):
    r = jnp.arange(n)[:, None]
    c = jnp.arange(n)[None, :]
    m = (r // block) == (c // block)
    if lower:
        m = m & (r >= c)
    return m.astype(dtype)


def _gdn_kernel(qkv_ref, z_ref, gate_ref, par_ref, btri_ref, eye_ref, o_ref, ext_ref, s_ref, *, nch):
    rows = nch * CHUNK

    @pl.when(pl.program_id(0) == 0)
    def _():
        s_ref[...] = jnp.zeros_like(s_ref)

    _stage_with_halo(ext_ref, (qkv_ref,), rows)
    qkv = _silu(_causal_conv(ext_ref, _par(par_ref, R_GDN_CONV, 1536, CONV_W), rows))
    _keep_halo(ext_ref, rows)

    causal, strict, _ = _chunk_masks()
    pair = _Pair()

    gt = gate_ref[...]
    beta = _sigmoid(gt)
    log_a = -jnp.exp(_par(par_ref, R_GDN_ALOG, 128)) * _softplus(gt + _par(par_ref, R_GDN_DTB, 128))
    g_all = _dot_sel(btri_ref[...], log_a)
    g_t = _dot_sel(g_all, eye_ref[...], TN)
    z = z_ref[...]
    nw = _par(par_ref, R_GDN_NW, GDN_DV)

    qs, ks, vs = [], [], []
    for h in range(GDN_HEADS):
        qh = qkv[:, h * GDN_DK:(h + 1) * GDN_DK]
        kh = qkv[:, 512 + h * GDN_DK:512 + (h + 1) * GDN_DK]
        qs.append(qh * lax.rsqrt(jnp.sum(qh * qh, axis=-1, keepdims=True) + EPS) * (GDN_DK ** -0.5))
        ks.append(kh * lax.rsqrt(jnp.sum(kh * kh, axis=-1, keepdims=True) + EPS))
        vs.append(qkv[:, 1024 + h * GDN_DV:1024 + (h + 1) * GDN_DV])

    combos = [(c, h) for c in range(nch) for h in range(GDN_HEADS)]
    rsl = lambda c: slice(c * CHUNK, (c + 1) * CHUNK)
    bcol = {(c, h): beta[rsl(c), h:h + 1] for c, h in combos}
    gcol = {(c, h): g_all[rsl(c), 4 + h:5 + h] for c, h in combos}
    kq = {(c, h): _dot1(jnp.concatenate([ks[h][rsl(c)], qs[h][rsl(c)]], axis=0), ks[h][rsl(c)], NT)
          for c, h in combos}
    lows, qks = {}, {}
    for c, h in combos:
        grow = g_t[4 + h:5 + h, c * CHUNK:(c + 1) * CHUNK]
        gamma = jnp.exp(jnp.where(causal, gcol[c, h] - grow, -jnp.inf))
        lows[c, h] = jnp.where(strict, kq[c, h][:CHUNK] * gamma * bcol[c, h], 0.0)
        qks[c, h] = kq[c, h][CHUNK:] * gamma
    pairs = [(c, p) for c in range(nch) for p in range(GDN_HEADS // 2)]
    tps = pair.inverse([jnp.concatenate([lows[c, 2 * p], lows[c, 2 * p + 1]], axis=1) for c, p in pairs])
    tinv = {}
    for (c, p), tp in zip(pairs, tps):
        tinv[c, 2 * p] = tp[:, :CHUNK]
        tinv[c, 2 * p + 1] = tp[:, CHUNK:]
    egc = {ch: jnp.exp(gcol[ch]) for ch in combos}
    glast = {ch: gcol[ch][CHUNK - 1:CHUNK, :] for ch in combos}
    wu = {(c, h): _dot1(tinv[c, h], jnp.concatenate(
        [ks[h][rsl(c)] * (bcol[c, h] * egc[c, h]), vs[h][rsl(c)] * bcol[c, h]], axis=1)) for c, h in combos}
    qwu = {ch: _dot1(qks[ch], wu[ch]) for ch in combos}
    kwu = {(c, h): _dot1(ks[h][rsl(c)] * jnp.exp(glast[c, h] - gcol[c, h]), wu[c, h], TN) for c, h in combos}
    for c in range(nch):
        rs = rsl(c)
        s0 = [s_ref[h] if c == 0 else s_new[h] for h in range(GDN_HEADS)]
        s_new = []
        for h in range(GDN_HEADS):
            sl = slice(h * GDN_DV, (h + 1) * GDN_DV)
            q_eff = qs[h][rs] * egc[c, h] - qwu[c, h][:, :GDN_DK]
            o = _dot1(q_eff, s0[h]) + qwu[c, h][:, GDN_DK:]
            s_new.append(s0[h] * jnp.exp(glast[c, h]) - _dot1(kwu[c, h][:, :GDN_DK], s0[h])
                         + kwu[c, h][:, GDN_DK:])
            o = o * lax.rsqrt(jnp.mean(o * o, axis=-1, keepdims=True) + EPS) * nw
            o_ref[rs, sl] = o * _silu(z[rs, sl])
    for h in range(GDN_HEADS):
        s_ref[h] = s_new[h]


def _gdn(proj, par, l, nch):
    t = proj.shape[0]
    rows = nch * CHUNK
    return pl.pallas_call(
        functools.partial(_gdn_kernel, nch=nch),
        out_shape=jax.ShapeDtypeStruct((t, MIX_W), F32),
        grid=(t // rows,),
        in_specs=[pl.BlockSpec((rows, 1536), lambda i: (i, 0)),
                  pl.BlockSpec((rows, 512), lambda i: (i, 8)),
                  pl.BlockSpec((rows, 128), lambda i: (i, 52)),
                  _layer_spec((PAR_ROWS, PAR_W), l),
                  _const_spec((rows, rows)), _const_spec((rows, rows))],
        out_specs=pl.BlockSpec((rows, MIX_W), lambda i: (i, 0)),
        scratch_shapes=[pltpu.VMEM((rows + 8, 1536), F32),
                        pltpu.VMEM((GDN_HEADS, GDN_DK, GDN_DV), F32)],
        compiler_params=_cparams(1),
    )(proj, proj, proj, par, _block_diag_ones(rows, CHUNK, lower=True, dtype=BF16),
      jnp.eye(rows, dtype=BF16))


def _ret_kernel(qk_ref, v_ref, g_ref, cos_ref, sin_ref, par_ref, o_ref, r_ref, *, nch):
    @pl.when(pl.program_id(0) == 0)
    def _():
        r_ref[...] = jnp.zeros_like(r_ref)

    rows = nch * CHUNK
    qk = qk_ref[...]
    cos = cos_ref[...]
    sin = sin_ref[...]
    even = (_iota2((rows, 256), 1) % 2) == 0

    def rot(x):
        partner = jnp.where(even, pltpu.roll(x, 255, 1), pltpu.roll(x, 1, 1))
        return x * cos + partner * sin

    q = rot(qk[:, 0:256])
    k = rot(qk[:, 256:512]) * (RET_DK ** -0.5)
    v = v_ref[...]
    gate = g_ref[...]
    nw = _par(par_ref, R_RET_NW, MIX_W)

    pair = _Pair()
    pos = _iota2((CHUNK, 2 * CHUNK), 0).astype(F32)
    zeros_v = jnp.zeros((CHUNK, RET_DV), F32)
    r2 = _iota2((2 * RET_DK, 2 * RET_DV), 0)
    c2 = _iota2((2 * RET_DK, 2 * RET_DV), 1)
    state_block = (r2 < RET_DK) == (c2 < RET_DV)

    for p in range(RET_HEADS // 2):
        lg0 = math.log1p(-(2.0 ** (-5.0 - 2 * p)))
        lg1 = math.log1p(-(2.0 ** (-5.0 - (2 * p + 1))))
        lg = jnp.where(pair.lo, lg0, lg1)
        dmask = jnp.exp(jnp.where(pair.causal, pair.dist * lg, -jnp.inf))
        k_dec = jnp.exp((CHUNK - 1.0 - pos) * lg)
        q_dec = jnp.exp((pos + 1.0) * lg)
        lg_v = jnp.where(_iota2((1, 2 * RET_DV), 1) < RET_DV, lg0, lg1)
        chunk_decay = jnp.exp(CHUNK * lg_v)
        ls = slice(p * 2 * RET_DK, (p + 1) * 2 * RET_DK)
        vsl = slice(p * 2 * RET_DV, (p + 1) * 2 * RET_DV)
        rsl = lambda c: slice(c * CHUNK, (c + 1) * CHUNK)
        sc = [_dot1(q[rsl(c), ls], pair.bd(k[rsl(c), ls]), NT) * dmask for c in range(nch)]
        upd = [_dot1(k[rsl(c), ls] * k_dec, v[rsl(c), vsl], TN) for c in range(nch)]
        r_start = []
        r_cur = r_ref[p]
        for c in range(nch):
            r_start.append(r_cur)
            r_cur = r_cur * chunk_decay + jnp.where(state_block, upd[c], 0.0)
        r_ref[p] = r_cur
        for c in range(nch):
            rs = rsl(c)
            vp = v[rs, vsl]
            v_bd = jnp.concatenate(
                [jnp.concatenate([vp[:, :RET_DV], zeros_v], axis=1),
                 jnp.concatenate([zeros_v, vp[:, RET_DV:]], axis=1)], axis=0)
            o = _dot1(sc[c], v_bd) + _dot1(q[rs, ls] * q_dec, r_start[c])
            for hh in range(2):
                sl = slice(vsl.start + hh * RET_DV, vsl.start + (hh + 1) * RET_DV)
                oh = o[:, hh * RET_DV:(hh + 1) * RET_DV]
                mu = jnp.mean(oh, axis=-1, keepdims=True)
                oc = oh - mu
                oh = oc * lax.rsqrt(jnp.mean(oc * oc, axis=-1, keepdims=True) + EPS) * nw[:, sl]
                o_ref[rs, sl] = oh * _silu(gate[rs, sl])


def _retention(proj, cos, sin, par, l, nch):
    t = proj.shape[0]
    rows = nch * CHUNK
    return pl.pallas_call(
        functools.partial(_ret_kernel, nch=nch),
        out_shape=jax.ShapeDtypeStruct((t, MIX_W), F32),
        grid=(t // rows,),
        in_specs=[pl.BlockSpec((rows, 512), lambda i: (i, 9)),
                  pl.BlockSpec((rows, 512), lambda i: (i, 10)),
                  pl.BlockSpec((rows, 512), lambda i: (i, 11)),
                  pl.BlockSpec((rows, 256), lambda i: (i, 0)),
                  pl.BlockSpec((rows, 256), lambda i: (i, 0)),
                  _layer_spec((PAR_ROWS, PAR_W), l)],
        out_specs=pl.BlockSpec((rows, MIX_W), lambda i: (i, 0)),
        scratch_shapes=[pltpu.VMEM((RET_HEADS // 2, 2 * RET_DK, 2 * RET_DV), F32)],
        compiler_params=_cparams(1),
    )(proj, proj, proj, cos, sin, par)


def _ssd_kernel(z_ref, xbc_ref, dt_ref, par_ref, btri_ref, eye_ref, expand_ref, o_ref, ext_ref, h_ref,
                *, nch):
    rows = nch * CHUNK

    @pl.when(pl.program_id(0) == 0)
    def _():
        h_ref[...] = jnp.zeros_like(h_ref)

    _stage_with_halo(ext_ref, (xbc_ref,), rows)
    xbc = _silu(_causal_conv(ext_ref, _par(par_ref, R_SSD_CONV, 1024, CONV_W), rows)
                + _par(par_ref, R_SSD_CB, 1024))
    _keep_halo(ext_ref, rows)
    x = xbc[:, 0:512]

    pair = _Pair()
    dt = _softplus(dt_ref[...] + _par(par_ref, R_SSD_DTB, 128))
    g_all = _dot_sel(btri_ref[...], dt * (-jnp.exp(_par(par_ref, R_SSD_ALOG, 128))))
    g_t = _dot_sel(g_all, eye_ref[...], TN)
    wide = _dot_sel(jnp.concatenate([g_all, dt], axis=0), expand_ref[...])
    g_w, dt_w = wide[:rows], wide[rows:]
    gtot_w = _rows_bcast(g_w, nch)
    xdt = x * dt_w
    xtail = xdt * jnp.exp(gtot_w - g_w)
    eg_w = jnp.exp(g_w)
    etot_w = jnp.exp(gtot_w)
    z = z_ref[...]
    d_w = _par(par_ref, R_SSD_D, MIX_W)
    nw = _par(par_ref, R_SSD_NW, MIX_W)
    gw = M2_HEADS // M2_GROUPS * M2_HEADDIM

    combos = [(c, gi) for c in range(nch) for gi in range(M2_GROUPS)]
    rsl = lambda c: slice(c * CHUNK, (c + 1) * CHUNK)
    bg = {(c, gi): xbc[rsl(c), 512 + gi * M2_STATE:512 + (gi + 1) * M2_STATE] for c, gi in combos}
    cg = {(c, gi): xbc[rsl(c), 768 + gi * M2_STATE:768 + (gi + 1) * M2_STATE] for c, gi in combos}
    cb2 = {k: _dot1(cg[k], jnp.concatenate([bg[k], bg[k]], axis=0), NT) for k in combos}
    upd = {(c, gi): _dot1(bg[c, gi], xtail[rsl(c), gi * gw:(gi + 1) * gw], TN) for c, gi in combos}
    y_in = {}
    for c, gi in combos:
        for pp in range(2):
            ha = gi * 4 + 2 * pp
            gcol = jnp.where(pair.lo, g_all[rsl(c), ha:ha + 1], g_all[rsl(c), ha + 1:ha + 2])
            grow = jnp.concatenate([g_t[ha:ha + 1, rsl(c)], g_t[ha + 1:ha + 2, rsl(c)]], axis=1)
            decay = jnp.exp(jnp.where(pair.causal, gcol - grow, -jnp.inf))
            y_in[c, gi, pp] = _dot1(cb2[c, gi] * decay,
                                    pair.bd(xdt[rsl(c), ha * M2_HEADDIM:(ha + 2) * M2_HEADDIM]))
    h_start = {}
    for gi in range(M2_GROUPS):
        h = h_ref[gi]
        for c in range(nch):
            h_start[c, gi] = h
            h = h * etot_w[c * CHUNK:c * CHUNK + 1, gi * gw:(gi + 1) * gw] + upd[c, gi]
        h_ref[gi] = h
    for c, gi in combos:
        rs, gs = rsl(c), slice(gi * gw, (gi + 1) * gw)
        y = jnp.concatenate([y_in[c, gi, 0], y_in[c, gi, 1]], axis=1)
        y = y + _dot1(cg[c, gi], h_start[c, gi]) * eg_w[rs, gs]
        y = (y + d_w[:, gs] * x[rs, gs]) * _silu(z[rs, gs])
        y = y * lax.rsqrt(jnp.mean(y * y, axis=-1, keepdims=True) + EPS) * nw[:, gs]
        o_ref[rs, gs] = y


def _head_expand_matrix(n_heads, width):
    r = jnp.arange(128)[:, None]
    c = jnp.arange(n_heads * width)[None, :]
    return (r == c // width).astype(BF16)


def _ssd(proj, par, l, nch):
    t = proj.shape[0]
    rows = nch * CHUNK
    return pl.pallas_call(
        functools.partial(_ssd_kernel, nch=nch),
        out_shape=jax.ShapeDtypeStruct((t, MIX_W), F32),
        grid=(t // rows,),
        in_specs=[pl.BlockSpec((rows, 512), lambda i: (i, 12)),
                  pl.BlockSpec((rows, 1024), lambda i: (i, 3)),
                  pl.BlockSpec((rows, 128), lambda i: (i, 53)),
                  _layer_spec((PAR_ROWS, PAR_W), l),
                  _const_spec((rows, rows)), _const_spec((rows, rows)), _const_spec((128, 512))],
        out_specs=pl.BlockSpec((rows, MIX_W), lambda i: (i, 0)),
        scratch_shapes=[pltpu.VMEM((rows + 8, 1024), F32),
                        pltpu.VMEM((M2_GROUPS, M2_STATE, 256), F32)],
        compiler_params=_cparams(1),
    )(proj, proj, proj, par,
      _block_diag_ones(rows, CHUNK, lower=True, dtype=BF16), jnp.eye(rows, dtype=BF16),
      _head_expand_matrix(M2_HEADS, M2_HEADDIM))


def _rwkv_kernel(rkv_ref, lora_ref, par_ref, lmat_ref, btri_ref, hblk_ref, o_ref, ext_ref, s_ref, *, nch):
    rows = nch * CHUNK

    @pl.when(pl.program_id(0) == 0)
    def _():
        s_ref[...] = jnp.zeros_like(s_ref)

    _stage_with_halo(ext_ref, (rkv_ref, lora_ref), rows)
    cur = ext_ref[pl.ds(8, rows), :]
    prev = ext_ref[pl.ds(7, rows), :]
    _keep_halo(ext_ref, rows)
    mixed = cur + (prev - cur) * _par(par_ref, R_RW_MU, PAR_W)
    r = mixed[:, 0:512]
    k = mixed[:, 512:1024]
    v = mixed[:, 1024:1536]
    lora = mixed[:, 1536:1792]

    w_raw = -_softplus(-(_par(par_ref, R_RW_W0, MIX_W) + _dot1(jnp.tanh(lora), lmat_ref[0:256, :]))) - 0.5
    log_d = -jnp.exp(w_raw)
    a = _sigmoid(_par(par_ref, R_RW_A0, MIX_W) + _dot1(lora, lmat_ref[256:512, :]))
    gate = _dot1(_sigmoid(lora), lmat_ref[512:768, :])
    hblk = hblk_ref[...]

    g_in = _dot_sel(btri_ref[...], log_d)
    g_tot = _rows_bcast(g_in, nch)
    e_in = jnp.exp(g_in)
    e_neg = jnp.exp(-g_in)
    e_ex = jnp.exp(g_in - log_d)
    e_tail = jnp.exp(g_tot - g_in)
    e_end = jnp.exp(g_tot)

    kk = k * _par(par_ref, R_RW_KK, MIX_W)
    kk = kk * lax.rsqrt(_sum_bcast(kk * kk, hblk) + EPS)
    k_mod = k * (1.0 + (a - 1.0) * _par(par_ref, R_RW_KA, MIX_W))
    a_vec = -(a * kk)
    r_t = r * e_in
    b_t = kk * e_ex
    k_t = k_mod * e_neg
    a_t = a_vec * e_neg
    k_c = k_mod * e_tail
    a_c = a_vec * e_tail

    pair = _Pair()
    npair = RW_HEADS // 2
    combos = [(c, j) for c in range(nch) for j in range(npair)]
    cut = lambda arr, cj: arr[cj[0] * CHUNK:(cj[0] + 1) * CHUNK, cj[1] * 2 * RW_N:(cj[1] + 1) * 2 * RW_N]
    lhs = {cj: jnp.concatenate([cut(b_t, cj), cut(r_t, cj)], axis=0) for cj in combos}
    x1 = {cj: _dot1(lhs[cj], pair.bd(cut(k_t, cj)), NT) for cj in combos}
    x2 = {cj: _dot1(lhs[cj], pair.bd(cut(a_t, cj)), NT) for cj in combos}
    a_ra = {cj: jnp.where(pair.causal, x2[cj][CHUNK:], 0.0) for cj in combos}
    tinv = dict(zip(combos, pair.inverse([-jnp.where(pair.strict, x2[cj][:CHUNK], 0.0) for cj in combos])))
    av = {cj: _dot1(jnp.concatenate([jnp.where(pair.strict, x1[cj][:CHUNK], 0.0),
                                     jnp.where(pair.causal, x1[cj][CHUNK:], 0.0)], axis=0),
                    pair.bd(cut(v, cj))) for cj in combos}
    tz = {cj: _dot1(tinv[cj], jnp.concatenate([pair.bd(cut(b_t, cj)), pair.bd(av[cj][:CHUNK])], axis=1))
          for cj in combos}
    az = {cj: _dot1(a_ra[cj], jnp.concatenate([pair.bd(tz[cj][:, :2 * RW_N]), pair.bd(tz[cj][:, 2 * RW_N:])],
                                              axis=1)) for cj in combos}
    r_eff = {cj: cut(r_t, cj) + az[cj][:, :2 * RW_N] for cj in combos}
    y0 = {cj: av[cj][CHUNK:] + az[cj][:, 2 * RW_N:] for cj in combos}
    p_low = {cj: jnp.where(pair.same_block, _dot1(tz[cj][:, :2 * RW_N], cut(a_c, cj), TN), 0.0) for cj in combos}
    q_mat = {cj: jnp.where(pair.same_block,
                           _dot1(jnp.concatenate([cut(v, cj), tz[cj][:, 2 * RW_N:]], axis=0),
                                 jnp.concatenate([cut(k_c, cj), cut(a_c, cj)], axis=0), TN), 0.0)
             for cj in combos}
    y_rows = []
    for c in range(nch):
        s0 = [s_ref[j] if c == 0 else s_new[j] for j in range(npair)]
        y_rows.append(jnp.concatenate(
            [_dot1(r_eff[c, j], s0[j], NT) + y0[c, j] for j in range(npair)], axis=1))
        s_new = [s0[j] * e_end[c * CHUNK:c * CHUNK + 1, j * 2 * RW_N:(j + 1) * 2 * RW_N]
                 + _dot1(s0[j], p_low[c, j]) + q_mat[c, j] for j in range(npair)]
    for j in range(npair):
        s_ref[j] = s_new[j]
    y = jnp.concatenate(y_rows, axis=0)

    inv_n = 1.0 / RW_N
    mu = _sum_bcast(y, hblk, pieces=2) * inv_n
    yc = y - mu
    var = _sum_bcast(yc * yc, hblk) * inv_n
    y = yc * lax.rsqrt(var + RWKV_LN_EPS) * _par(par_ref, R_RW_LNW, MIX_W) + _par(par_ref, R_RW_LNB, MIX_W)
    y = y + _sum_bcast(r * k_mod * _par(par_ref, R_RW_RK, MIX_W), hblk) * v
    o_ref[...] = y * gate


def _rwkv(proj, par, lmat, l, nch):
    t = proj.shape[0]
    rows = nch * CHUNK
    return pl.pallas_call(
        functools.partial(_rwkv_kernel, nch=nch),
        out_shape=jax.ShapeDtypeStruct((t, MIX_W), F32),
        grid=(t // rows,),
        in_specs=[pl.BlockSpec((rows, 1536), lambda i: (i, 1)),
                  pl.BlockSpec((rows, 256), lambda i: (i, 27)),
                  _layer_spec((PAR_ROWS, PAR_W), l), _layer_spec((768, MIX_W), l),
                  _const_spec((rows, rows)), _const_spec((512, 512))],
        out_specs=pl.BlockSpec((rows, MIX_W), lambda i: (i, 0)),
        scratch_shapes=[pltpu.VMEM((rows + 8, 1792), F32),
                        pltpu.VMEM((RW_HEADS // 2, 2 * RW_N, 2 * RW_N), F32)],
        compiler_params=_cparams(1),
    )(proj, proj, par, lmat,
      _block_diag_ones(rows, CHUNK, lower=True, dtype=BF16), _block_diag_ones(MIX_W, RW_N, dtype=BF16))


def _layout_w_in_kernel(w_ref, o_ref):
    x = w_ref[...]
    col = 0
    for s, n, p in _SRC_PIECES:
        o_ref[:, col:col + n] = x[:, s:s + n].astype(BF16)
        if p > n:
            o_ref[:, col + n:col + p] = jnp.zeros((x.shape[0], p - n), BF16)
        col += p


def _layout_w_in(w, tr=256):
    depth, d, n = w.shape
    return pl.pallas_call(
        _layout_w_in_kernel,
        out_shape=jax.ShapeDtypeStruct((depth, d, P_PAD), BF16),
        grid=(depth, d // tr),
        in_specs=[pl.BlockSpec((None, tr, n), lambda l, i: (l, i, 0))],
        out_specs=pl.BlockSpec((None, tr, P_PAD), lambda l, i: (l, i, 0)),
        compiler_params=_cparams(2),
    )(w)


def _param_slab(p):
    depth = p['w_in'].shape[0]

    def rows(a, lane=0):
        a = a.astype(F32).reshape(depth, -1, a.shape[-1])
        return jnp.pad(a, ((0, 0), (0, 0), (lane, PAR_W - lane - a.shape[-1])))

    pieces = [rows(p['gdn_conv_w']), rows(p['gdn_a_log'], 4), rows(p['gdn_dt_bias'], 4),
              rows(p['gdn_norm_w']), rows(p['ret_norm_w']),
              rows(p['m2_conv_w']), rows(p['m2_conv_b']), rows(p['m2_a_log']), rows(p['m2_dt_bias']),
              rows(jnp.repeat(p['m2_d'], M2_HEADDIM, axis=1)), rows(p['m2_norm_w']),
              rows(p['rw_mu']), rows(p['rw_w0']), rows(p['rw_a0']), rows(p['rw_k_k']), rows(p['rw_k_a']),
              rows(p['rw_r_k'].reshape(depth, -1)), rows(p['rw_ln_w']), rows(p['rw_ln_b'])]
    slab = jnp.concatenate(pieces, axis=1)
    return jnp.pad(slab, ((0, 0), (0, PAR_ROWS - slab.shape[1]), (0, 0)))


def _lora_mats(p):
    def at(m, offset):
        return jnp.pad(m, ((0, 0), (offset, 256 - offset - m.shape[1]), (0, 0)))
    return jnp.concatenate([at(p['rw_w_up'], 0), at(p['rw_a_up'], RW_W_LORA),
                            at(p['rw_g_up'], RW_W_LORA + RW_A_LORA)], axis=1).astype(BF16)


def _rotary_tables(t):
    theta = 1.0 / (ROPE_BASE ** jnp.linspace(0.0, 1.0, RET_DK // 2, dtype=F32))
    ang = jnp.arange(t, dtype=F32)[:, None] * theta
    cos = jnp.repeat(jnp.cos(ang), 2, axis=1)
    sin = jnp.stack([-jnp.sin(ang), jnp.sin(ang)], axis=-1).reshape(t, RET_DK)
    return jnp.tile(cos, (1, RET_HEADS)), jnp.tile(sin, (1, RET_HEADS))


NCH_GDN, NCH_RET, NCH_SSD, NCH_RWKV = 4, 4, 4, 4


def _prepare(p, t):
    cos, sin = _rotary_tables(t)
    return dict(norm1=p['norm1_w'].astype(F32)[:, None, :], w_in=_layout_w_in(p['w_in']),
                par=_param_slab(p), lmat=_lora_mats(p), w_out=p['w_out'].astype(BF16), cos=cos, sin=sin)


def _token_mix(h, l, q):
    t = h.shape[0]
    nch = lambda n: min(n, t // CHUNK)
    proj = _norm_matmul(h, q['norm1'], q['w_in'], l)
    o_a = _gdn(proj, q['par'], l, nch(NCH_GDN))
    o_b = _retention(proj, q['cos'], q['sin'], q['par'], l, nch(NCH_RET))
    o_c = _ssd(proj, q['par'], l, nch(NCH_SSD))
    o_d = _rwkv(proj, q['par'], q['lmat'], l, nch(NCH_RWKV))
    return _out_proj(h, (o_a, o_b, o_c, o_d), q['w_out'], l)


def kernel(x, norm1_w, w_in, gdn_conv_w, gdn_a_log, gdn_dt_bias, gdn_norm_w, ret_norm_w, m2_conv_w, m2_conv_b, m2_a_log, m2_dt_bias, m2_d, m2_norm_w, rw_mu, rw_w0, rw_w_up, rw_a0, rw_a_up, rw_g_up, rw_k_k, rw_k_a, rw_r_k, rw_ln_w, rw_ln_b, w_out, norm2_w, w_ffn_up, w_ffn_down, final_norm_w):
    p = dict(norm1_w=norm1_w, w_in=w_in, gdn_conv_w=gdn_conv_w, gdn_a_log=gdn_a_log,
             gdn_dt_bias=gdn_dt_bias, gdn_norm_w=gdn_norm_w, ret_norm_w=ret_norm_w,
             m2_conv_w=m2_conv_w, m2_conv_b=m2_conv_b, m2_a_log=m2_a_log, m2_dt_bias=m2_dt_bias,
             m2_d=m2_d, m2_norm_w=m2_norm_w, rw_mu=rw_mu, rw_w0=rw_w0, rw_w_up=rw_w_up,
             rw_a0=rw_a0, rw_a_up=rw_a_up, rw_g_up=rw_g_up, rw_k_k=rw_k_k, rw_k_a=rw_k_a,
             rw_r_k=rw_r_k, rw_ln_w=rw_ln_w, rw_ln_b=rw_ln_b, w_out=w_out)
    bsz, t, d = x.shape
    depth = w_in.shape[0]
    q = _prepare(p, t)
    norm2 = norm2_w.astype(F32)[:, None, :]
    w_up = w_ffn_up.astype(BF16)
    w_down = w_ffn_down.astype(BF16)
    final_w = final_norm_w.astype(F32).reshape(1, d)
    outs = []
    for b in range(bsz):
        h = x[b]
        for l in range(depth):
            h = _token_mix(h, l, q)
            h = _ffn(h, norm2, w_up, w_down, final_w, l, final_norm=(l == depth - 1))
        outs.append(h)
    return outs[0].reshape(1, t, d) if bsz == 1 else jnp.stack(outs, axis=0)
```

```python
import functools
import math

import jax
import jax.numpy as jnp
from jax import lax
from jax.experimental import pallas as pl
from jax.experimental.pallas import tpu as pltpu

F32 = jnp.float32
BF16 = jnp.bfloat16

D_MODEL = 2048
D_FF = 4 * D_MODEL
CONV_W = 4
CHUNK = 64
EPS = 1e-6
ROPE_BASE = 10000.0
RWKV_LN_EPS = 64e-5

GDN_HEADS, GDN_DK, GDN_DV = 4, 128, 128
RET_HEADS, RET_DK, RET_DV = 4, 64, 128
M2_HEADS, M2_HEADDIM, M2_GROUPS, M2_STATE = 8, 64, 2, 128
RW_HEADS, RW_N = 8, 64
RW_W_LORA, RW_A_LORA, RW_G_LORA = 32, 32, 96
MIX_W = 512

_GDN0, _RET0, _M20, _RW0 = 0, 2056, 3592, 5136

P_PAD = 7168
_SRC_PIECES = (
    (_GDN0, 1536, 1536),
    (_RW0, 1536, 1536),
    (_M20 + 512, 1024, 1024),
    (_GDN0 + 1536, 512, 512),
    (_RET0, 512, 512),
    (_RET0 + 512, 512, 512),
    (_RET0 + 1024, 512, 512),
    (_M20, 512, 512),
    (_GDN0 + 2048, 8, 128),
    (_M20 + 1536, 8, 128),
    (_RW0 + 1536, 160, 256),
)

VMEM_LIMIT = 56 * 1024 * 1024

NN = (((1,), (0,)), ((), ()))
NT = (((1,), (1,)), ((), ()))
TN = (((0,), (0,)), ((), ()))


def _cparams(n_axes):
    return pltpu.CompilerParams(dimension_semantics=("arbitrary",) * n_axes,
                                vmem_limit_bytes=VMEM_LIMIT)


def _dg(a, b, dims=NN, prec=None):
    return lax.dot_general(a, b, dims, preferred_element_type=F32, precision=prec)


def _split(a):
    hi = a.astype(BF16)
    lo = (a - hi.astype(F32)).astype(BF16)
    return hi, lo


def _dot1(a, b, dims=NN):
    return _dg(a.astype(BF16), b.astype(BF16), dims)


def _split3(a):
    hi = a.astype(BF16)
    r1 = a - hi.astype(F32)
    mid = r1.astype(BF16)
    lo = (r1 - mid.astype(F32)).astype(BF16)
    return hi, mid, lo


def _dot_sel(a, b, dims=NN):
    if a.dtype == BF16:
        return sum(_dg(a, piece, dims) for piece in _split3(b))
    return sum(_dg(piece, b, dims) for piece in _split3(a))


def _sum_bcast(x, blk, pieces=1):
    outs = []
    for s in range(0, x.shape[1], 256):
        xs, bs = x[:, s:s + 256], blk[s:s + 256, s:s + 256]
        if pieces == 1:
            outs.append(_dg(xs.astype(BF16), bs))
        else:
            xh, xl = _split(xs)
            outs.append(_dg(xh, bs) + _dg(xl, bs))
    return jnp.concatenate(outs, axis=1)


def _rows_bcast(x, nch):
    return jnp.concatenate(
        [jnp.broadcast_to(x[(c + 1) * CHUNK - 1:(c + 1) * CHUNK, :], (CHUNK, x.shape[1])) for c in range(nch)],
        axis=0)


def _sigmoid(x):
    return 1.0 / (1.0 + jnp.exp(-x))


def _silu(x):
    return x * _sigmoid(x)


def _softplus(x):
    return jnp.maximum(x, 0.0) + jnp.log1p(jnp.exp(-jnp.abs(x)))


def _iota2(shape, axis):
    return lax.broadcasted_iota(jnp.int32, shape, axis)


def _chunk_masks():
    r = _iota2((CHUNK, CHUNK), 0)
    c = _iota2((CHUNK, CHUNK), 1)
    return r >= c, r > c, r == c


class _Pair:
    def __init__(self):
        lane = _iota2((CHUNK, 2 * CHUNK), 1)
        row = _iota2((CHUNK, 2 * CHUNK), 0)
        self.lo = lane < CHUNK
        col = jnp.where(self.lo, lane, lane - CHUNK)
        self.dist = (row - col).astype(F32)
        self.causal = row >= col
        self.strict = row > col
        self.eye = jnp.where(row == col, 1.0, 0.0).astype(F32)
        r2 = _iota2((2 * CHUNK, 2 * CHUNK), 0)
        c2 = _iota2((2 * CHUNK, 2 * CHUNK), 1)
        self.same_block = (r2 < CHUNK) == (c2 < CHUNK)

    def bd(self, x):
        z = jnp.zeros_like(x)
        return jnp.concatenate([jnp.where(self.lo, x, z), jnp.where(self.lo, z, x)], axis=0)

    def inverse(self, lows):
        ps = [self.eye - low for low in lows]
        curs = list(lows)
        n = 1
        while n < CHUNK:
            rhss = [self.bd(cur) for cur in curs]
            if n == 1:
                curs = [_dot1(cur, rhs) for cur, rhs in zip(curs, rhss)]
            elif 2 * n < CHUNK:
                outs = [_dot1(jnp.concatenate([cur, p], axis=0), rhs)
                        for cur, p, rhs in zip(curs, ps, rhss)]
                curs = [out[:CHUNK] for out in outs]
                ps = [p + out[CHUNK:] for p, out in zip(ps, outs)]
            else:
                ps = [p + _dot1(p, rhs) for p, rhs in zip(ps, rhss)]
            n *= 2
        return ps


def _norm_matmul_kernel(x_ref, nw_ref, w_ref, o_ref, a_ref):
    @pl.when(pl.program_id(1) == 0)
    def _():
        x = x_ref[...]
        y = x * lax.rsqrt(jnp.mean(x * x, axis=-1, keepdims=True) + EPS) * nw_ref[...]
        a_ref[...] = y.astype(BF16)

    o_ref[...] = jnp.dot(a_ref[...], w_ref[...], preferred_element_type=F32)


def _norm_matmul(x, nw, w, l, tm=1024, tn=1792):
    t, d = x.shape
    n = w.shape[2]
    tm = min(tm, t)
    return pl.pallas_call(
        _norm_matmul_kernel,
        out_shape=jax.ShapeDtypeStruct((t, n), F32),
        grid=(t // tm, n // tn),
        in_specs=[pl.BlockSpec((tm, d), lambda i, j: (i, 0)),
                  pl.BlockSpec((None, 1, d), lambda i, j: (l, 0, 0)),
                  pl.BlockSpec((None, d, tn), lambda i, j: (l, 0, j))],
        out_specs=pl.BlockSpec((tm, tn), lambda i, j: (i, j)),
        scratch_shapes=[pltpu.VMEM((tm, d), BF16)],
        compiler_params=_cparams(2),
    )(x, nw, w)


def _out_proj_kernel(h_ref, oa_ref, ob_ref, oc_ref, od_ref, w_ref, o_ref):
    acc = h_ref[...]
    for idx, part in enumerate((oa_ref, ob_ref, oc_ref, od_ref)):
        acc = acc + jnp.dot(part[...].astype(BF16), w_ref[idx * MIX_W:(idx + 1) * MIX_W, :],
                            preferred_element_type=F32)
    o_ref[...] = acc


def _out_proj(h, parts, w, l, tm=512):
    t, d = h.shape
    tm = min(tm, t)
    part_spec = pl.BlockSpec((tm, MIX_W), lambda i: (i, 0))
    return pl.pallas_call(
        _out_proj_kernel,
        out_shape=jax.ShapeDtypeStruct((t, d), F32),
        grid=(t // tm,),
        in_specs=[pl.BlockSpec((tm, d), lambda i: (i, 0)),
                  part_spec, part_spec, part_spec, part_spec,
                  pl.BlockSpec((None, 4 * MIX_W, d), lambda i: (l, 0, 0))],
        out_specs=pl.BlockSpec((tm, d), lambda i: (i, 0)),
        compiler_params=_cparams(1),
    )(h, *parts, w)


def _ffn_kernel(h_ref, nw_ref, wu_ref, wd_ref, fw_ref, o_ref, a_ref, *, final_norm):
    f = pl.program_id(1)

    @pl.when(f == 0)
    def _():
        x = h_ref[...]
        y = x * lax.rsqrt(jnp.mean(x * x, axis=-1, keepdims=True) + EPS) * nw_ref[...]
        a_ref[...] = y.astype(BF16)
        o_ref[...] = x

    u = jnp.dot(a_ref[...], wu_ref[...], preferred_element_type=F32)
    s = jnp.square(jnp.maximum(u, 0.0)).astype(BF16)
    o_ref[...] += jnp.dot(s, wd_ref[...], preferred_element_type=F32)

    if final_norm:
        @pl.when(f == pl.num_programs(1) - 1)
        def _():
            y = o_ref[...]
            o_ref[...] = y * lax.rsqrt(jnp.mean(y * y, axis=-1, keepdims=True) + EPS) * fw_ref[...]


def _ffn(h, nw, wu, wd, fw, l, final_norm, tm=1024, tf=512):
    t, d = h.shape
    ff = wu.shape[2]
    tm = min(tm, t)
    return pl.pallas_call(
        functools.partial(_ffn_kernel, final_norm=final_norm),
        out_shape=jax.ShapeDtypeStruct((t, d), F32),
        grid=(t // tm, ff // tf),
        in_specs=[pl.BlockSpec((tm, d), lambda i, f: (i, 0)),
                  pl.BlockSpec((None, 1, d), lambda i, f: (l, 0, 0)),
                  pl.BlockSpec((None, d, tf), lambda i, f: (l, 0, f)),
                  pl.BlockSpec((None, tf, d), lambda i, f: (l, f, 0)),
                  pl.BlockSpec((1, d), lambda i, f: (0, 0))],
        out_specs=pl.BlockSpec((tm, d), lambda i, f: (i, 0)),
        scratch_shapes=[pltpu.VMEM((tm, d), BF16)],
        compiler_params=_cparams(2),
    )(h, nw, wu, wd, fw)


def _stage_with_halo(ext_ref, pieces, rows):
    @pl.when(pl.program_id(0) == 0)
    def _():
        ext_ref[0:8, :] = jnp.zeros((8, ext_ref.shape[1]), F32)

    col = 0
    for ref in pieces:
        w = ref.shape[1]
        ext_ref[8:8 + rows, col:col + w] = ref[...]
        col += w


def _keep_halo(ext_ref, rows):
    ext_ref[0:8, :] = ext_ref[rows:rows + 8, :]


def _causal_conv(ext_ref, cw, rows):
    acc = ext_ref[pl.ds(8, rows), :] * cw[3:4, :]
    for i in range(CONV_W - 1):
        acc = acc + ext_ref[pl.ds(5 + i, rows), :] * cw[i:i + 1, :]
    return acc


def _const_spec(shape):
    return pl.BlockSpec(shape, lambda i: (0,) * len(shape))


PAR_ROWS, PAR_W = 32, 1792
(R_GDN_CONV, R_GDN_ALOG, R_GDN_DTB, R_GDN_NW, R_RET_NW, R_SSD_CONV, R_SSD_CB, R_SSD_ALOG, R_SSD_DTB,
 R_SSD_D, R_SSD_NW, R_RW_MU, R_RW_W0, R_RW_A0, R_RW_KK, R_RW_KA, R_RW_RK, R_RW_LNW, R_RW_LNB) = (
    0, 4, 5, 6, 7, 8, 12, 13, 14, 15, 16, 17, 18, 19, 20, 21, 22, 23, 24)


def _layer_spec(shape, l):
    return pl.BlockSpec((None,) + tuple(shape), lambda i: (l,) + (0,) * len(shape))


def _par(par_ref, row, width, nrows=1):
    return par_ref[row:row + nrows, 0:width]


def _block_diag_ones(n, block, lower=False, dtype=F32):
    r = jnp.arange(n)[:, None]
    c = jnp.arange(n)[None, :]
    m = (r // block) == (c // block)
    if lower:
        m = m & (r >= c)
    return m.astype(dtype)


def _gdn_kernel(qkv_ref, z_ref, gate_ref, par_ref, btri_ref, eye_ref, o_ref, ext_ref, s_ref, *, nch):
    rows = nch * CHUNK

    @pl.when(pl.program_id(0) == 0)
    def _():
        s_ref[...] = jnp.zeros_like(s_ref)

    _stage_with_halo(ext_ref, (qkv_ref,), rows)
    qkv = _silu(_causal_conv(ext_ref, _par(par_ref, R_GDN_CONV, 1536, CONV_W), rows))
    _keep_halo(ext_ref, rows)

    causal, strict, _ = _chunk_masks()
    pair = _Pair()

    gt = gate_ref[...]
    beta = _sigmoid(gt)
    log_a = -jnp.exp(_par(par_ref, R_GDN_ALOG, 128)) * _softplus(gt + _par(par_ref, R_GDN_DTB, 128))
    g_all = _dot_sel(btri_ref[...], log_a)
    g_t = _dot_sel(g_all, eye_ref[...], TN)
    z = z_ref[...]
    nw = _par(par_ref, R_GDN_NW, GDN_DV)

    qs, ks, vs = [], [], []
    for h in range(GDN_HEADS):
        qh = qkv[:, h * GDN_DK:(h + 1) * GDN_DK]
        kh = qkv[:, 512 + h * GDN_DK:512 + (h + 1) * GDN_DK]
        qs.append(qh * lax.rsqrt(jnp.sum(qh * qh, axis=-1, keepdims=True) + EPS) * (GDN_DK ** -0.5))
        ks.append(kh * lax.rsqrt(jnp.sum(kh * kh, axis=-1, keepdims=True) + EPS))
        vs.append(qkv[:, 1024 + h * GDN_DV:1024 + (h + 1) * GDN_DV])

    combos = [(c, h) for c in range(nch) for h in range(GDN_HEADS)]
    rsl = lambda c: slice(c * CHUNK, (c + 1) * CHUNK)
    bcol = {(c, h): beta[rsl(c), h:h + 1] for c, h in combos}
    gcol = {(c, h): g_all[rsl(c), 4 + h:5 + h] for c, h in combos}
    kq = {(c, h): _dot1(jnp.concatenate([ks[h][rsl(c)], qs[h][rsl(c)]], axis=0), ks[h][rsl(c)], NT)
          for c, h in combos}
    lows, qks = {}, {}
    for c, h in combos:
        grow = g_t[4 + h:5 + h, c * CHUNK:(c + 1) * CHUNK]
        gamma = jnp.exp(jnp.where(causal, gcol[c, h] - grow, -jnp.inf))
        lows[c, h] = jnp.where(strict, kq[c, h][:CHUNK] * gamma * bcol[c, h], 0.0)
        qks[c, h] = kq[c, h][CHUNK:] * gamma
    pairs = [(c, p) for c in range(nch) for p in range(GDN_HEADS // 2)]
    tps = pair.inverse([jnp.concatenate([lows[c, 2 * p], lows[c, 2 * p + 1]], axis=1) for c, p in pairs])
    tinv = {}
    for (c, p), tp in zip(pairs, tps):
        tinv[c, 2 * p] = tp[:, :CHUNK]
        tinv[c, 2 * p + 1] = tp[:, CHUNK:]
    egc = {ch: jnp.exp(gcol[ch]) for ch in combos}
    glast = {ch: gcol[ch][CHUNK - 1:CHUNK, :] for ch in combos}
    wu = {(c, h): _dot1(tinv[c, h], jnp.concatenate(
        [ks[h][rsl(c)] * (bcol[c, h] * egc[c, h]), vs[h][rsl(c)] * bcol[c, h]], axis=1)) for c, h in combos}
    qwu = {ch: _dot1(qks[ch], wu[ch]) for ch in combos}
    kwu = {(c, h): _dot1(ks[h][rsl(c)] * jnp.exp(glast[c, h] - gcol[c, h]), wu[c, h], TN) for c, h in combos}
    for c in range(nch):
        rs = rsl(c)
        s0 = [s_ref[h] if c == 0 else s_new[h] for h in range(GDN_HEADS)]
        s_new = []
        for h in range(GDN_HEADS):
            sl = slice(h * GDN_DV, (h + 1) * GDN_DV)
            q_eff = qs[h][rs] * egc[c, h] - qwu[c, h][:, :GDN_DK]
            o = _dot1(q_eff, s0[h]) + qwu[c, h][:, GDN_DK:]
            s_new.append(s0[h] * jnp.exp(glast[c, h]) - _dot1(kwu[c, h][:, :GDN_DK], s0[h])
                         + kwu[c, h][:, GDN_DK:])
            o = o * lax.rsqrt(jnp.mean(o * o, axis=-1, keepdims=True) + EPS) * nw
            o_ref[rs, sl] = o * _silu(z[rs, sl])
    for h in range(GDN_HEADS):
        s_ref[h] = s_new[h]


def _gdn(proj, par, l, nch):
    t = proj.shape[0]
    rows = nch * CHUNK
    return pl.pallas_call(
        functools.partial(_gdn_kernel, nch=nch),
        out_shape=jax.ShapeDtypeStruct((t, MIX_W), F32),
        grid=(t // rows,),
        in_specs=[pl.BlockSpec((rows, 1536), lambda i: (i, 0)),
                  pl.BlockSpec((rows, 512), lambda i: (i, 8)),
                  pl.BlockSpec((rows, 128), lambda i: (i, 52)),
                  _layer_spec((PAR_ROWS, PAR_W), l),
                  _const_spec((rows, rows)), _const_spec((rows, rows))],
        out_specs=pl.BlockSpec((rows, MIX_W), lambda i: (i, 0)),
        scratch_shapes=[pltpu.VMEM((rows + 8, 1536), F32),
                        pltpu.VMEM((GDN_HEADS, GDN_DK, GDN_DV), F32)],
        compiler_params=_cparams(1),
    )(proj, proj, proj, par, _block_diag_ones(rows, CHUNK, lower=True, dtype=BF16),
      jnp.eye(rows, dtype=BF16))


def _ret_kernel(qk_ref, v_ref, g_ref, cos_ref, sin_ref, par_ref, o_ref, r_ref, *, nch):
    @pl.when(pl.program_id(0) == 0)
    def _():
        r_ref[...] = jnp.zeros_like(r_ref)

    rows = nch * CHUNK
    qk = qk_ref[...]
    cos = cos_ref[...]
    sin = sin_ref[...]
    even = (_iota2((rows, 256), 1) % 2) == 0

    def rot(x):
        partner = jnp.where(even, pltpu.roll(x, 255, 1), pltpu.roll(x, 1, 1))
        return x * cos + partner * sin

    q = rot(qk[:, 0:256])
    k = rot(qk[:, 256:512]) * (RET_DK ** -0.5)
    v = v_ref[...]
    gate = g_ref[...]
    nw = _par(par_ref, R_RET_NW, MIX_W)

    pair = _Pair()
    pos = _iota2((CHUNK, 2 * CHUNK), 0).astype(F32)
    zeros_v = jnp.zeros((CHUNK, RET_DV), F32)
    r2 = _iota2((2 * RET_DK, 2 * RET_DV), 0)
    c2 = _iota2((2 * RET_DK, 2 * RET_DV), 1)
    state_block = (r2 < RET_DK) == (c2 < RET_DV)

    for p in range(RET_HEADS // 2):
        lg0 = math.log1p(-(2.0 ** (-5.0 - 2 * p)))
        lg1 = math.log1p(-(2.0 ** (-5.0 - (2 * p + 1))))
        lg = jnp.where(pair.lo, lg0, lg1)
        dmask = jnp.exp(jnp.where(pair.causal, pair.dist * lg, -jnp.inf))
        k_dec = jnp.exp((CHUNK - 1.0 - pos) * lg)
        q_dec = jnp.exp((pos + 1.0) * lg)
        lg_v = jnp.where(_iota2((1, 2 * RET_DV), 1) < RET_DV, lg0, lg1)
        chunk_decay = jnp.exp(CHUNK * lg_v)
        ls = slice(p * 2 * RET_DK, (p + 1) * 2 * RET_DK)
        vsl = slice(p * 2 * RET_DV, (p + 1) * 2 * RET_DV)
        rsl = lambda c: slice(c * CHUNK, (c + 1) * CHUNK)
        sc = [_dot1(q[rsl(c), ls], pair.bd(k[rsl(c), ls]), NT) * dmask for c in range(nch)]
        upd = [_dot1(k[rsl(c), ls] * k_dec, v[rsl(c), vsl], TN) for c in range(nch)]
        r_start = []
        r_cur = r_ref[p]
        for c in range(nch):
            r_start.append(r_cur)
            r_cur = r_cur * chunk_decay + jnp.where(state_block, upd[c], 0.0)
        r_ref[p] = r_cur
        for c in range(nch):
            rs = rsl(c)
            vp = v[rs, vsl]
            v_bd = jnp.concatenate(
                [jnp.concatenate([vp[:, :RET_DV], zeros_v], axis=1),
                 jnp.concatenate([zeros_v, vp[:, RET_DV:]], axis=1)], axis=0)
            o = _dot1(sc[c], v_bd) + _dot1(q[rs, ls] * q_dec, r_start[c])
            for hh in range(2):
                sl = slice(vsl.start + hh * RET_DV, vsl.start + (hh + 1) * RET_DV)
                oh = o[:, hh * RET_DV:(hh + 1) * RET_DV]
                mu = jnp.mean(oh, axis=-1, keepdims=True)
                oc = oh - mu
                oh = oc * lax.rsqrt(jnp.mean(oc * oc, axis=-1, keepdims=True) + EPS) * nw[:, sl]
                o_ref[rs, sl] = oh * _silu(gate[rs, sl])


def _retention(proj, cos, sin, par, l, nch):
    t = proj.shape[0]
    rows = nch * CHUNK
    return pl.pallas_call(
        functools.partial(_ret_kernel, nch=nch),
        out_shape=jax.ShapeDtypeStruct((t, MIX_W), F32),
        grid=(t // rows,),
        in_specs=[pl.BlockSpec((rows, 512), lambda i: (i, 9)),
                  pl.BlockSpec((rows, 512), lambda i: (i, 10)),
                  pl.BlockSpec((rows, 512), lambda i: (i, 11)),
                  pl.BlockSpec((rows, 256), lambda i: (i, 0)),
                  pl.BlockSpec((rows, 256), lambda i: (i, 0)),
                  _layer_spec((PAR_ROWS, PAR_W), l)],
        out_specs=pl.BlockSpec((rows, MIX_W), lambda i: (i, 0)),
        scratch_shapes=[pltpu.VMEM((RET_HEADS // 2, 2 * RET_DK, 2 * RET_DV), F32)],
        compiler_params=_cparams(1),
    )(proj, proj, proj, cos, sin, par)


def _ssd_kernel(z_ref, xbc_ref, dt_ref, par_ref, btri_ref, eye_ref, expand_ref, o_ref, ext_ref, h_ref,
                *, nch):
    rows = nch * CHUNK

    @pl.when(pl.program_id(0) == 0)
    def _():
        h_ref[...] = jnp.zeros_like(h_ref)

    _stage_with_halo(ext_ref, (xbc_ref,), rows)
    xbc = _silu(_causal_conv(ext_ref, _par(par_ref, R_SSD_CONV, 1024, CONV_W), rows)
                + _par(par_ref, R_SSD_CB, 1024))
    _keep_halo(ext_ref, rows)
    x = xbc[:, 0:512]

    pair = _Pair()
    dt = _softplus(dt_ref[...] + _par(par_ref, R_SSD_DTB, 128))
    g_all = _dot_sel(btri_ref[...], dt * (-jnp.exp(_par(par_ref, R_SSD_ALOG, 128))))
    g_t = _dot_sel(g_all, eye_ref[...], TN)
    wide = _dot_sel(jnp.concatenate([g_all, dt], axis=0), expand_ref[...])
    g_w, dt_w = wide[:rows], wide[rows:]
    gtot_w = _rows_bcast(g_w, nch)
    xdt = x * dt_w
    xtail = xdt * jnp.exp(gtot_w - g_w)
    eg_w = jnp.exp(g_w)
    etot_w = jnp.exp(gtot_w)
    z = z_ref[...]
    d_w = _par(par_ref, R_SSD_D, MIX_W)
    nw = _par(par_ref, R_SSD_NW, MIX_W)
    gw = M2_HEADS // M2_GROUPS * M2_HEADDIM

    combos = [(c, gi) for c in range(nch) for gi in range(M2_GROUPS)]
    rsl = lambda c: slice(c * CHUNK, (c + 1) * CHUNK)
    bg = {(c, gi): xbc[rsl(c), 512 + gi * M2_STATE:512 + (gi + 1) * M2_STATE] for c, gi in combos}
    cg = {(c, gi): xbc[rsl(c), 768 + gi * M2_STATE:768 + (gi + 1) * M2_STATE] for c, gi in combos}
    cb2 = {k: _dot1(cg[k], jnp.concatenate([bg[k], bg[k]], axis=0), NT) for k in combos}
    upd = {(c, gi): _dot1(bg[c, gi], xtail[rsl(c), gi * gw:(gi + 1) * gw], TN) for c, gi in combos}
    y_in = {}
    for c, gi in combos:
        for pp in range(2):
            ha = gi * 4 + 2 * pp
            gcol = jnp.where(pair.lo, g_all[rsl(c), ha:ha + 1], g_all[rsl(c), ha + 1:ha + 2])
            grow = jnp.concatenate([g_t[ha:ha + 1, rsl(c)], g_t[ha + 1:ha + 2, rsl(c)]], axis=1)
            decay = jnp.exp(jnp.where(pair.causal, gcol - grow, -jnp.inf))
            y_in[c, gi, pp] = _dot1(cb2[c, gi] * decay,
                                    pair.bd(xdt[rsl(c), ha * M2_HEADDIM:(ha + 2) * M2_HEADDIM]))
    h_start = {}
    for gi in range(M2_GROUPS):
        h = h_ref[gi]
        for c in range(nch):
            h_start[c, gi] = h
            h = h * etot_w[c * CHUNK:c * CHUNK + 1, gi * gw:(gi + 1) * gw] + upd[c, gi]
        h_ref[gi] = h
    for c, gi in combos:
        rs, gs = rsl(c), slice(gi * gw, (gi + 1) * gw)
        y = jnp.concatenate([y_in[c, gi, 0], y_in[c, gi, 1]], axis=1)
        y = y + _dot1(cg[c, gi], h_start[c, gi]) * eg_w[rs, gs]
        y = (y + d_w[:, gs] * x[rs, gs]) * _silu(z[rs, gs])
        y = y * lax.rsqrt(jnp.mean(y * y, axis=-1, keepdims=True) + EPS) * nw[:, gs]
        o_ref[rs, gs] = y


def _head_expand_matrix(n_heads, width):
    r = jnp.arange(128)[:, None]
    c = jnp.arange(n_heads * width)[None, :]
    return (r == c // width).astype(BF16)


def _ssd(proj, par, l, nch):
    t = proj.shape[0]
    rows = nch * CHUNK
    return pl.pallas_call(
        functools.partial(_ssd_kernel, nch=nch),
        out_shape=jax.ShapeDtypeStruct((t, MIX_W), F32),
        grid=(t // rows,),
        in_specs=[pl.BlockSpec((rows, 512), lambda i: (i, 12)),
                  pl.BlockSpec((rows, 1024), lambda i: (i, 3)),
                  pl.BlockSpec((rows, 128), lambda i: (i, 53)),
                  _layer_spec((PAR_ROWS, PAR_W), l),
                  _const_spec((rows, rows)), _const_spec((rows, rows)), _const_spec((128, 512))],
        out_specs=pl.BlockSpec((rows, MIX_W), lambda i: (i, 0)),
        scratch_shapes=[pltpu.VMEM((rows + 8, 1024), F32),
                        pltpu.VMEM((M2_GROUPS, M2_STATE, 256), F32)],
        compiler_params=_cparams(1),
    )(proj, proj, proj, par,
      _block_diag_ones(rows, CHUNK, lower=True, dtype=BF16), jnp.eye(rows, dtype=BF16),
      _head_expand_matrix(M2_HEADS, M2_HEADDIM))


def _rwkv_kernel(rkv_ref, lora_ref, par_ref, lmat_ref, btri_ref, hblk_ref, o_ref, ext_ref, s_ref, *, nch):
    rows = nch * CHUNK

    @pl.when(pl.program_id(0) == 0)
    def _():
        s_ref[...] = jnp.zeros_like(s_ref)

    _stage_with_halo(ext_ref, (rkv_ref, lora_ref), rows)
    cur = ext_ref[pl.ds(8, rows), :]
    prev = ext_ref[pl.ds(7, rows), :]
    _keep_halo(ext_ref, rows)
    mixed = cur + (prev - cur) * _par(par_ref, R_RW_MU, PAR_W)
    r = mixed[:, 0:512]
    k = mixed[:, 512:1024]
    v = mixed[:, 1024:1536]
    lora = mixed[:, 1536:1792]

    w_raw = -_softplus(-(_par(par_ref, R_RW_W0, MIX_W) + _dot1(jnp.tanh(lora), lmat_ref[0:256, :]))) - 0.5
    log_d = -jnp.exp(w_raw)
    a = _sigmoid(_par(par_ref, R_RW_A0, MIX_W) + _dot1(lora, lmat_ref[256:512, :]))
    gate = _dot1(_sigmoid(lora), lmat_ref[512:768, :])
    hblk = hblk_ref[...]

    g_in = _dot_sel(btri_ref[...], log_d)
    g_tot = _rows_bcast(g_in, nch)
    e_in = jnp.exp(g_in)
    e_neg = jnp.exp(-g_in)
    e_ex = jnp.exp(g_in - log_d)
    e_tail = jnp.exp(g_tot - g_in)
    e_end = jnp.exp(g_tot)

    kk = k * _par(par_ref, R_RW_KK, MIX_W)
    kk = kk * lax.rsqrt(_sum_bcast(kk * kk, hblk) + EPS)
    k_mod = k * (1.0 + (a - 1.0) * _par(par_ref, R_RW_KA, MIX_W))
    a_vec = -(a * kk)
    r_t = r * e_in
    b_t = kk * e_ex
    k_t = k_mod * e_neg
    a_t = a_vec * e_neg
    k_c = k_mod * e_tail
    a_c = a_vec * e_tail

    pair = _Pair()
    npair = RW_HEADS // 2
    combos = [(c, j) for c in range(nch) for j in range(npair)]
    cut = lambda arr, cj: arr[cj[0] * CHUNK:(cj[0] + 1) * CHUNK, cj[1] * 2 * RW_N:(cj[1] + 1) * 2 * RW_N]
    lhs = {cj: jnp.concatenate([cut(b_t, cj), cut(r_t, cj)], axis=0) for cj in combos}
    x1 = {cj: _dot1(lhs[cj], pair.bd(cut(k_t, cj)), NT) for cj in combos}
    x2 = {cj: _dot1(lhs[cj], pair.bd(cut(a_t, cj)), NT) for cj in combos}
    a_ra = {cj: jnp.where(pair.causal, x2[cj][CHUNK:], 0.0) for cj in combos}
    tinv = dict(zip(combos, pair.inverse([-jnp.where(pair.strict, x2[cj][:CHUNK], 0.0) for cj in combos])))
    av = {cj: _dot1(jnp.concatenate([jnp.where(pair.strict, x1[cj][:CHUNK], 0.0),
                                     jnp.where(pair.causal, x1[cj][CHUNK:], 0.0)], axis=0),
                    pair.bd(cut(v, cj))) for cj in combos}
    tz = {cj: _dot1(tinv[cj], jnp.concatenate([pair.bd(cut(b_t, cj)), pair.bd(av[cj][:CHUNK])], axis=1))
          for cj in combos}
    az = {cj: _dot1(a_ra[cj], jnp.concatenate([pair.bd(tz[cj][:, :2 * RW_N]), pair.bd(tz[cj][:, 2 * RW_N:])],
                                              axis=1)) for cj in combos}
    r_eff = {cj: cut(r_t, cj) + az[cj][:, :2 * RW_N] for cj in combos}
    y0 = {cj: av[cj][CHUNK:] + az[cj][:, 2 * RW_N:] for cj in combos}
    p_low = {cj: jnp.where(pair.same_block, _dot1(tz[cj][:, :2 * RW_N], cut(a_c, cj), TN), 0.0) for cj in combos}
    q_mat = {cj: jnp.where(pair.same_block,
                           _dot1(jnp.concatenate([cut(v, cj), tz[cj][:, 2 * RW_N:]], axis=0),
                                 jnp.concatenate([cut(k_c, cj), cut(a_c, cj)], axis=0), TN), 0.0)
             for cj in combos}
    y_rows = []
    for c in range(nch):
        s0 = [s_ref[j] if c == 0 else s_new[j] for j in range(npair)]
        y_rows.append(jnp.concatenate(
            [_dot1(r_eff[c, j], s0[j], NT) + y0[c, j] for j in range(npair)], axis=1))
        s_new = [s0[j] * e_end[c * CHUNK:c * CHUNK + 1, j * 2 * RW_N:(j + 1) * 2 * RW_N]
                 + _dot1(s0[j], p_low[c, j]) + q_mat[c, j] for j in range(npair)]
    for j in range(npair):
        s_ref[j] = s_new[j]
    y = jnp.concatenate(y_rows, axis=0)

    inv_n = 1.0 / RW_N
    mu = _sum_bcast(y, hblk, pieces=2) * inv_n
    yc = y - mu
    var = _sum_bcast(yc * yc, hblk) * inv_n
    y = yc * lax.rsqrt(var + RWKV_LN_EPS) * _par(par_ref, R_RW_LNW, MIX_W) + _par(par_ref, R_RW_LNB, MIX_W)
    y = y + _sum_bcast(r * k_mod * _par(par_ref, R_RW_RK, MIX_W), hblk) * v
    o_ref[...] = y * gate


def _rwkv(proj, par, lmat, l, nch):
    t = proj.shape[0]
    rows = nch * CHUNK
    return pl.pallas_call(
        functools.partial(_rwkv_kernel, nch=nch),
        out_shape=jax.ShapeDtypeStruct((t, MIX_W), F32),
        grid=(t // rows,),
        in_specs=[pl.BlockSpec((rows, 1536), lambda i: (i, 1)),
                  pl.BlockSpec((rows, 256), lambda i: (i, 27)),
                  _layer_spec((PAR_ROWS, PAR_W), l), _layer_spec((768, MIX_W), l),
                  _const_spec((rows, rows)), _const_spec((512, 512))],
        out_specs=pl.BlockSpec((rows, MIX_W), lambda i: (i, 0)),
        scratch_shapes=[pltpu.VMEM((rows + 8, 1792), F32),
                        pltpu.VMEM((RW_HEADS // 2, 2 * RW_N, 2 * RW_N), F32)],
        compiler_params=_cparams(1),
    )(proj, proj, par, lmat,
      _block_diag_ones(rows, CHUNK, lower=True, dtype=BF16), _block_diag_ones(MIX_W, RW_N, dtype=BF16))


def _layout_w_in_kernel(w_ref, o_ref):
    x = w_ref[...]
    col = 0
    for s, n, p in _SRC_PIECES:
        o_ref[:, col:col + n] = x[:, s:s + n].astype(BF16)
        if p > n:
            o_ref[:, col + n:col + p] = jnp.zeros((x.shape[0], p - n), BF16)
        col += p


def _layout_w_in(w, tr=256):
    depth, d, n = w.shape
    out = pl.pallas_call(
        _layout_w_in_kernel,
        out_shape=jax.ShapeDtypeStruct((depth * d, P_PAD), BF16),
        grid=(depth * d // tr,),
        in_specs=[pl.BlockSpec((tr, n), lambda i: (i, 0))],
        out_specs=pl.BlockSpec((tr, P_PAD), lambda i: (i, 0)),
        compiler_params=_cparams(1),
    )(w.reshape(depth * d, n))
    return out.reshape(depth, d, P_PAD)


def _param_slab(p):
    depth = p['w_in'].shape[0]

    def rows(a, lane=0):
        a = a.astype(F32).reshape(depth, -1, a.shape[-1])
        return jnp.pad(a, ((0, 0), (0, 0), (lane, PAR_W - lane - a.shape[-1])))

    pieces = [rows(p['gdn_conv_w']), rows(p['gdn_a_log'], 4), rows(p['gdn_dt_bias'], 4),
              rows(p['gdn_norm_w']), rows(p['ret_norm_w']),
              rows(p['m2_conv_w']), rows(p['m2_conv_b']), rows(p['m2_a_log']), rows(p['m2_dt_bias']),
              rows(jnp.repeat(p['m2_d'], M2_HEADDIM, axis=1)), rows(p['m2_norm_w']),
              rows(p['rw_mu']), rows(p['rw_w0']), rows(p['rw_a0']), rows(p['rw_k_k']), rows(p['rw_k_a']),
              rows(p['rw_r_k'].reshape(depth, -1)), rows(p['rw_ln_w']), rows(p['rw_ln_b'])]
    slab = jnp.concatenate(pieces, axis=1)
    return jnp.pad(slab, ((0, 0), (0, PAR_ROWS - slab.shape[1]), (0, 0)))


def _lora_mats(p):
    def at(m, offset):
        return jnp.pad(m, ((0, 0), (offset, 256 - offset - m.shape[1]), (0, 0)))
    return jnp.concatenate([at(p['rw_w_up'], 0), at(p['rw_a_up'], RW_W_LORA),
                            at(p['rw_g_up'], RW_W_LORA + RW_A_LORA)], axis=1).astype(BF16)


def _rotary_tables(t):
    theta = 1.0 / (ROPE_BASE ** jnp.linspace(0.0, 1.0, RET_DK // 2, dtype=F32))
    ang = jnp.arange(t, dtype=F32)[:, None] * theta
    cos = jnp.repeat(jnp.cos(ang), 2, axis=1)
    sin = jnp.stack([-jnp.sin(ang), jnp.sin(ang)], axis=-1).reshape(t, RET_DK)
    return jnp.tile(cos, (1, RET_HEADS)), jnp.tile(sin, (1, RET_HEADS))


NCH_GDN, NCH_RET, NCH_SSD, NCH_RWKV = 4, 4, 4, 4


def _prepare(p, t):
    cos, sin = _rotary_tables(t)
    return dict(norm1=p['norm1_w'].astype(F32)[:, None, :], w_in=_layout_w_in(p['w_in']),
                par=_param_slab(p), lmat=_lora_mats(p), w_out=p['w_out'].astype(BF16), cos=cos, sin=sin)


def _token_mix(h, l, q):
    t = h.shape[0]
    nch = lambda n: min(n, t // CHUNK)
    proj = _norm_matmul(h, q['norm1'], q['w_in'], l)
    o_a = _gdn(proj, q['par'], l, nch(NCH_GDN))
    o_b = _retention(proj, q['cos'], q['sin'], q['par'], l, nch(NCH_RET))
    o_c = _ssd(proj, q['par'], l, nch(NCH_SSD))
    o_d = _rwkv(proj, q['par'], q['lmat'], l, nch(NCH_RWKV))
    return _out_proj(h, (o_a, o_b, o_c, o_d), q['w_out'], l)


def kernel(x, norm1_w, w_in, gdn_conv_w, gdn_a_log, gdn_dt_bias, gdn_norm_w, ret_norm_w, m2_conv_w, m2_conv_b, m2_a_log, m2_dt_bias, m2_d, m2_norm_w, rw_mu, rw_w0, rw_w_up, rw_a0, rw_a_up, rw_g_up, rw_k_k, rw_k_a, rw_r_k, rw_ln_w, rw_ln_b, w_out, norm2_w, w_ffn_up, w_ffn_down, final_norm_w):
    p = dict(norm1_w=norm1_w, w_in=w_in, gdn_conv_w=gdn_conv_w, gdn_a_log=gdn_a_log,
             gdn_dt_bias=gdn_dt_bias, gdn_norm_w=gdn_norm_w, ret_norm_w=ret_norm_w,
             m2_conv_w=m2_conv_w, m2_conv_b=m2_conv_b, m2_a_log=m2_a_log, m2_dt_bias=m2_dt_bias,
             m2_d=m2_d, m2_norm_w=m2_norm_w, rw_mu=rw_mu, rw_w0=rw_w0, rw_w_up=rw_w_up,
             rw_a0=rw_a0, rw_a_up=rw_a_up, rw_g_up=rw_g_up, rw_k_k=rw_k_k, rw_k_a=rw_k_a,
             rw_r_k=rw_r_k, rw_ln_w=rw_ln_w, rw_ln_b=rw_ln_b, w_out=w_out)
    bsz, t, d = x.shape
    depth = w_in.shape[0]
    q = _prepare(p, t)
    norm2 = norm2_w.astype(F32)[:, None, :]
    w_up = w_ffn_up.astype(BF16)
    w_down = w_ffn_down.astype(BF16)
    final_w = final_norm_w.astype(F32).reshape(1, d)
    outs = []
    for b in range(bsz):
        h = x[b]
        for l in range(depth):
            h = _token_mix(h, l, q)
            h = _ffn(h, norm2, w_up, w_down, final_w, l, final_norm=(l == depth - 1))
        outs.append(h)
    return outs[0].reshape(1, t, d) if bsz == 1 else jnp.stack(outs, axis=0)
```

```python
import functools
import math

import jax
import jax.numpy as jnp
from jax import lax
from jax.experimental import pallas as pl
from jax.experimental.pallas import tpu as pltpu

F32 = jnp.float32
BF16 = jnp.bfloat16

D_MODEL = 2048
D_FF = 4 * D_MODEL
CONV_W = 4
CHUNK = 64
EPS = 1e-6
ROPE_BASE = 10000.0
RWKV_LN_EPS = 64e-5

GDN_HEADS, GDN_DK, GDN_DV = 4, 128, 128
RET_HEADS, RET_DK, RET_DV = 4, 64, 128
M2_HEADS, M2_HEADDIM, M2_GROUPS, M2_STATE = 8, 64, 2, 128
RW_HEADS, RW_N = 8, 64
RW_W_LORA, RW_A_LORA, RW_G_LORA = 32, 32, 96
MIX_W = 512

_GDN0, _RET0, _M20, _RW0 = 0, 2056, 3592, 5136

P_PAD = 7168
_SRC_PIECES = (
    (_GDN0, 1536, 1536),
    (_RW0, 1536, 1536),
    (_M20 + 512, 1024, 1024),
    (_GDN0 + 1536, 512, 512),
    (_RET0, 512, 512),
    (_RET0 + 512, 512, 512),
    (_RET0 + 1024, 512, 512),
    (_M20, 512, 512),
    (_GDN0 + 2048, 8, 128),
    (_M20 + 1536, 8, 128),
    (_RW0 + 1536, 160, 256),
)

VMEM_LIMIT = 56 * 1024 * 1024

NN = (((1,), (0,)), ((), ()))
NT = (((1,), (1,)), ((), ()))
TN = (((0,), (0,)), ((), ()))


def _cparams(n_axes):
    return pltpu.CompilerParams(dimension_semantics=("arbitrary",) * n_axes,
                                vmem_limit_bytes=VMEM_LIMIT)


def _dg(a, b, dims=NN, prec=None):
    return lax.dot_general(a, b, dims, preferred_element_type=F32, precision=prec)


def _split(a):
    hi = a.astype(BF16)
    lo = (a - hi.astype(F32)).astype(BF16)
    return hi, lo


def _dot1(a, b, dims=NN):
    return _dg(a.astype(BF16), b.astype(BF16), dims)


def _split3(a):
    hi = a.astype(BF16)
    r1 = a - hi.astype(F32)
    mid = r1.astype(BF16)
    lo = (r1 - mid.astype(F32)).astype(BF16)
    return hi, mid, lo


def _dot_sel(a, b, dims=NN):
    if a.dtype == BF16:
        return sum(_dg(a, piece, dims) for piece in _split3(b))
    return sum(_dg(piece, b, dims) for piece in _split3(a))


def _sum_bcast(x, blk, pieces=1):
    outs = []
    for s in range(0, x.shape[1], 256):
        xs, bs = x[:, s:s + 256], blk[s:s + 256, s:s + 256]
        if pieces == 1:
            outs.append(_dg(xs.astype(BF16), bs))
        else:
            xh, xl = _split(xs)
            outs.append(_dg(xh, bs) + _dg(xl, bs))
    return jnp.concatenate(outs, axis=1)


def _rows_bcast(x, nch):
    return jnp.concatenate(
        [jnp.broadcast_to(x[(c + 1) * CHUNK - 1:(c + 1) * CHUNK, :], (CHUNK, x.shape[1])) for c in range(nch)],
        axis=0)


def _sigmoid(x):
    return 1.0 / (1.0 + jnp.exp(-x))


def _silu(x):
    return x * _sigmoid(x)


def _softplus(x):
    return jnp.maximum(x, 0.0) + jnp.log1p(jnp.exp(-jnp.abs(x)))


def _iota2(shape, axis):
    return lax.broadcasted_iota(jnp.int32, shape, axis)


def _chunk_masks():
    r = _iota2((CHUNK, CHUNK), 0)
    c = _iota2((CHUNK, CHUNK), 1)
    return r >= c, r > c, r == c


class _Pair:
    def __init__(self):
        lane = _iota2((CHUNK, 2 * CHUNK), 1)
        row = _iota2((CHUNK, 2 * CHUNK), 0)
        self.lo = lane < CHUNK
        col = jnp.where(self.lo, lane, lane - CHUNK)
        self.dist = (row - col).astype(F32)
        self.causal = row >= col
        self.strict = row > col
        self.eye = jnp.where(row == col, 1.0, 0.0).astype(F32)
        r2 = _iota2((2 * CHUNK, 2 * CHUNK), 0)
        c2 = _iota2((2 * CHUNK, 2 * CHUNK), 1)
        self.same_block = (r2 < CHUNK) == (c2 < CHUNK)

    def bd(self, x):
        z = jnp.zeros_like(x)
        return jnp.concatenate([jnp.where(self.lo, x, z), jnp.where(self.lo, z, x)], axis=0)

    def inverse(self, lows):
        ps = [self.eye - low for low in lows]
        curs = list(lows)
        n = 1
        while n < CHUNK:
            rhss = [self.bd(cur) for cur in curs]
            if n == 1:
                curs = [_dot1(cur, rhs) for cur, rhs in zip(curs, rhss)]
            elif 2 * n < CHUNK:
                outs = [_dot1(jnp.concatenate([cur, p], axis=0), rhs)
                        for cur, p, rhs in zip(curs, ps, rhss)]
                curs = [out[:CHUNK] for out in outs]
                ps = [p + out[CHUNK:] for p, out in zip(ps, outs)]
            else:
                ps = [p + _dot1(p, rhs) for p, rhs in zip(ps, rhss)]
            n *= 2
        return ps


def _norm_matmul_kernel(x_ref, nw_ref, w_ref, o_ref, a_ref):
    @pl.when(pl.program_id(1) == 0)
    def _():
        x = x_ref[...]
        y = x * lax.rsqrt(jnp.mean(x * x, axis=-1, keepdims=True) + EPS) * nw_ref[...]
        a_ref[...] = y.astype(BF16)

    o_ref[...] = jnp.dot(a_ref[...], w_ref[...], preferred_element_type=F32)


def _norm_matmul(x, nw, w, l, tm=1024, tn=1792):
    t, d = x.shape
    n = w.shape[2]
    tm = min(tm, t)
    return pl.pallas_call(
        _norm_matmul_kernel,
        out_shape=jax.ShapeDtypeStruct((t, n), F32),
        grid=(t // tm, n // tn),
        in_specs=[pl.BlockSpec((tm, d), lambda i, j: (i, 0)),
                  pl.BlockSpec((None, 1, d), lambda i, j: (l, 0, 0)),
                  pl.BlockSpec((None, d, tn), lambda i, j: (l, 0, j))],
        out_specs=pl.BlockSpec((tm, tn), lambda i, j: (i, j)),
        scratch_shapes=[pltpu.VMEM((tm, d), BF16)],
        compiler_params=_cparams(2),
    )(x, nw, w)


def _out_proj_kernel(h_ref, oa_ref, ob_ref, oc_ref, od_ref, w_ref, o_ref):
    acc = h_ref[...]
    for idx, part in enumerate((oa_ref, ob_ref, oc_ref, od_ref)):
        acc = acc + jnp.dot(part[...].astype(BF16), w_ref[idx * MIX_W:(idx + 1) * MIX_W, :],
                            preferred_element_type=F32)
    o_ref[...] = acc


def _out_proj(h, parts, w, l, tm=512):
    t, d = h.shape
    tm = min(tm, t)
    part_spec = pl.BlockSpec((tm, MIX_W), lambda i: (i, 0))
    return pl.pallas_call(
        _out_proj_kernel,
        out_shape=jax.ShapeDtypeStruct((t, d), F32),
        grid=(t // tm,),
        in_specs=[pl.BlockSpec((tm, d), lambda i: (i, 0)),
                  part_spec, part_spec, part_spec, part_spec,
                  pl.BlockSpec((None, 4 * MIX_W, d), lambda i: (l, 0, 0))],
        out_specs=pl.BlockSpec((tm, d), lambda i: (i, 0)),
        compiler_params=_cparams(1),
    )(h, *parts, w)


def _ffn_kernel(h_ref, nw_ref, wu_ref, wd_ref, fw_ref, o_ref, a_ref, *, final_norm):
    f = pl.program_id(1)

    @pl.when(f == 0)
    def _():
        x = h_ref[...]
        y = x * lax.rsqrt(jnp.mean(x * x, axis=-1, keepdims=True) + EPS) * nw_ref[...]
        a_ref[...] = y.astype(BF16)
        o_ref[...] = x

    u = jnp.dot(a_ref[...], wu_ref[...], preferred_element_type=F32)
    s = jnp.square(jnp.maximum(u, 0.0)).astype(BF16)
    o_ref[...] += jnp.dot(s, wd_ref[...], preferred_element_type=F32)

    if final_norm:
        @pl.when(f == pl.num_programs(1) - 1)
        def _():
            y = o_ref[...]
            o_ref[...] = y * lax.rsqrt(jnp.mean(y * y, axis=-1, keepdims=True) + EPS) * fw_ref[...]


FFN_TF = 512


def _ffn(h, nw, wu, wd, fw, l, final_norm, tm=1024):
    t, d = h.shape
    ff, tf = wd.shape[1], FFN_TF
    tm = min(tm, t)
    return pl.pallas_call(
        functools.partial(_ffn_kernel, final_norm=final_norm),
        out_shape=jax.ShapeDtypeStruct((t, d), F32),
        grid=(t // tm, ff // tf),
        in_specs=[pl.BlockSpec((tm, d), lambda i, f: (i, 0)),
                  pl.BlockSpec((None, 1, d), lambda i, f: (l, 0, 0)),
                  pl.BlockSpec((None, None, d, tf), lambda i, f: (l, f, 0, 0)),
                  pl.BlockSpec((None, tf, d), lambda i, f: (l, f, 0)),
                  pl.BlockSpec((1, d), lambda i, f: (0, 0))],
        out_specs=pl.BlockSpec((tm, d), lambda i, f: (i, 0)),
        scratch_shapes=[pltpu.VMEM((tm, d), BF16)],
        compiler_params=_cparams(2),
    )(h, nw, wu, wd, fw)


def _stage_with_halo(ext_ref, pieces, rows):
    @pl.when(pl.program_id(0) == 0)
    def _():
        ext_ref[0:8, :] = jnp.zeros((8, ext_ref.shape[1]), F32)

    col = 0
    for ref in pieces:
        w = ref.shape[1]
        ext_ref[8:8 + rows, col:col + w] = ref[...]
        col += w


def _keep_halo(ext_ref, rows):
    ext_ref[0:8, :] = ext_ref[rows:rows + 8, :]


def _causal_conv(ext_ref, cw, rows):
    acc = ext_ref[pl.ds(8, rows), :] * cw[3:4, :]
    for i in range(CONV_W - 1):
        acc = acc + ext_ref[pl.ds(5 + i, rows), :] * cw[i:i + 1, :]
    return acc


def _const_spec(shape):
    return pl.BlockSpec(shape, lambda i: (0,) * len(shape))


PAR_ROWS, PAR_W = 32, 1792
(R_GDN_CONV, R_GDN_ALOG, R_GDN_DTB, R_GDN_NW, R_RET_NW, R_SSD_CONV, R_SSD_CB, R_SSD_ALOG, R_SSD_DTB,
 R_SSD_D, R_SSD_NW, R_RW_MU, R_RW_W0, R_RW_A0, R_RW_KK, R_RW_KA, R_RW_RK, R_RW_LNW, R_RW_LNB) = (
    0, 4, 5, 6, 7, 8, 12, 13, 14, 15, 16, 17, 18, 19, 20, 21, 22, 23, 24)


def _layer_spec(shape, l):
    return pl.BlockSpec((None,) + tuple(shape), lambda i: (l,) + (0,) * len(shape))


def _par(par_ref, row, width, nrows=1):
    return par_ref[row:row + nrows, 0:width]


def _block_diag_ones(n, block, lower=False, dtype=F32):
    r = jnp.arange(n)[:, None]
    c = jnp.arange(n)[None, :]
    m = (r // block) == (c // block)
    if lower:
        m = m & (r >= c)
    return m.astype(dtype)


def _gdn_kernel(qkv_ref, z_ref, gate_ref, par_ref, btri_ref, eye_ref, o_ref, ext_ref, s_ref, *, nch):
    rows = nch * CHUNK

    @pl.when(pl.program_id(0) == 0)
    def _():
        s_ref[...] = jnp.zeros_like(s_ref)

    _stage_with_halo(ext_ref, (qkv_ref,), rows)
    qkv = _silu(_causal_conv(ext_ref, _par(par_ref, R_GDN_CONV, 1536, CONV_W), rows))
    _keep_halo(ext_ref, rows)

    causal, strict, _ = _chunk_masks()
    pair = _Pair()

    gt = gate_ref[...]
    beta = _sigmoid(gt)
    log_a = -jnp.exp(_par(par_ref, R_GDN_ALOG, 128)) * _softplus(gt + _par(par_ref, R_GDN_DTB, 128))
    g_all = _dot_sel(btri_ref[...], log_a)
    g_t = _dot_sel(g_all, eye_ref[...], TN)
    z = z_ref[...]
    nw = _par(par_ref, R_GDN_NW, GDN_DV)

    qs, ks, vs = [], [], []
    for h in range(GDN_HEADS):
        qh = qkv[:, h * GDN_DK:(h + 1) * GDN_DK]
        kh = qkv[:, 512 + h * GDN_DK:512 + (h + 1) * GDN_DK]
        qs.append(qh * lax.rsqrt(jnp.sum(qh * qh, axis=-1, keepdims=True) + EPS) * (GDN_DK ** -0.5))
        ks.append(kh * lax.rsqrt(jnp.sum(kh * kh, axis=-1, keepdims=True) + EPS))
        vs.append(qkv[:, 1024 + h * GDN_DV:1024 + (h + 1) * GDN_DV])

    combos = [(c, h) for c in range(nch) for h in range(GDN_HEADS)]
    rsl = lambda c: slice(c * CHUNK, (c + 1) * CHUNK)
    bcol = {(c, h): beta[rsl(c), h:h + 1] for c, h in combos}
    gcol = {(c, h): g_all[rsl(c), 4 + h:5 + h] for c, h in combos}
    kq = {(c, h): _dot1(jnp.concatenate([ks[h][rsl(c)], qs[h][rsl(c)]], axis=0), ks[h][rsl(c)], NT)
          for c, h in combos}
    lows, qks = {}, {}
    for c, h in combos:
        grow = g_t[4 + h:5 + h, c * CHUNK:(c + 1) * CHUNK]
        gamma = jnp.exp(jnp.where(causal, gcol[c, h] - grow, -jnp.inf))
        lows[c, h] = jnp.where(strict, kq[c, h][:CHUNK] * gamma * bcol[c, h], 0.0)
        qks[c, h] = kq[c, h][CHUNK:] * gamma
    pairs = [(c, p) for c in range(nch) for p in range(GDN_HEADS // 2)]
    tps = pair.inverse([jnp.concatenate([lows[c, 2 * p], lows[c, 2 * p + 1]], axis=1) for c, p in pairs])
    tinv = {}
    for (c, p), tp in zip(pairs, tps):
        tinv[c, 2 * p] = tp[:, :CHUNK]
        tinv[c, 2 * p + 1] = tp[:, CHUNK:]
    egc = {ch: jnp.exp(gcol[ch]) for ch in combos}
    glast = {ch: gcol[ch][CHUNK - 1:CHUNK, :] for ch in combos}
    wu = {(c, h): _dot1(tinv[c, h], jnp.concatenate(
        [ks[h][rsl(c)] * (bcol[c, h] * egc[c, h]), vs[h][rsl(c)] * bcol[c, h]], axis=1)) for c, h in combos}
    qwu = {ch: _dot1(qks[ch], wu[ch]) for ch in combos}
    kwu = {(c, h): _dot1(ks[h][rsl(c)] * jnp.exp(glast[c, h] - gcol[c, h]), wu[c, h], TN) for c, h in combos}
    for c in range(nch):
        rs = rsl(c)
        s0 = [s_ref[h] if c == 0 else s_new[h] for h in range(GDN_HEADS)]
        s_new = []
        for h in range(GDN_HEADS):
            sl = slice(h * GDN_DV, (h + 1) * GDN_DV)
            q_eff = qs[h][rs] * egc[c, h] - qwu[c, h][:, :GDN_DK]
            o = _dot1(q_eff, s0[h]) + qwu[c, h][:, GDN_DK:]
            s_new.append(s0[h] * jnp.exp(glast[c, h]) - _dot1(kwu[c, h][:, :GDN_DK], s0[h])
                         + kwu[c, h][:, GDN_DK:])
            o = o * lax.rsqrt(jnp.mean(o * o, axis=-1, keepdims=True) + EPS) * nw
            o_ref[rs, sl] = o * _silu(z[rs, sl])
    for h in range(GDN_HEADS):
        s_ref[h] = s_new[h]


def _gdn(proj, par, l, nch):
    t = proj.shape[0]
    rows = nch * CHUNK
    return pl.pallas_call(
        functools.partial(_gdn_kernel, nch=nch),
        out_shape=jax.ShapeDtypeStruct((t, MIX_W), F32),
        grid=(t // rows,),
        in_specs=[pl.BlockSpec((rows, 1536), lambda i: (i, 0)),
                  pl.BlockSpec((rows, 512), lambda i: (i, 8)),
                  pl.BlockSpec((rows, 128), lambda i: (i, 52)),
                  _layer_spec((PAR_ROWS, PAR_W), l),
                  _const_spec((rows, rows)), _const_spec((rows, rows))],
        out_specs=pl.BlockSpec((rows, MIX_W), lambda i: (i, 0)),
        scratch_shapes=[pltpu.VMEM((rows + 8, 1536), F32),
                        pltpu.VMEM((GDN_HEADS, GDN_DK, GDN_DV), F32)],
        compiler_params=_cparams(1),
    )(proj, proj, proj, par, _block_diag_ones(rows, CHUNK, lower=True, dtype=BF16),
      jnp.eye(rows, dtype=BF16))


def _ret_kernel(qk_ref, v_ref, g_ref, cos_ref, sin_ref, par_ref, o_ref, r_ref, *, nch):
    @pl.when(pl.program_id(0) == 0)
    def _():
        r_ref[...] = jnp.zeros_like(r_ref)

    rows = nch * CHUNK
    qk = qk_ref[...]
    cos = cos_ref[...]
    sin = sin_ref[...]
    even = (_iota2((rows, 256), 1) % 2) == 0

    def rot(x):
        partner = jnp.where(even, pltpu.roll(x, 255, 1), pltpu.roll(x, 1, 1))
        return x * cos + partner * sin

    q = rot(qk[:, 0:256])
    k = rot(qk[:, 256:512]) * (RET_DK ** -0.5)
    v = v_ref[...]
    gate = g_ref[...]
    nw = _par(par_ref, R_RET_NW, MIX_W)

    pair = _Pair()
    pos = _iota2((CHUNK, 2 * CHUNK), 0).astype(F32)
    zeros_v = jnp.zeros((CHUNK, RET_DV), F32)
    r2 = _iota2((2 * RET_DK, 2 * RET_DV), 0)
    c2 = _iota2((2 * RET_DK, 2 * RET_DV), 1)
    state_block = (r2 < RET_DK) == (c2 < RET_DV)

    for p in range(RET_HEADS // 2):
        lg0 = math.log1p(-(2.0 ** (-5.0 - 2 * p)))
        lg1 = math.log1p(-(2.0 ** (-5.0 - (2 * p + 1))))
        lg = jnp.where(pair.lo, lg0, lg1)
        dmask = jnp.exp(jnp.where(pair.causal, pair.dist * lg, -jnp.inf))
        k_dec = jnp.exp((CHUNK - 1.0 - pos) * lg)
        q_dec = jnp.exp((pos + 1.0) * lg)
        lg_v = jnp.where(_iota2((1, 2 * RET_DV), 1) < RET_DV, lg0, lg1)
        chunk_decay = jnp.exp(CHUNK * lg_v)
        ls = slice(p * 2 * RET_DK, (p + 1) * 2 * RET_DK)
        vsl = slice(p * 2 * RET_DV, (p + 1) * 2 * RET_DV)
        rsl = lambda c: slice(c * CHUNK, (c + 1) * CHUNK)
        sc = [_dot1(q[rsl(c), ls], pair.bd(k[rsl(c), ls]), NT) * dmask for c in range(nch)]
        upd = [_dot1(k[rsl(c), ls] * k_dec, v[rsl(c), vsl], TN) for c in range(nch)]
        r_start = []
        r_cur = r_ref[p]
        for c in range(nch):
            r_start.append(r_cur)
            r_cur = r_cur * chunk_decay + jnp.where(state_block, upd[c], 0.0)
        r_ref[p] = r_cur
        for c in range(nch):
            rs = rsl(c)
            vp = v[rs, vsl]
            v_bd = jnp.concatenate(
                [jnp.concatenate([vp[:, :RET_DV], zeros_v], axis=1),
                 jnp.concatenate([zeros_v, vp[:, RET_DV:]], axis=1)], axis=0)
            o = _dot1(sc[c], v_bd) + _dot1(q[rs, ls] * q_dec, r_start[c])
            for hh in range(2):
                sl = slice(vsl.start + hh * RET_DV, vsl.start + (hh + 1) * RET_DV)
                oh = o[:, hh * RET_DV:(hh + 1) * RET_DV]
                mu = jnp.mean(oh, axis=-1, keepdims=True)
                oc = oh - mu
                oh = oc * lax.rsqrt(jnp.mean(oc * oc, axis=-1, keepdims=True) + EPS) * nw[:, sl]
                o_ref[rs, sl] = oh * _silu(gate[rs, sl])


def _retention(proj, cos, sin, par, l, nch):
    t = proj.shape[0]
    rows = nch * CHUNK
    return pl.pallas_call(
        functools.partial(_ret_kernel, nch=nch),
        out_shape=jax.ShapeDtypeStruct((t, MIX_W), F32),
        grid=(t // rows,),
        in_specs=[pl.BlockSpec((rows, 512), lambda i: (i, 9)),
                  pl.BlockSpec((rows, 512), lambda i: (i, 10)),
                  pl.BlockSpec((rows, 512), lambda i: (i, 11)),
                  pl.BlockSpec((rows, 256), lambda i: (i, 0)),
                  pl.BlockSpec((rows, 256), lambda i: (i, 0)),
                  _layer_spec((PAR_ROWS, PAR_W), l)],
        out_specs=pl.BlockSpec((rows, MIX_W), lambda i: (i, 0)),
        scratch_shapes=[pltpu.VMEM((RET_HEADS // 2, 2 * RET_DK, 2 * RET_DV), F32)],
        compiler_params=_cparams(1),
    )(proj, proj, proj, cos, sin, par)


def _ssd_kernel(z_ref, xbc_ref, dt_ref, par_ref, btri_ref, eye_ref, expand_ref, o_ref, ext_ref, h_ref,
                *, nch):
    rows = nch * CHUNK

    @pl.when(pl.program_id(0) == 0)
    def _():
        h_ref[...] = jnp.zeros_like(h_ref)

    _stage_with_halo(ext_ref, (xbc_ref,), rows)
    xbc = _silu(_causal_conv(ext_ref, _par(par_ref, R_SSD_CONV, 1024, CONV_W), rows)
                + _par(par_ref, R_SSD_CB, 1024))
    _keep_halo(ext_ref, rows)
    x = xbc[:, 0:512]

    pair = _Pair()
    dt = _softplus(dt_ref[...] + _par(par_ref, R_SSD_DTB, 128))
    g_all = _dot_sel(btri_ref[...], dt * (-jnp.exp(_par(par_ref, R_SSD_ALOG, 128))))
    g_t = _dot_sel(g_all, eye_ref[...], TN)
    wide = _dot_sel(jnp.concatenate([g_all, dt], axis=0), expand_ref[...])
    g_w, dt_w = wide[:rows], wide[rows:]
    gtot_w = _rows_bcast(g_w, nch)
    xdt = x * dt_w
    xtail = xdt * jnp.exp(gtot_w - g_w)
    eg_w = jnp.exp(g_w)
    etot_w = jnp.exp(gtot_w)
    z = z_ref[...]
    d_w = _par(par_ref, R_SSD_D, MIX_W)
    nw = _par(par_ref, R_SSD_NW, MIX_W)
    gw = M2_HEADS // M2_GROUPS * M2_HEADDIM

    combos = [(c, gi) for c in range(nch) for gi in range(M2_GROUPS)]
    rsl = lambda c: slice(c * CHUNK, (c + 1) * CHUNK)
    bg = {(c, gi): xbc[rsl(c), 512 + gi * M2_STATE:512 + (gi + 1) * M2_STATE] for c, gi in combos}
    cg = {(c, gi): xbc[rsl(c), 768 + gi * M2_STATE:768 + (gi + 1) * M2_STATE] for c, gi in combos}
    cb2 = {k: _dot1(cg[k], jnp.concatenate([bg[k], bg[k]], axis=0), NT) for k in combos}
    upd = {(c, gi): _dot1(bg[c, gi], xtail[rsl(c), gi * gw:(gi + 1) * gw], TN) for c, gi in combos}
    y_in = {}
    for c, gi in combos:
        for pp in range(2):
            ha = gi * 4 + 2 * pp
            gcol = jnp.where(pair.lo, g_all[rsl(c), ha:ha + 1], g_all[rsl(c), ha + 1:ha + 2])
            grow = jnp.concatenate([g_t[ha:ha + 1, rsl(c)], g_t[ha + 1:ha + 2, rsl(c)]], axis=1)
            decay = jnp.exp(jnp.where(pair.causal, gcol - grow, -jnp.inf))
            y_in[c, gi, pp] = _dot1(cb2[c, gi] * decay,
                                    pair.bd(xdt[rsl(c), ha * M2_HEADDIM:(ha + 2) * M2_HEADDIM]))
    h_start = {}
    for gi in range(M2_GROUPS):
        h = h_ref[gi]
        for c in range(nch):
            h_start[c, gi] = h
            h = h * etot_w[c * CHUNK:c * CHUNK + 1, gi * gw:(gi + 1) * gw] + upd[c, gi]
        h_ref[gi] = h
    for c, gi in combos:
        rs, gs = rsl(c), slice(gi * gw, (gi + 1) * gw)
        y = jnp.concatenate([y_in[c, gi, 0], y_in[c, gi, 1]], axis=1)
        y = y + _dot1(cg[c, gi], h_start[c, gi]) * eg_w[rs, gs]
        y = (y + d_w[:, gs] * x[rs, gs]) * _silu(z[rs, gs])
        y = y * lax.rsqrt(jnp.mean(y * y, axis=-1, keepdims=True) + EPS) * nw[:, gs]
        o_ref[rs, gs] = y


def _head_expand_matrix(n_heads, width):
    r = jnp.arange(128)[:, None]
    c = jnp.arange(n_heads * width)[None, :]
    return (r == c // width).astype(BF16)


def _ssd(proj, par, l, nch):
    t = proj.shape[0]
    rows = nch * CHUNK
    return pl.pallas_call(
        functools.partial(_ssd_kernel, nch=nch),
        out_shape=jax.ShapeDtypeStruct((t, MIX_W), F32),
        grid=(t // rows,),
        in_specs=[pl.BlockSpec((rows, 512), lambda i: (i, 12)),
                  pl.BlockSpec((rows, 1024), lambda i: (i, 3)),
                  pl.BlockSpec((rows, 128), lambda i: (i, 53)),
                  _layer_spec((PAR_ROWS, PAR_W), l),
                  _const_spec((rows, rows)), _const_spec((rows, rows)), _const_spec((128, 512))],
        out_specs=pl.BlockSpec((rows, MIX_W), lambda i: (i, 0)),
        scratch_shapes=[pltpu.VMEM((rows + 8, 1024), F32),
                        pltpu.VMEM((M2_GROUPS, M2_STATE, 256), F32)],
        compiler_params=_cparams(1),
    )(proj, proj, proj, par,
      _block_diag_ones(rows, CHUNK, lower=True, dtype=BF16), jnp.eye(rows, dtype=BF16),
      _head_expand_matrix(M2_HEADS, M2_HEADDIM))


def _rwkv_kernel(rkv_ref, lora_ref, par_ref, lmat_ref, btri_ref, hblk_ref, o_ref, ext_ref, s_ref, *, nch):
    rows = nch * CHUNK

    @pl.when(pl.program_id(0) == 0)
    def _():
        s_ref[...] = jnp.zeros_like(s_ref)

    _stage_with_halo(ext_ref, (rkv_ref, lora_ref), rows)
    cur = ext_ref[pl.ds(8, rows), :]
    prev = ext_ref[pl.ds(7, rows), :]
    _keep_halo(ext_ref, rows)
    mixed = cur + (prev - cur) * _par(par_ref, R_RW_MU, PAR_W)
    r = mixed[:, 0:512]
    k = mixed[:, 512:1024]
    v = mixed[:, 1024:1536]
    lora = mixed[:, 1536:1792]

    w_raw = -_softplus(-(_par(par_ref, R_RW_W0, MIX_W) + _dot1(jnp.tanh(lora), lmat_ref[0:256, :]))) - 0.5
    log_d = -jnp.exp(w_raw)
    a = _sigmoid(_par(par_ref, R_RW_A0, MIX_W) + _dot1(lora, lmat_ref[256:512, :]))
    gate = _dot1(_sigmoid(lora), lmat_ref[512:768, :])
    hblk = hblk_ref[...]

    g_in = _dot_sel(btri_ref[...], log_d)
    g_tot = _rows_bcast(g_in, nch)
    e_in = jnp.exp(g_in)
    e_neg = jnp.exp(-g_in)
    e_ex = jnp.exp(g_in - log_d)
    e_tail = jnp.exp(g_tot - g_in)
    e_end = jnp.exp(g_tot)

    kk = k * _par(par_ref, R_RW_KK, MIX_W)
    kk = kk * lax.rsqrt(_sum_bcast(kk * kk, hblk) + EPS)
    k_mod = k * (1.0 + (a - 1.0) * _par(par_ref, R_RW_KA, MIX_W))
    a_vec = -(a * kk)
    r_t = r * e_in
    b_t = kk * e_ex
    k_t = k_mod * e_neg
    a_t = a_vec * e_neg
    k_c = k_mod * e_tail
    a_c = a_vec * e_tail

    pair = _Pair()
    npair = RW_HEADS // 2
    combos = [(c, j) for c in range(nch) for j in range(npair)]
    cut = lambda arr, cj: arr[cj[0] * CHUNK:(cj[0] + 1) * CHUNK, cj[1] * 2 * RW_N:(cj[1] + 1) * 2 * RW_N]
    lhs = {cj: jnp.concatenate([cut(b_t, cj), cut(r_t, cj)], axis=0) for cj in combos}
    x1 = {cj: _dot1(lhs[cj], pair.bd(cut(k_t, cj)), NT) for cj in combos}
    x2 = {cj: _dot1(lhs[cj], pair.bd(cut(a_t, cj)), NT) for cj in combos}
    a_ra = {cj: jnp.where(pair.causal, x2[cj][CHUNK:], 0.0) for cj in combos}
    tinv = dict(zip(combos, pair.inverse([-jnp.where(pair.strict, x2[cj][:CHUNK], 0.0) for cj in combos])))
    av = {cj: _dot1(jnp.concatenate([jnp.where(pair.strict, x1[cj][:CHUNK], 0.0),
                                     jnp.where(pair.causal, x1[cj][CHUNK:], 0.0)], axis=0),
                    pair.bd(cut(v, cj))) for cj in combos}
    tz = {cj: _dot1(tinv[cj], jnp.concatenate([pair.bd(cut(b_t, cj)), pair.bd(av[cj][:CHUNK])], axis=1))
          for cj in combos}
    az = {cj: _dot1(a_ra[cj], jnp.concatenate([pair.bd(tz[cj][:, :2 * RW_N]), pair.bd(tz[cj][:, 2 * RW_N:])],
                                              axis=1)) for cj in combos}
    r_eff = {cj: cut(r_t, cj) + az[cj][:, :2 * RW_N] for cj in combos}
    y0 = {cj: av[cj][CHUNK:] + az[cj][:, 2 * RW_N:] for cj in combos}
    p_low = {cj: jnp.where(pair.same_block, _dot1(tz[cj][:, :2 * RW_N], cut(a_c, cj), TN), 0.0) for cj in combos}
    q_mat = {cj: jnp.where(pair.same_block,
                           _dot1(jnp.concatenate([cut(v, cj), tz[cj][:, 2 * RW_N:]], axis=0),
                                 jnp.concatenate([cut(k_c, cj), cut(a_c, cj)], axis=0), TN), 0.0)
             for cj in combos}
    y_rows = []
    for c in range(nch):
        s0 = [s_ref[j] if c == 0 else s_new[j] for j in range(npair)]
        y_rows.append(jnp.concatenate(
            [_dot1(r_eff[c, j], s0[j], NT) + y0[c, j] for j in range(npair)], axis=1))
        s_new = [s0[j] * e_end[c * CHUNK:c * CHUNK + 1, j * 2 * RW_N:(j + 1) * 2 * RW_N]
                 + _dot1(s0[j], p_low[c, j]) + q_mat[c, j] for j in range(npair)]
    for j in range(npair):
        s_ref[j] = s_new[j]
    y = jnp.concatenate(y_rows, axis=0)

    inv_n = 1.0 / RW_N
    mu = _sum_bcast(y, hblk, pieces=2) * inv_n
    yc = y - mu
    var = _sum_bcast(yc * yc, hblk) * inv_n
    y = yc * lax.rsqrt(var + RWKV_LN_EPS) * _par(par_ref, R_RW_LNW, MIX_W) + _par(par_ref, R_RW_LNB, MIX_W)
    y = y + _sum_bcast(r * k_mod * _par(par_ref, R_RW_RK, MIX_W), hblk) * v
    o_ref[...] = y * gate


def _rwkv(proj, par, lmat, l, nch):
    t = proj.shape[0]
    rows = nch * CHUNK
    return pl.pallas_call(
        functools.partial(_rwkv_kernel, nch=nch),
        out_shape=jax.ShapeDtypeStruct((t, MIX_W), F32),
        grid=(t // rows,),
        in_specs=[pl.BlockSpec((rows, 1536), lambda i: (i, 1)),
                  pl.BlockSpec((rows, 256), lambda i: (i, 27)),
                  _layer_spec((PAR_ROWS, PAR_W), l), _layer_spec((768, MIX_W), l),
                  _const_spec((rows, rows)), _const_spec((512, 512))],
        out_specs=pl.BlockSpec((rows, MIX_W), lambda i: (i, 0)),
        scratch_shapes=[pltpu.VMEM((rows + 8, 1792), F32),
                        pltpu.VMEM((RW_HEADS // 2, 2 * RW_N, 2 * RW_N), F32)],
        compiler_params=_cparams(1),
    )(proj, proj, par, lmat,
      _block_diag_ones(rows, CHUNK, lower=True, dtype=BF16), _block_diag_ones(MIX_W, RW_N, dtype=BF16))


def _layout_w_in_kernel(w_ref, o_ref):
    x = w_ref[...]
    col = 0
    for s, n, p in _SRC_PIECES:
        o_ref[:, col:col + n] = x[:, s:s + n].astype(BF16)
        if p > n:
            o_ref[:, col + n:col + p] = jnp.zeros((x.shape[0], p - n), BF16)
        col += p


def _layout_w_in(w, tr=256):
    depth, d, n = w.shape
    out = pl.pallas_call(
        _layout_w_in_kernel,
        out_shape=jax.ShapeDtypeStruct((depth * d, P_PAD), BF16),
        grid=(depth * d // tr,),
        in_specs=[pl.BlockSpec((tr, n), lambda i: (i, 0))],
        out_specs=pl.BlockSpec((tr, P_PAD), lambda i: (i, 0)),
        compiler_params=_cparams(1),
    )(w.reshape(depth * d, n))
    return out.reshape(depth, d, P_PAD)


def _param_slab(p):
    depth = p['w_in'].shape[0]

    def rows(a, lane=0):
        a = a.astype(F32).reshape(depth, -1, a.shape[-1])
        return jnp.pad(a, ((0, 0), (0, 0), (lane, PAR_W - lane - a.shape[-1])))

    pieces = [rows(p['gdn_conv_w']), rows(p['gdn_a_log'], 4), rows(p['gdn_dt_bias'], 4),
              rows(p['gdn_norm_w']), rows(p['ret_norm_w']),
              rows(p['m2_conv_w']), rows(p['m2_conv_b']), rows(p['m2_a_log']), rows(p['m2_dt_bias']),
              rows(jnp.repeat(p['m2_d'], M2_HEADDIM, axis=1)), rows(p['m2_norm_w']),
              rows(p['rw_mu']), rows(p['rw_w0']), rows(p['rw_a0']), rows(p['rw_k_k']), rows(p['rw_k_a']),
              rows(p['rw_r_k'].reshape(depth, -1)), rows(p['rw_ln_w']), rows(p['rw_ln_b'])]
    slab = jnp.concatenate(pieces, axis=1)
    return jnp.pad(slab, ((0, 0), (0, PAR_ROWS - slab.shape[1]), (0, 0)))


def _lora_mats(p):
    def at(m, offset):
        return jnp.pad(m, ((0, 0), (offset, 256 - offset - m.shape[1]), (0, 0)))
    return jnp.concatenate([at(p['rw_w_up'], 0), at(p['rw_a_up'], RW_W_LORA),
                            at(p['rw_g_up'], RW_W_LORA + RW_A_LORA)], axis=1).astype(BF16)


def _rotary_tables(t):
    theta = 1.0 / (ROPE_BASE ** jnp.linspace(0.0, 1.0, RET_DK // 2, dtype=F32))
    ang = jnp.arange(t, dtype=F32)[:, None] * theta
    cos = jnp.repeat(jnp.cos(ang), 2, axis=1)
    sin = jnp.stack([-jnp.sin(ang), jnp.sin(ang)], axis=-1).reshape(t, RET_DK)
    return jnp.tile(cos, (1, RET_HEADS)), jnp.tile(sin, (1, RET_HEADS))


NCH_GDN, NCH_RET, NCH_SSD, NCH_RWKV = 4, 4, 4, 4


def _prepare(p, t):
    cos, sin = _rotary_tables(t)
    return dict(norm1=p['norm1_w'].astype(F32)[:, None, :], w_in=_layout_w_in(p['w_in']),
                par=_param_slab(p), lmat=_lora_mats(p), w_out=p['w_out'].astype(BF16), cos=cos, sin=sin)


def _token_mix(h, l, q):
    t = h.shape[0]
    nch = lambda n: min(n, t // CHUNK)
    proj = _norm_matmul(h, q['norm1'], q['w_in'], l)
    o_a = _gdn(proj, q['par'], l, nch(NCH_GDN))
    o_b = _retention(proj, q['cos'], q['sin'], q['par'], l, nch(NCH_RET))
    o_c = _ssd(proj, q['par'], l, nch(NCH_SSD))
    o_d = _rwkv(proj, q['par'], q['lmat'], l, nch(NCH_RWKV))
    return _out_proj(h, (o_a, o_b, o_c, o_d), q['w_out'], l)


def kernel(x, norm1_w, w_in, gdn_conv_w, gdn_a_log, gdn_dt_bias, gdn_norm_w, ret_norm_w, m2_conv_w, m2_conv_b, m2_a_log, m2_dt_bias, m2_d, m2_norm_w, rw_mu, rw_w0, rw_w_up, rw_a0, rw_a_up, rw_g_up, rw_k_k, rw_k_a, rw_r_k, rw_ln_w, rw_ln_b, w_out, norm2_w, w_ffn_up, w_ffn_down, final_norm_w):
    p = dict(norm1_w=norm1_w, w_in=w_in, gdn_conv_w=gdn_conv_w, gdn_a_log=gdn_a_log,
             gdn_dt_bias=gdn_dt_bias, gdn_norm_w=gdn_norm_w, ret_norm_w=ret_norm_w,
             m2_conv_w=m2_conv_w, m2_conv_b=m2_conv_b, m2_a_log=m2_a_log, m2_dt_bias=m2_dt_bias,
             m2_d=m2_d, m2_norm_w=m2_norm_w, rw_mu=rw_mu, rw_w0=rw_w0, rw_w_up=rw_w_up,
             rw_a0=rw_a0, rw_a_up=rw_a_up, rw_g_up=rw_g_up, rw_k_k=rw_k_k, rw_k_a=rw_k_a,
             rw_r_k=rw_r_k, rw_ln_w=rw_ln_w, rw_ln_b=rw_ln_b, w_out=w_out)
    bsz, t, d = x.shape
    depth = w_in.shape[0]
    q = _prepare(p, t)
    norm2 = norm2_w.astype(F32)[:, None, :]
    ff = w_ffn_up.shape[2]
    w_up = w_ffn_up.astype(BF16).reshape(depth, d, ff // FFN_TF, FFN_TF).transpose(0, 2, 1, 3)
    w_down = w_ffn_down.astype(BF16)
    final_w = final_norm_w.astype(F32).reshape(1, d)
    outs = []
    for b in range(bsz):
        h = x[b]
        for l in range(depth):
            h = _token_mix(h, l, q)
            h = _ffn(h, norm2, w_up, w_down, final_w, l, final_norm=(l == depth - 1))
        outs.append(h)
    return outs[0].reshape(1, t, d) if bsz == 1 else jnp.stack(outs, axis=0)
```

```python
import functools
import math

import jax
import jax.numpy as jnp
from jax import lax
from jax.experimental import pallas as pl
from jax.experimental.pallas import tpu as pltpu

F32 = jnp.float32
BF16 = jnp.bfloat16

D_MODEL = 2048
D_FF = 4 * D_MODEL
CONV_W = 4
CHUNK = 64
EPS = 1e-6
ROPE_BASE = 10000.0
RWKV_LN_EPS = 64e-5

GDN_HEADS, GDN_DK, GDN_DV = 4, 128, 128
RET_HEADS, RET_DK, RET_DV = 4, 64, 128
M2_HEADS, M2_HEADDIM, M2_GROUPS, M2_STATE = 8, 64, 2, 128
RW_HEADS, RW_N = 8, 64
RW_W_LORA, RW_A_LORA, RW_G_LORA = 32, 32, 96
MIX_W = 512

_GDN0, _RET0, _M20, _RW0 = 0, 2056, 3592, 5136

P_PAD = 7168
_SRC_PIECES = (
    (_GDN0, 1536, 1536),
    (_RW0, 1536, 1536),
    (_M20 + 512, 1024, 1024),
    (_GDN0 + 1536, 512, 512),
    (_RET0, 512, 512),
    (_RET0 + 512, 512, 512),
    (_RET0 + 1024, 512, 512),
    (_M20, 512, 512),
    (_GDN0 + 2048, 8, 128),
    (_M20 + 1536, 8, 128),
    (_RW0 + 1536, 160, 256),
)

VMEM_LIMIT = 56 * 1024 * 1024

NN = (((1,), (0,)), ((), ()))
NT = (((1,), (1,)), ((), ()))
TN = (((0,), (0,)), ((), ()))


def _cparams(n_axes):
    return pltpu.CompilerParams(dimension_semantics=("arbitrary",) * n_axes,
                                vmem_limit_bytes=VMEM_LIMIT)


def _dg(a, b, dims=NN, prec=None):
    return lax.dot_general(a, b, dims, preferred_element_type=F32, precision=prec)


def _split(a):
    hi = a.astype(BF16)
    lo = (a - hi.astype(F32)).astype(BF16)
    return hi, lo


def _dot1(a, b, dims=NN):
    return _dg(a.astype(BF16), b.astype(BF16), dims)


def _split3(a):
    hi = a.astype(BF16)
    r1 = a - hi.astype(F32)
    mid = r1.astype(BF16)
    lo = (r1 - mid.astype(F32)).astype(BF16)
    return hi, mid, lo


def _dot_sel(a, b, dims=NN):
    if a.dtype == BF16:
        return sum(_dg(a, piece, dims) for piece in _split3(b))
    return sum(_dg(piece, b, dims) for piece in _split3(a))


def _sum_bcast(x, blk, pieces=1):
    outs = []
    for s in range(0, x.shape[1], 256):
        xs, bs = x[:, s:s + 256], blk[s:s + 256, s:s + 256]
        if pieces == 1:
            outs.append(_dg(xs.astype(BF16), bs))
        else:
            xh, xl = _split(xs)
            outs.append(_dg(xh, bs) + _dg(xl, bs))
    return jnp.concatenate(outs, axis=1)


def _rows_bcast(x, nch):
    return jnp.concatenate(
        [jnp.broadcast_to(x[(c + 1) * CHUNK - 1:(c + 1) * CHUNK, :], (CHUNK, x.shape[1])) for c in range(nch)],
        axis=0)


def _sigmoid(x):
    return 1.0 / (1.0 + jnp.exp(-x))


def _silu(x):
    return x * _sigmoid(x)


def _softplus(x):
    return jnp.maximum(x, 0.0) + jnp.log1p(jnp.exp(-jnp.abs(x)))


def _iota2(shape, axis):
    return lax.broadcasted_iota(jnp.int32, shape, axis)


def _chunk_masks():
    r = _iota2((CHUNK, CHUNK), 0)
    c = _iota2((CHUNK, CHUNK), 1)
    return r >= c, r > c, r == c


class _Pair:
    def __init__(self):
        lane = _iota2((CHUNK, 2 * CHUNK), 1)
        row = _iota2((CHUNK, 2 * CHUNK), 0)
        self.lo = lane < CHUNK
        col = jnp.where(self.lo, lane, lane - CHUNK)
        self.dist = (row - col).astype(F32)
        self.causal = row >= col
        self.strict = row > col
        self.eye = jnp.where(row == col, 1.0, 0.0).astype(F32)
        r2 = _iota2((2 * CHUNK, 2 * CHUNK), 0)
        c2 = _iota2((2 * CHUNK, 2 * CHUNK), 1)
        self.same_block = (r2 < CHUNK) == (c2 < CHUNK)

    def bd(self, x):
        z = jnp.zeros_like(x)
        return jnp.concatenate([jnp.where(self.lo, x, z), jnp.where(self.lo, z, x)], axis=0)

    def inverse(self, lows):
        ps = [self.eye - low for low in lows]
        curs = list(lows)
        n = 1
        while n < CHUNK:
            rhss = [self.bd(cur) for cur in curs]
            if n == 1:
                curs = [_dot1(cur, rhs) for cur, rhs in zip(curs, rhss)]
            elif 2 * n < CHUNK:
                outs = [_dot1(jnp.concatenate([cur, p], axis=0), rhs)
                        for cur, p, rhs in zip(curs, ps, rhss)]
                curs = [out[:CHUNK] for out in outs]
                ps = [p + out[CHUNK:] for p, out in zip(ps, outs)]
            else:
                ps = [p + _dot1(p, rhs) for p, rhs in zip(ps, rhss)]
            n *= 2
        return ps


def _norm_matmul_kernel(x_ref, nw_ref, w_ref, o_ref, a_ref):
    @pl.when(pl.program_id(1) == 0)
    def _():
        x = x_ref[...]
        y = x * lax.rsqrt(jnp.mean(x * x, axis=-1, keepdims=True) + EPS) * nw_ref[...]
        a_ref[...] = y.astype(BF16)

    o_ref[...] = jnp.dot(a_ref[...], w_ref[...], preferred_element_type=F32)


def _norm_matmul(x, nw, w, l, tm=1024, tn=1792):
    t, d = x.shape
    n = w.shape[2]
    tm = min(tm, t)
    return pl.pallas_call(
        _norm_matmul_kernel,
        out_shape=jax.ShapeDtypeStruct((t, n), F32),
        grid=(t // tm, n // tn),
        in_specs=[pl.BlockSpec((tm, d), lambda i, j: (i, 0)),
                  pl.BlockSpec((None, 1, d), lambda i, j: (l, 0, 0)),
                  pl.BlockSpec((None, d, tn), lambda i, j: (l, 0, j))],
        out_specs=pl.BlockSpec((tm, tn), lambda i, j: (i, j)),
        scratch_shapes=[pltpu.VMEM((tm, d), BF16)],
        compiler_params=_cparams(2),
    )(x, nw, w)


def _out_proj_kernel(h_ref, oa_ref, ob_ref, oc_ref, od_ref, w_ref, o_ref):
    acc = h_ref[...]
    for idx, part in enumerate((oa_ref, ob_ref, oc_ref, od_ref)):
        acc = acc + jnp.dot(part[...].astype(BF16), w_ref[idx * MIX_W:(idx + 1) * MIX_W, :],
                            preferred_element_type=F32)
    o_ref[...] = acc


def _out_proj(h, parts, w, l, tm=512):
    t, d = h.shape
    tm = min(tm, t)
    part_spec = pl.BlockSpec((tm, MIX_W), lambda i: (i, 0))
    return pl.pallas_call(
        _out_proj_kernel,
        out_shape=jax.ShapeDtypeStruct((t, d), F32),
        grid=(t // tm,),
        in_specs=[pl.BlockSpec((tm, d), lambda i: (i, 0)),
                  part_spec, part_spec, part_spec, part_spec,
                  pl.BlockSpec((None, 4 * MIX_W, d), lambda i: (l, 0, 0))],
        out_specs=pl.BlockSpec((tm, d), lambda i: (i, 0)),
        compiler_params=_cparams(1),
    )(h, *parts, w)


def _ffn_kernel(h_ref, nw_ref, wu_ref, wd_ref, fw_ref, o_ref, a_ref, *, final_norm):
    f = pl.program_id(1)

    @pl.when(f == 0)
    def _():
        x = h_ref[...]
        y = x * lax.rsqrt(jnp.mean(x * x, axis=-1, keepdims=True) + EPS) * nw_ref[...]
        a_ref[...] = y.astype(BF16)
        o_ref[...] = x

    u = jnp.dot(a_ref[...], wu_ref[...], preferred_element_type=F32)
    s = jnp.square(jnp.maximum(u, 0.0)).astype(BF16)
    o_ref[...] += jnp.dot(s, wd_ref[...], preferred_element_type=F32)

    if final_norm:
        @pl.when(f == pl.num_programs(1) - 1)
        def _():
            y = o_ref[...]
            o_ref[...] = y * lax.rsqrt(jnp.mean(y * y, axis=-1, keepdims=True) + EPS) * fw_ref[...]


def _ffn(h, nw, wu, wd, fw, l, final_norm, tm=1024, tf=512):
    t, d = h.shape
    ff = wu.shape[2]
    tm = min(tm, t)
    return pl.pallas_call(
        functools.partial(_ffn_kernel, final_norm=final_norm),
        out_shape=jax.ShapeDtypeStruct((t, d), F32),
        grid=(t // tm, ff // tf),
        in_specs=[pl.BlockSpec((tm, d), lambda i, f: (i, 0)),
                  pl.BlockSpec((None, 1, d), lambda i, f: (l, 0, 0)),
                  pl.BlockSpec((None, d, tf), lambda i, f: (l, 0, f)),
                  pl.BlockSpec((None, tf, d), lambda i, f: (l, f, 0)),
                  pl.BlockSpec((1, d), lambda i, f: (0, 0))],
        out_specs=pl.BlockSpec((tm, d), lambda i, f: (i, 0)),
        scratch_shapes=[pltpu.VMEM((tm, d), BF16)],
        compiler_params=_cparams(2),
    )(h, nw, wu, wd, fw)


def _stage_with_halo(ext_ref, pieces, rows):
    @pl.when(pl.program_id(0) == 0)
    def _():
        ext_ref[0:8, :] = jnp.zeros((8, ext_ref.shape[1]), F32)

    col = 0
    for ref in pieces:
        w = ref.shape[1]
        ext_ref[8:8 + rows, col:col + w] = ref[...]
        col += w


def _keep_halo(ext_ref, rows):
    ext_ref[0:8, :] = ext_ref[rows:rows + 8, :]


def _causal_conv(ext_ref, cw, rows):
    acc = ext_ref[pl.ds(8, rows), :] * cw[3:4, :]
    for i in range(CONV_W - 1):
        acc = acc + ext_ref[pl.ds(5 + i, rows), :] * cw[i:i + 1, :]
    return acc


def _const_spec(shape):
    return pl.BlockSpec(shape, lambda i: (0,) * len(shape))


PAR_ROWS, PAR_W = 32, 1792
(R_GDN_CONV, R_GDN_ALOG, R_GDN_DTB, R_GDN_NW, R_RET_NW, R_SSD_CONV, R_SSD_CB, R_SSD_ALOG, R_SSD_DTB,
 R_SSD_D, R_SSD_NW, R_RW_MU, R_RW_W0, R_RW_A0, R_RW_KK, R_RW_KA, R_RW_RK, R_RW_LNW, R_RW_LNB) = (
    0, 4, 5, 6, 7, 8, 12, 13, 14, 15, 16, 17, 18, 19, 20, 21, 22, 23, 24)


def _layer_spec(shape, l):
    return pl.BlockSpec((None,) + tuple(shape), lambda i: (l,) + (0,) * len(shape))


def _par(par_ref, row, width, nrows=1):
    return par_ref[row:row + nrows, 0:width]


def _block_diag_ones(n, block, lower=False, dtype=F32):
    r = jnp.arange(n)[:, None]
    c = jnp.arange(n)[None, :]
    m = (r // block) == (c // block)
    if lower:
        m = m & (r >= c)
    return m.astype(dtype)


def _gdn_kernel(qkv_ref, z_ref, gate_ref, par_ref, btri_ref, eye_ref, o_ref, ext_ref, s_ref, *, nch):
    rows = nch * CHUNK

    @pl.when(pl.program_id(0) == 0)
    def _():
        s_ref[...] = jnp.zeros_like(s_ref)

    _stage_with_halo(ext_ref, (qkv_ref,), rows)
    qkv = _silu(_causal_conv(ext_ref, _par(par_ref, R_GDN_CONV, 1536, CONV_W), rows))
    _keep_halo(ext_ref, rows)

    causal, strict, _ = _chunk_masks()
    pair = _Pair()

    gt = gate_ref[...]
    beta = _sigmoid(gt)
    log_a = -jnp.exp(_par(par_ref, R_GDN_ALOG, 128)) * _softplus(gt + _par(par_ref, R_GDN_DTB, 128))
    g_all = _dot_sel(btri_ref[...], log_a)
    g_t = _dot_sel(g_all, eye_ref[...], TN)
    z = z_ref[...]
    nw = _par(par_ref, R_GDN_NW, GDN_DV)

    qs, ks, vs = [], [], []
    for h in range(GDN_HEADS):
        qh = qkv[:, h * GDN_DK:(h + 1) * GDN_DK]
        kh = qkv[:, 512 + h * GDN_DK:512 + (h + 1) * GDN_DK]
        qs.append(qh * lax.rsqrt(jnp.sum(qh * qh, axis=-1, keepdims=True) + EPS) * (GDN_DK ** -0.5))
        ks.append(kh * lax.rsqrt(jnp.sum(kh * kh, axis=-1, keepdims=True) + EPS))
        vs.append(qkv[:, 1024 + h * GDN_DV:1024 + (h + 1) * GDN_DV])

    combos = [(c, h) for c in range(nch) for h in range(GDN_HEADS)]
    rsl = lambda c: slice(c * CHUNK, (c + 1) * CHUNK)
    bcol = {(c, h): beta[rsl(c), h:h + 1] for c, h in combos}
    gcol = {(c, h): g_all[rsl(c), 4 + h:5 + h] for c, h in combos}
    kq = {(c, h): _dot1(jnp.concatenate([ks[h][rsl(c)], qs[h][rsl(c)]], axis=0), ks[h][rsl(c)], NT)
          for c, h in combos}
    lows, qks = {}, {}
    for c, h in combos:
        grow = g_t[4 + h:5 + h, c * CHUNK:(c + 1) * CHUNK]
        gamma = jnp.exp(jnp.where(causal, gcol[c, h] - grow, -jnp.inf))
        lows[c, h] = jnp.where(strict, kq[c, h][:CHUNK] * gamma * bcol[c, h], 0.0)
        qks[c, h] = kq[c, h][CHUNK:] * gamma
    pairs = [(c, p) for c in range(nch) for p in range(GDN_HEADS // 2)]
    tps = pair.inverse([jnp.concatenate([lows[c, 2 * p], lows[c, 2 * p + 1]], axis=1) for c, p in pairs])
    tinv = {}
    for (c, p), tp in zip(pairs, tps):
        tinv[c, 2 * p] = tp[:, :CHUNK]
        tinv[c, 2 * p + 1] = tp[:, CHUNK:]
    egc = {ch: jnp.exp(gcol[ch]) for ch in combos}
    glast = {ch: gcol[ch][CHUNK - 1:CHUNK, :] for ch in combos}
    wu = {(c, h): _dot1(tinv[c, h], jnp.concatenate(
        [ks[h][rsl(c)] * (bcol[c, h] * egc[c, h]), vs[h][rsl(c)] * bcol[c, h]], axis=1)) for c, h in combos}
    qwu = {ch: _dot1(qks[ch], wu[ch]) for ch in combos}
    kwu = {(c, h): _dot1(ks[h][rsl(c)] * jnp.exp(glast[c, h] - gcol[c, h]), wu[c, h], TN) for c, h in combos}
    for c in range(nch):
        rs = rsl(c)
        s0 = [s_ref[h] if c == 0 else s_new[h] for h in range(GDN_HEADS)]
        s_new = []
        for h in range(GDN_HEADS):
            sl = slice(h * GDN_DV, (h + 1) * GDN_DV)
            q_eff = qs[h][rs] * egc[c, h] - qwu[c, h][:, :GDN_DK]
            o = _dot1(q_eff, s0[h]) + qwu[c, h][:, GDN_DK:]
            s_new.append(s0[h] * jnp.exp(glast[c, h]) - _dot1(kwu[c, h][:, :GDN_DK], s0[h])
                         + kwu[c, h][:, GDN_DK:])
            o = o * lax.rsqrt(jnp.mean(o * o, axis=-1, keepdims=True) + EPS) * nw
            o_ref[rs, sl] = o * _silu(z[rs, sl])
    for h in range(GDN_HEADS):
        s_ref[h] = s_new[h]


def _gdn(proj, par, l, nch):
    t = proj.shape[0]
    rows = nch * CHUNK
    return pl.pallas_call(
        functools.partial(_gdn_kernel, nch=nch),
        out_shape=jax.ShapeDtypeStruct((t, MIX_W), F32),
        grid=(t // rows,),
        in_specs=[pl.BlockSpec((rows, 1536), lambda i: (i, 0)),
                  pl.BlockSpec((rows, 512), lambda i: (i, 8)),
                  pl.BlockSpec((rows, 128), lambda i: (i, 52)),
                  _layer_spec((PAR_ROWS, PAR_W), l),
                  _const_spec((rows, rows)), _const_spec((rows, rows))],
        out_specs=pl.BlockSpec((rows, MIX_W), lambda i: (i, 0)),
        scratch_shapes=[pltpu.VMEM((rows + 8, 1536), F32),
                        pltpu.VMEM((GDN_HEADS, GDN_DK, GDN_DV), F32)],
        compiler_params=_cparams(1),
    )(proj, proj, proj, par, _block_diag_ones(rows, CHUNK, lower=True, dtype=BF16),
      jnp.eye(rows, dtype=BF16))


def _ret_kernel(qk_ref, v_ref, g_ref, cos_ref, sin_ref, par_ref, o_ref, r_ref, *, nch):
    @pl.when(pl.program_id(0) == 0)
    def _():
        r_ref[...] = jnp.zeros_like(r_ref)

    rows = nch * CHUNK
    qk = qk_ref[...]
    cos = cos_ref[...]
    sin = sin_ref[...]
    even = (_iota2((rows, 256), 1) % 2) == 0

    def rot(x):
        partner = jnp.where(even, pltpu.roll(x, 255, 1), pltpu.roll(x, 1, 1))
        return x * cos + partner * sin

    q = rot(qk[:, 0:256])
    k = rot(qk[:, 256:512]) * (RET_DK ** -0.5)
    v = v_ref[...]
    gate = g_ref[...]
    nw = _par(par_ref, R_RET_NW, MIX_W)

    pair = _Pair()
    pos = _iota2((CHUNK, 2 * CHUNK), 0).astype(F32)
    zeros_v = jnp.zeros((CHUNK, RET_DV), F32)
    r2 = _iota2((2 * RET_DK, 2 * RET_DV), 0)
    c2 = _iota2((2 * RET_DK, 2 * RET_DV), 1)
    state_block = (r2 < RET_DK) == (c2 < RET_DV)

    for p in range(RET_HEADS // 2):
        lg0 = math.log1p(-(2.0 ** (-5.0 - 2 * p)))
        lg1 = math.log1p(-(2.0 ** (-5.0 - (2 * p + 1))))
        lg = jnp.where(pair.lo, lg0, lg1)
        dmask = jnp.exp(jnp.where(pair.causal, pair.dist * lg, -jnp.inf))
        k_dec = jnp.exp((CHUNK - 1.0 - pos) * lg)
        q_dec = jnp.exp((pos + 1.0) * lg)
        lg_v = jnp.where(_iota2((1, 2 * RET_DV), 1) < RET_DV, lg0, lg1)
        chunk_decay = jnp.exp(CHUNK * lg_v)
        ls = slice(p * 2 * RET_DK, (p + 1) * 2 * RET_DK)
        vsl = slice(p * 2 * RET_DV, (p + 1) * 2 * RET_DV)
        rsl = lambda c: slice(c * CHUNK, (c + 1) * CHUNK)
        sc = [_dot1(q[rsl(c), ls], pair.bd(k[rsl(c), ls]), NT) * dmask for c in range(nch)]
        upd = [_dot1(k[rsl(c), ls] * k_dec, v[rsl(c), vsl], TN) for c in range(nch)]
        r_start = []
        r_cur = r_ref[p]
        for c in range(nch):
            r_start.append(r_cur)
            r_cur = r_cur * chunk_decay + jnp.where(state_block, upd[c], 0.0)
        r_ref[p] = r_cur
        for c in range(nch):
            rs = rsl(c)
            vp = v[rs, vsl]
            v_bd = jnp.concatenate(
                [jnp.concatenate([vp[:, :RET_DV], zeros_v], axis=1),
                 jnp.concatenate([zeros_v, vp[:, RET_DV:]], axis=1)], axis=0)
            o = _dot1(sc[c], v_bd) + _dot1(q[rs, ls] * q_dec, r_start[c])
            for hh in range(2):
                sl = slice(vsl.start + hh * RET_DV, vsl.start + (hh + 1) * RET_DV)
                oh = o[:, hh * RET_DV:(hh + 1) * RET_DV]
                mu = jnp.mean(oh, axis=-1, keepdims=True)
                oc = oh - mu
                oh = oc * lax.rsqrt(jnp.mean(oc * oc, axis=-1, keepdims=True) + EPS) * nw[:, sl]
                o_ref[rs, sl] = oh * _silu(gate[rs, sl])


def _retention(proj, cos, sin, par, l, nch):
    t = proj.shape[0]
    rows = nch * CHUNK
    return pl.pallas_call(
        functools.partial(_ret_kernel, nch=nch),
        out_shape=jax.ShapeDtypeStruct((t, MIX_W), F32),
        grid=(t // rows,),
        in_specs=[pl.BlockSpec((rows, 512), lambda i: (i, 9)),
                  pl.BlockSpec((rows, 512), lambda i: (i, 10)),
                  pl.BlockSpec((rows, 512), lambda i: (i, 11)),
                  pl.BlockSpec((rows, 256), lambda i: (i, 0)),
                  pl.BlockSpec((rows, 256), lambda i: (i, 0)),
                  _layer_spec((PAR_ROWS, PAR_W), l)],
        out_specs=pl.BlockSpec((rows, MIX_W), lambda i: (i, 0)),
        scratch_shapes=[pltpu.VMEM((RET_HEADS // 2, 2 * RET_DK, 2 * RET_DV), F32)],
        compiler_params=_cparams(1),
    )(proj, proj, proj, cos, sin, par)


def _ssd_kernel(z_ref, xbc_ref, dt_ref, par_ref, btri_ref, eye_ref, expand_ref, o_ref, ext_ref, h_ref,
                *, nch):
    rows = nch * CHUNK

    @pl.when(pl.program_id(0) == 0)
    def _():
        h_ref[...] = jnp.zeros_like(h_ref)

    _stage_with_halo(ext_ref, (xbc_ref,), rows)
    xbc = _silu(_causal_conv(ext_ref, _par(par_ref, R_SSD_CONV, 1024, CONV_W), rows)
                + _par(par_ref, R_SSD_CB, 1024))
    _keep_halo(ext_ref, rows)
    x = xbc[:, 0:512]

    pair = _Pair()
    dt = _softplus(dt_ref[...] + _par(par_ref, R_SSD_DTB, 128))
    g_all = _dot_sel(btri_ref[...], dt * (-jnp.exp(_par(par_ref, R_SSD_ALOG, 128))))
    g_t = _dot_sel(g_all, eye_ref[...], TN)
    wide = _dot_sel(jnp.concatenate([g_all, dt], axis=0), expand_ref[...])
    g_w, dt_w = wide[:rows], wide[rows:]
    gtot_w = _rows_bcast(g_w, nch)
    xdt = x * dt_w
    xtail = xdt * jnp.exp(gtot_w - g_w)
    eg_w = jnp.exp(g_w)
    etot_w = jnp.exp(gtot_w)
    z = z_ref[...]
    d_w = _par(par_ref, R_SSD_D, MIX_W)
    nw = _par(par_ref, R_SSD_NW, MIX_W)
    gw = M2_HEADS // M2_GROUPS * M2_HEADDIM

    combos = [(c, gi) for c in range(nch) for gi in range(M2_GROUPS)]
    rsl = lambda c: slice(c * CHUNK, (c + 1) * CHUNK)
    bg = {(c, gi): xbc[rsl(c), 512 + gi * M2_STATE:512 + (gi + 1) * M2_STATE] for c, gi in combos}
    cg = {(c, gi): xbc[rsl(c), 768 + gi * M2_STATE:768 + (gi + 1) * M2_STATE] for c, gi in combos}
    cb2 = {k: _dot1(cg[k], jnp.concatenate([bg[k], bg[k]], axis=0), NT) for k in combos}
    upd = {(c, gi): _dot1(bg[c, gi], xtail[rsl(c), gi * gw:(gi + 1) * gw], TN) for c, gi in combos}
    y_in = {}
    for c, gi in combos:
        for pp in range(2):
            ha = gi * 4 + 2 * pp
            gcol = jnp.where(pair.lo, g_all[rsl(c), ha:ha + 1], g_all[rsl(c), ha + 1:ha + 2])
            grow = jnp.concatenate([g_t[ha:ha + 1, rsl(c)], g_t[ha + 1:ha + 2, rsl(c)]], axis=1)
            decay = jnp.exp(jnp.where(pair.causal, gcol - grow, -jnp.inf))
            y_in[c, gi, pp] = _dot1(cb2[c, gi] * decay,
                                    pair.bd(xdt[rsl(c), ha * M2_HEADDIM:(ha + 2) * M2_HEADDIM]))
    h_start = {}
    for gi in range(M2_GROUPS):
        h = h_ref[gi]
        for c in range(nch):
            h_start[c, gi] = h
            h = h * etot_w[c * CHUNK:c * CHUNK + 1, gi * gw:(gi + 1) * gw] + upd[c, gi]
        h_ref[gi] = h
    for c, gi in combos:
        rs, gs = rsl(c), slice(gi * gw, (gi + 1) * gw)
        y = jnp.concatenate([y_in[c, gi, 0], y_in[c, gi, 1]], axis=1)
        y = y + _dot1(cg[c, gi], h_start[c, gi]) * eg_w[rs, gs]
        y = (y + d_w[:, gs] * x[rs, gs]) * _silu(z[rs, gs])
        y = y * lax.rsqrt(jnp.mean(y * y, axis=-1, keepdims=True) + EPS) * nw[:, gs]
        o_ref[rs, gs] = y


def _head_expand_matrix(n_heads, width):
    r = jnp.arange(128)[:, None]
    c = jnp.arange(n_heads * width)[None, :]
    return (r == c // width).astype(BF16)


def _ssd(proj, par, l, nch):
    t = proj.shape[0]
    rows = nch * CHUNK
    return pl.pallas_call(
        functools.partial(_ssd_kernel, nch=nch),
        out_shape=jax.ShapeDtypeStruct((t, MIX_W), F32),
        grid=(t // rows,),
        in_specs=[pl.BlockSpec((rows, 512), lambda i: (i, 12)),
                  pl.BlockSpec((rows, 1024), lambda i: (i, 3)),
                  pl.BlockSpec((rows, 128), lambda i: (i, 53)),
                  _layer_spec((PAR_ROWS, PAR_W), l),
                  _const_spec((rows, rows)), _const_spec((rows, rows)), _const_spec((128, 512))],
        out_specs=pl.BlockSpec((rows, MIX_W), lambda i: (i, 0)),
        scratch_shapes=[pltpu.VMEM((rows + 8, 1024), F32),
                        pltpu.VMEM((M2_GROUPS, M2_STATE, 256), F32)],
        compiler_params=_cparams(1),
    )(proj, proj, proj, par,
      _block_diag_ones(rows, CHUNK, lower=True, dtype=BF16), jnp.eye(rows, dtype=BF16),
      _head_expand_matrix(M2_HEADS, M2_HEADDIM))


def _rwkv_kernel(rkv_ref, lora_ref, par_ref, lmat_ref, btri_ref, hblk_ref, o_ref, ext_ref, s_ref, *, nch):
    rows = nch * CHUNK

    @pl.when(pl.program_id(0) == 0)
    def _():
        s_ref[...] = jnp.zeros_like(s_ref)

    _stage_with_halo(ext_ref, (rkv_ref, lora_ref), rows)
    cur = ext_ref[pl.ds(8, rows), :]
    prev = ext_ref[pl.ds(7, rows), :]
    _keep_halo(ext_ref, rows)
    mixed = cur + (prev - cur) * _par(par_ref, R_RW_MU, PAR_W)
    r = mixed[:, 0:512]
    k = mixed[:, 512:1024]
    v = mixed[:, 1024:1536]
    lora = mixed[:, 1536:1792]

    w_raw = -_softplus(-(_par(par_ref, R_RW_W0, MIX_W) + _dot1(jnp.tanh(lora), lmat_ref[0:256, :]))) - 0.5
    log_d = -jnp.exp(w_raw)
    a = _sigmoid(_par(par_ref, R_RW_A0, MIX_W) + _dot1(lora, lmat_ref[256:512, :]))
    gate = _dot1(_sigmoid(lora), lmat_ref[512:768, :])
    hblk = hblk_ref[...]

    g_in = _dot_sel(btri_ref[...], log_d)
    g_tot = _rows_bcast(g_in, nch)
    e_in = jnp.exp(g_in)
    e_neg = jnp.exp(-g_in)
    e_ex = jnp.exp(g_in - log_d)
    e_tail = jnp.exp(g_tot - g_in)
    e_end = jnp.exp(g_tot)

    kk = k * _par(par_ref, R_RW_KK, MIX_W)
    kk = kk * lax.rsqrt(_sum_bcast(kk * kk, hblk) + EPS)
    k_mod = k * (1.0 + (a - 1.0) * _par(par_ref, R_RW_KA, MIX_W))
    a_vec = -(a * kk)
    r_t = r * e_in
    b_t = kk * e_ex
    k_t = k_mod * e_neg
    a_t = a_vec * e_neg
    k_c = k_mod * e_tail
    a_c = a_vec * e_tail

    pair = _Pair()
    npair = RW_HEADS // 2
    combos = [(c, j) for c in range(nch) for j in range(npair)]
    cut = lambda arr, cj: arr[cj[0] * CHUNK:(cj[0] + 1) * CHUNK, cj[1] * 2 * RW_N:(cj[1] + 1) * 2 * RW_N]
    lhs = {cj: jnp.concatenate([cut(b_t, cj), cut(r_t, cj)], axis=0) for cj in combos}
    x1 = {cj: _dot1(lhs[cj], pair.bd(cut(k_t, cj)), NT) for cj in combos}
    x2 = {cj: _dot1(lhs[cj], pair.bd(cut(a_t, cj)), NT) for cj in combos}
    a_ra = {cj: jnp.where(pair.causal, x2[cj][CHUNK:], 0.0) for cj in combos}
    tinv = dict(zip(combos, pair.inverse([-jnp.where(pair.strict, x2[cj][:CHUNK], 0.0) for cj in combos])))
    av = {cj: _dot1(jnp.concatenate([jnp.where(pair.strict, x1[cj][:CHUNK], 0.0),
                                     jnp.where(pair.causal, x1[cj][CHUNK:], 0.0)], axis=0),
                    pair.bd(cut(v, cj))) for cj in combos}
    tz = {cj: _dot1(tinv[cj], jnp.concatenate([pair.bd(cut(b_t, cj)), pair.bd(av[cj][:CHUNK])], axis=1))
          for cj in combos}
    az = {cj: _dot1(a_ra[cj], jnp.concatenate([pair.bd(tz[cj][:, :2 * RW_N]), pair.bd(tz[cj][:, 2 * RW_N:])],
                                              axis=1)) for cj in combos}
    r_eff = {cj: cut(r_t, cj) + az[cj][:, :2 * RW_N] for cj in combos}
    y0 = {cj: av[cj][CHUNK:] + az[cj][:, 2 * RW_N:] for cj in combos}
    p_low = {cj: jnp.where(pair.same_block, _dot1(tz[cj][:, :2 * RW_N], cut(a_c, cj), TN), 0.0) for cj in combos}
    q_mat = {cj: jnp.where(pair.same_block,
                           _dot1(jnp.concatenate([cut(v, cj), tz[cj][:, 2 * RW_N:]], axis=0),
                                 jnp.concatenate([cut(k_c, cj), cut(a_c, cj)], axis=0), TN), 0.0)
             for cj in combos}
    y_rows = []
    for c in range(nch):
        s0 = [s_ref[j] if c == 0 else s_new[j] for j in range(npair)]
        y_rows.append(jnp.concatenate(
            [_dot1(r_eff[c, j], s0[j], NT) + y0[c, j] for j in range(npair)], axis=1))
        s_new = [s0[j] * e_end[c * CHUNK:c * CHUNK + 1, j * 2 * RW_N:(j + 1) * 2 * RW_N]
                 + _dot1(s0[j], p_low[c, j]) + q_mat[c, j] for j in range(npair)]
    for j in range(npair):
        s_ref[j] = s_new[j]
    y = jnp.concatenate(y_rows, axis=0)

    inv_n = 1.0 / RW_N
    mu = _sum_bcast(y, hblk, pieces=2) * inv_n
    yc = y - mu
    var = _sum_bcast(yc * yc, hblk) * inv_n
    y = yc * lax.rsqrt(var + RWKV_LN_EPS) * _par(par_ref, R_RW_LNW, MIX_W) + _par(par_ref, R_RW_LNB, MIX_W)
    y = y + _sum_bcast(r * k_mod * _par(par_ref, R_RW_RK, MIX_W), hblk) * v
    o_ref[...] = y * gate


def _rwkv(proj, par, lmat, l, nch):
    t = proj.shape[0]
    rows = nch * CHUNK
    return pl.pallas_call(
        functools.partial(_rwkv_kernel, nch=nch),
        out_shape=jax.ShapeDtypeStruct((t, MIX_W), F32),
        grid=(t // rows,),
        in_specs=[pl.BlockSpec((rows, 1536), lambda i: (i, 1)),
                  pl.BlockSpec((rows, 256), lambda i: (i, 27)),
                  _layer_spec((PAR_ROWS, PAR_W), l), _layer_spec((768, MIX_W), l),
                  _const_spec((rows, rows)), _const_spec((512, 512))],
        out_specs=pl.BlockSpec((rows, MIX_W), lambda i: (i, 0)),
        scratch_shapes=[pltpu.VMEM((rows + 8, 1792), F32),
                        pltpu.VMEM((RW_HEADS // 2, 2 * RW_N, 2 * RW_N), F32)],
        compiler_params=_cparams(1),
    )(proj, proj, par, lmat,
      _block_diag_ones(rows, CHUNK, lower=True, dtype=BF16), _block_diag_ones(MIX_W, RW_N, dtype=BF16))


def _layout_w_in_kernel(w_hbm, o_ref, buf, sem, *, tr, blocks_per_layer):
    i = pl.program_id(0)

    def fetch(step, slot):
        src = w_hbm.at[step // blocks_per_layer, pl.ds((step % blocks_per_layer) * tr, tr), :]
        return pltpu.make_async_copy(src, buf.at[slot], sem.at[slot])

    @pl.when(i == 0)
    def _():
        fetch(0, 0).start()

    slot = i % 2

    @pl.when(i + 1 < pl.num_programs(0))
    def _():
        fetch(i + 1, 1 - slot).start()

    fetch(i, slot).wait()
    x = buf[slot]
    col = 0
    for s, n, p in _SRC_PIECES:
        o_ref[:, col:col + n] = x[:, s:s + n].astype(BF16)
        if p > n:
            o_ref[:, col + n:col + p] = jnp.zeros((x.shape[0], p - n), BF16)
        col += p


def _layout_w_in(w, tr=256):
    depth, d, n = w.shape
    out = pl.pallas_call(
        functools.partial(_layout_w_in_kernel, tr=tr, blocks_per_layer=d // tr),
        out_shape=jax.ShapeDtypeStruct((depth * d, P_PAD), BF16),
        grid=(depth * d // tr,),
        in_specs=[pl.BlockSpec(memory_space=pl.ANY)],
        out_specs=pl.BlockSpec((tr, P_PAD), lambda i: (i, 0)),
        scratch_shapes=[pltpu.VMEM((2, tr, n), F32), pltpu.SemaphoreType.DMA((2,))],
        compiler_params=_cparams(1),
    )(w)
    return out.reshape(depth, d, P_PAD)


def _param_slab(p):
    depth = p['w_in'].shape[0]

    def rows(a, lane=0):
        a = a.astype(F32).reshape(depth, -1, a.shape[-1])
        return jnp.pad(a, ((0, 0), (0, 0), (lane, PAR_W - lane - a.shape[-1])))

    pieces = [rows(p['gdn_conv_w']), rows(p['gdn_a_log'], 4), rows(p['gdn_dt_bias'], 4),
              rows(p['gdn_norm_w']), rows(p['ret_norm_w']),
              rows(p['m2_conv_w']), rows(p['m2_conv_b']), rows(p['m2_a_log']), rows(p['m2_dt_bias']),
              rows(jnp.repeat(p['m2_d'], M2_HEADDIM, axis=1)), rows(p['m2_norm_w']),
              rows(p['rw_mu']), rows(p['rw_w0']), rows(p['rw_a0']), rows(p['rw_k_k']), rows(p['rw_k_a']),
              rows(p['rw_r_k'].reshape(depth, -1)), rows(p['rw_ln_w']), rows(p['rw_ln_b'])]
    slab = jnp.concatenate(pieces, axis=1)
    return jnp.pad(slab, ((0, 0), (0, PAR_ROWS - slab.shape[1]), (0, 0)))


def _lora_mats(p):
    def at(m, offset):
        return jnp.pad(m, ((0, 0), (offset, 256 - offset - m.shape[1]), (0, 0)))
    return jnp.concatenate([at(p['rw_w_up'], 0), at(p['rw_a_up'], RW_W_LORA),
                            at(p['rw_g_up'], RW_W_LORA + RW_A_LORA)], axis=1).astype(BF16)


def _rotary_tables(t):
    theta = 1.0 / (ROPE_BASE ** jnp.linspace(0.0, 1.0, RET_DK // 2, dtype=F32))
    ang = jnp.arange(t, dtype=F32)[:, None] * theta
    cos = jnp.repeat(jnp.cos(ang), 2, axis=1)
    sin = jnp.stack([-jnp.sin(ang), jnp.sin(ang)], axis=-1).reshape(t, RET_DK)
    return jnp.tile(cos, (1, RET_HEADS)), jnp.tile(sin, (1, RET_HEADS))


NCH_GDN, NCH_RET, NCH_SSD, NCH_RWKV = 4, 4, 4, 4


def _prepare(p, t):
    cos, sin = _rotary_tables(t)
    return dict(norm1=p['norm1_w'].astype(F32)[:, None, :], w_in=_layout_w_in(p['w_in']),
                par=_param_slab(p), lmat=_lora_mats(p), w_out=p['w_out'].astype(BF16), cos=cos, sin=sin)


def _token_mix(h, l, q):
    t = h.shape[0]
    nch = lambda n: min(n, t // CHUNK)
    proj = _norm_matmul(h, q['norm1'], q['w_in'], l)
    o_a = _gdn(proj, q['par'], l, nch(NCH_GDN))
    o_b = _retention(proj, q['cos'], q['sin'], q['par'], l, nch(NCH_RET))
    o_c = _ssd(proj, q['par'], l, nch(NCH_SSD))
    o_d = _rwkv(proj, q['par'], q['lmat'], l, nch(NCH_RWKV))
    return _out_proj(h, (o_a, o_b, o_c, o_d), q['w_out'], l)


def kernel(x, norm1_w, w_in, gdn_conv_w, gdn_a_log, gdn_dt_bias, gdn_norm_w, ret_norm_w, m2_conv_w, m2_conv_b, m2_a_log, m2_dt_bias, m2_d, m2_norm_w, rw_mu, rw_w0, rw_w_up, rw_a0, rw_a_up, rw_g_up, rw_k_k, rw_k_a, rw_r_k, rw_ln_w, rw_ln_b, w_out, norm2_w, w_ffn_up, w_ffn_down, final_norm_w):
    p = dict(norm1_w=norm1_w, w_in=w_in, gdn_conv_w=gdn_conv_w, gdn_a_log=gdn_a_log,
             gdn_dt_bias=gdn_dt_bias, gdn_norm_w=gdn_norm_w, ret_norm_w=ret_norm_w,
             m2_conv_w=m2_conv_w, m2_conv_b=m2_conv_b, m2_a_log=m2_a_log, m2_dt_bias=m2_dt_bias,
             m2_d=m2_d, m2_norm_w=m2_norm_w, rw_mu=rw_mu, rw_w0=rw_w0, rw_w_up=rw_w_up,
             rw_a0=rw_a0, rw_a_up=rw_a_up, rw_g_up=rw_g_up, rw_k_k=rw_k_k, rw_k_a=rw_k_a,
             rw_r_k=rw_r_k, rw_ln_w=rw_ln_w, rw_ln_b=rw_ln_b, w_out=w_out)
    bsz, t, d = x.shape
    depth = w_in.shape[0]
    q = _prepare(p, t)
    norm2 = norm2_w.astype(F32)[:, None, :]
    w_up = w_ffn_up.astype(BF16)
    w_down = w_ffn_down.astype(BF16)
    final_w = final_norm_w.astype(F32).reshape(1, d)
    outs = []
    for b in range(bsz):
        h = x[b]
        for l in range(depth):
            h = _token_mix(h, l, q)
            h = _ffn(h, norm2, w_up, w_down, final_w, l, final_norm=(l == depth - 1))
        outs.append(h)
    return outs[0].reshape(1, t, d) if bsz == 1 else jnp.stack(outs, axis=0)
```

```python
import functools
import math

import jax
import jax.numpy as jnp
from jax import lax
from jax.experimental import pallas as pl
from jax.experimental.pallas import tpu as pltpu

F32 = jnp.float32
BF16 = jnp.bfloat16

D_MODEL = 2048
D_FF = 4 * D_MODEL
CONV_W = 4
CHUNK = 64
EPS = 1e-6
ROPE_BASE = 10000.0
RWKV_LN_EPS = 64e-5

GDN_HEADS, GDN_DK, GDN_DV = 4, 128, 128
RET_HEADS, RET_DK, RET_DV = 4, 64, 128
M2_HEADS, M2_HEADDIM, M2_GROUPS, M2_STATE = 8, 64, 2, 128
RW_HEADS, RW_N = 8, 64
RW_W_LORA, RW_A_LORA, RW_G_LORA = 32, 32, 96
MIX_W = 512

_GDN0, _RET0, _M20, _RW0 = 0, 2056, 3592, 5136

P_PAD = 7168
_SRC_PIECES = (
    (_GDN0, 1536, 1536),
    (_RW0, 1536, 1536),
    (_M20 + 512, 1024, 1024),
    (_GDN0 + 1536, 512, 512),
    (_RET0, 512, 512),
    (_RET0 + 512, 512, 512),
    (_RET0 + 1024, 512, 512),
    (_M20, 512, 512),
    (_GDN0 + 2048, 8, 128),
    (_M20 + 1536, 8, 128),
    (_RW0 + 1536, 160, 256),
)

VMEM_LIMIT = 56 * 1024 * 1024

NN = (((1,), (0,)), ((), ()))
NT = (((1,), (1,)), ((), ()))
TN = (((0,), (0,)), ((), ()))


def _cparams(n_axes):
    return pltpu.CompilerParams(dimension_semantics=("arbitrary",) * n_axes,
                                vmem_limit_bytes=VMEM_LIMIT)


def _dg(a, b, dims=NN, prec=None):
    return lax.dot_general(a, b, dims, preferred_element_type=F32, precision=prec)


def _split(a):
    hi = a.astype(BF16)
    lo = (a - hi.astype(F32)).astype(BF16)
    return hi, lo


def _dot1(a, b, dims=NN):
    return _dg(a.astype(BF16), b.astype(BF16), dims)


def _split3(a):
    hi = a.astype(BF16)
    r1 = a - hi.astype(F32)
    mid = r1.astype(BF16)
    lo = (r1 - mid.astype(F32)).astype(BF16)
    return hi, mid, lo


def _dot_sel(a, b, dims=NN):
    if a.dtype == BF16:
        return sum(_dg(a, piece, dims) for piece in _split3(b))
    return sum(_dg(piece, b, dims) for piece in _split3(a))


def _sum_bcast(x, blk, pieces=1):
    outs = []
    for s in range(0, x.shape[1], 256):
        xs, bs = x[:, s:s + 256], blk[s:s + 256, s:s + 256]
        if pieces == 1:
            outs.append(_dg(xs.astype(BF16), bs))
        else:
            xh, xl = _split(xs)
            outs.append(_dg(xh, bs) + _dg(xl, bs))
    return jnp.concatenate(outs, axis=1)


def _rows_bcast(x, nch):
    return jnp.concatenate(
        [jnp.broadcast_to(x[(c + 1) * CHUNK - 1:(c + 1) * CHUNK, :], (CHUNK, x.shape[1])) for c in range(nch)],
        axis=0)


def _sigmoid(x):
    return 1.0 / (1.0 + jnp.exp(-x))


def _silu(x):
    return x * _sigmoid(x)


def _softplus(x):
    return jnp.maximum(x, 0.0) + jnp.log1p(jnp.exp(-jnp.abs(x)))


def _iota2(shape, axis):
    return lax.broadcasted_iota(jnp.int32, shape, axis)


def _chunk_masks():
    r = _iota2((CHUNK, CHUNK), 0)
    c = _iota2((CHUNK, CHUNK), 1)
    return r >= c, r > c, r == c


class _Pair:
    def __init__(self):
        lane = _iota2((CHUNK, 2 * CHUNK), 1)
        row = _iota2((CHUNK, 2 * CHUNK), 0)
        self.lo = lane < CHUNK
        col = jnp.where(self.lo, lane, lane - CHUNK)
        self.dist = (row - col).astype(F32)
        self.causal = row >= col
        self.strict = row > col
        self.eye = jnp.where(row == col, 1.0, 0.0).astype(F32)
        r2 = _iota2((2 * CHUNK, 2 * CHUNK), 0)
        c2 = _iota2((2 * CHUNK, 2 * CHUNK), 1)
        self.same_block = (r2 < CHUNK) == (c2 < CHUNK)

    def bd(self, x):
        z = jnp.zeros_like(x)
        return jnp.concatenate([jnp.where(self.lo, x, z), jnp.where(self.lo, z, x)], axis=0)

    def inverse(self, lows):
        ps = [self.eye - low for low in lows]
        curs = list(lows)
        n = 1
        while n < CHUNK:
            rhss = [self.bd(cur) for cur in curs]
            if n == 1:
                curs = [_dot1(cur, rhs) for cur, rhs in zip(curs, rhss)]
            elif 2 * n < CHUNK:
                outs = [_dot1(jnp.concatenate([cur, p], axis=0), rhs)
                        for cur, p, rhs in zip(curs, ps, rhss)]
                curs = [out[:CHUNK] for out in outs]
                ps = [p + out[CHUNK:] for p, out in zip(ps, outs)]
            else:
                ps = [p + _dot1(p, rhs) for p, rhs in zip(ps, rhss)]
            n *= 2
        return ps


def _norm_matmul_kernel(x_ref, nw_ref, w_ref, o_ref, a_ref):
    @pl.when(pl.program_id(1) == 0)
    def _():
        x = x_ref[...]
        y = x * lax.rsqrt(jnp.mean(x * x, axis=-1, keepdims=True) + EPS) * nw_ref[...]
        a_ref[...] = y.astype(BF16)

    o_ref[...] = jnp.dot(a_ref[...], w_ref[...], preferred_element_type=F32)


def _norm_matmul(x, nw, w, l, tm=1024, tn=1792):
    t, d = x.shape
    n = w.shape[2]
    tm = min(tm, t)
    return pl.pallas_call(
        _norm_matmul_kernel,
        out_shape=jax.ShapeDtypeStruct((t, n), F32),
        grid=(t // tm, n // tn),
        in_specs=[pl.BlockSpec((tm, d), lambda i, j: (i, 0)),
                  pl.BlockSpec((None, 1, d), lambda i, j: (l, 0, 0)),
                  pl.BlockSpec((None, d, tn), lambda i, j: (l, 0, j))],
        out_specs=pl.BlockSpec((tm, tn), lambda i, j: (i, j)),
        scratch_shapes=[pltpu.VMEM((tm, d), BF16)],
        compiler_params=_cparams(2),
    )(x, nw, w)


def _out_proj_kernel(h_ref, oa_ref, ob_ref, oc_ref, od_ref, w_ref, o_ref):
    acc = h_ref[...]
    for idx, part in enumerate((oa_ref, ob_ref, oc_ref, od_ref)):
        acc = acc + jnp.dot(part[...].astype(BF16), w_ref[idx * MIX_W:(idx + 1) * MIX_W, :],
                            preferred_element_type=F32)
    o_ref[...] = acc


def _out_proj(h, parts, w, l, tm=512):
    t, d = h.shape
    tm = min(tm, t)
    part_spec = pl.BlockSpec((tm, MIX_W), lambda i: (i, 0))
    return pl.pallas_call(
        _out_proj_kernel,
        out_shape=jax.ShapeDtypeStruct((t, d), F32),
        grid=(t // tm,),
        in_specs=[pl.BlockSpec((tm, d), lambda i: (i, 0)),
                  part_spec, part_spec, part_spec, part_spec,
                  pl.BlockSpec((None, 4 * MIX_W, d), lambda i: (l, 0, 0))],
        out_specs=pl.BlockSpec((tm, d), lambda i: (i, 0)),
        compiler_params=_cparams(1),
    )(h, *parts, w)


def _ffn_kernel(h_ref, nw_ref, wu_ref, wd_ref, fw_ref, o_ref, a_ref, *, final_norm):
    f = pl.program_id(1)

    @pl.when(f == 0)
    def _():
        x = h_ref[...]
        y = x * lax.rsqrt(jnp.mean(x * x, axis=-1, keepdims=True) + EPS) * nw_ref[...]
        a_ref[...] = y.astype(BF16)
        o_ref[...] = x

    u = jnp.dot(a_ref[...], wu_ref[...], preferred_element_type=F32)
    s = jnp.square(jnp.maximum(u, 0.0)).astype(BF16)
    o_ref[...] += jnp.dot(s, wd_ref[...], preferred_element_type=F32)

    if final_norm:
        @pl.when(f == pl.num_programs(1) - 1)
        def _():
            y = o_ref[...]
            o_ref[...] = y * lax.rsqrt(jnp.mean(y * y, axis=-1, keepdims=True) + EPS) * fw_ref[...]


def _ffn(h, nw, wu, wd, fw, l, final_norm, tm=1024, tf=512):
    t, d = h.shape
    ff = wu.shape[2]
    tm = min(tm, t)
    return pl.pallas_call(
        functools.partial(_ffn_kernel, final_norm=final_norm),
        out_shape=jax.ShapeDtypeStruct((t, d), F32),
        grid=(t // tm, ff // tf),
        in_specs=[pl.BlockSpec((tm, d), lambda i, f: (i, 0)),
                  pl.BlockSpec((None, 1, d), lambda i, f: (l, 0, 0)),
                  pl.BlockSpec((None, d, tf), lambda i, f: (l, 0, f)),
                  pl.BlockSpec((None, tf, d), lambda i, f: (l, f, 0)),
                  pl.BlockSpec((1, d), lambda i, f: (0, 0))],
        out_specs=pl.BlockSpec((tm, d), lambda i, f: (i, 0)),
        scratch_shapes=[pltpu.VMEM((tm, d), BF16)],
        compiler_params=_cparams(2),
    )(h, nw, wu, wd, fw)


def _stage_with_halo(ext_ref, pieces, rows):
    @pl.when(pl.program_id(0) == 0)
    def _():
        ext_ref[0:8, :] = jnp.zeros((8, ext_ref.shape[1]), F32)

    col = 0
    for ref in pieces:
        w = ref.shape[1]
        ext_ref[8:8 + rows, col:col + w] = ref[...]
        col += w


def _keep_halo(ext_ref, rows):
    ext_ref[0:8, :] = ext_ref[rows:rows + 8, :]


def _causal_conv(ext_ref, cw, rows):
    acc = ext_ref[pl.ds(8, rows), :] * cw[3:4, :]
    for i in range(CONV_W - 1):
        acc = acc + ext_ref[pl.ds(5 + i, rows), :] * cw[i:i + 1, :]
    return acc


def _const_spec(shape):
    return pl.BlockSpec(shape, lambda i: (0,) * len(shape))


PAR_ROWS, PAR_W = 32, 1792
(R_GDN_CONV, R_GDN_ALOG, R_GDN_DTB, R_GDN_NW, R_RET_NW, R_SSD_CONV, R_SSD_CB, R_SSD_ALOG, R_SSD_DTB,
 R_SSD_D, R_SSD_NW, R_RW_MU, R_RW_W0, R_RW_A0, R_RW_KK, R_RW_KA, R_RW_RK, R_RW_LNW, R_RW_LNB) = (
    0, 4, 5, 6, 7, 8, 12, 13, 14, 15, 16, 17, 18, 19, 20, 21, 22, 23, 24)


def _layer_spec(shape, l):
    return pl.BlockSpec((None,) + tuple(shape), lambda i: (l,) + (0,) * len(shape))


def _par(par_ref, row, width, nrows=1):
    return par_ref[row:row + nrows, 0:width]


def _block_diag_ones(n, block, lower=False, dtype=F32):
    r = jnp.arange(n)[:, None]
    c = jnp.arange(n)[None, :]
    m = (r // block) == (c // block)
    if lower:
        m = m & (r >= c)
    return m.astype(dtype)


def _gdn_kernel(qkv_ref, z_ref, gate_ref, par_ref, btri_ref, eye_ref, o_ref, ext_ref, s_ref, *, nch):
    rows = nch * CHUNK

    @pl.when(pl.program_id(0) == 0)
    def _():
        s_ref[...] = jnp.zeros_like(s_ref)

    _stage_with_halo(ext_ref, (qkv_ref,), rows)
    qkv = _silu(_causal_conv(ext_ref, _par(par_ref, R_GDN_CONV, 1536, CONV_W), rows))
    _keep_halo(ext_ref, rows)

    causal, strict, _ = _chunk_masks()
    pair = _Pair()

    gt = gate_ref[...]
    beta = _sigmoid(gt)
    log_a = -jnp.exp(_par(par_ref, R_GDN_ALOG, 128)) * _softplus(gt + _par(par_ref, R_GDN_DTB, 128))
    g_all = _dot_sel(btri_ref[...], log_a)
    g_t = _dot_sel(g_all, eye_ref[...], TN)
    z = z_ref[...]
    nw = _par(par_ref, R_GDN_NW, GDN_DV)

    qs, ks, vs = [], [], []
    for h in range(GDN_HEADS):
        qh = qkv[:, h * GDN_DK:(h + 1) * GDN_DK]
        kh = qkv[:, 512 + h * GDN_DK:512 + (h + 1) * GDN_DK]
        qs.append(qh * lax.rsqrt(jnp.sum(qh * qh, axis=-1, keepdims=True) + EPS) * (GDN_DK ** -0.5))
        ks.append(kh * lax.rsqrt(jnp.sum(kh * kh, axis=-1, keepdims=True) + EPS))
        vs.append(qkv[:, 1024 + h * GDN_DV:1024 + (h + 1) * GDN_DV])

    combos = [(c, h) for c in range(nch) for h in range(GDN_HEADS)]
    rsl = lambda c: slice(c * CHUNK, (c + 1) * CHUNK)
    bcol = {(c, h): beta[rsl(c), h:h + 1] for c, h in combos}
    gcol = {(c, h): g_all[rsl(c), 4 + h:5 + h] for c, h in combos}
    kq = {(c, h): _dot1(jnp.concatenate([ks[h][rsl(c)], qs[h][rsl(c)]], axis=0), ks[h][rsl(c)], NT)
          for c, h in combos}
    lows, qks = {}, {}
    for c, h in combos:
        grow = g_t[4 + h:5 + h, c * CHUNK:(c + 1) * CHUNK]
        gamma = jnp.exp(jnp.where(causal, gcol[c, h] - grow, -jnp.inf))
        lows[c, h] = jnp.where(strict, kq[c, h][:CHUNK] * gamma * bcol[c, h], 0.0)
        qks[c, h] = kq[c, h][CHUNK:] * gamma
    pairs = [(c, p) for c in range(nch) for p in range(GDN_HEADS // 2)]
    tps = pair.inverse([jnp.concatenate([lows[c, 2 * p], lows[c, 2 * p + 1]], axis=1) for c, p in pairs])
    tinv = {}
    for (c, p), tp in zip(pairs, tps):
        tinv[c, 2 * p] = tp[:, :CHUNK]
        tinv[c, 2 * p + 1] = tp[:, CHUNK:]
    egc = {ch: jnp.exp(gcol[ch]) for ch in combos}
    glast = {ch: gcol[ch][CHUNK - 1:CHUNK, :] for ch in combos}
    wu = {(c, h): _dot1(tinv[c, h], jnp.concatenate(
        [ks[h][rsl(c)] * (bcol[c, h] * egc[c, h]), vs[h][rsl(c)] * bcol[c, h]], axis=1)) for c, h in combos}
    qwu = {ch: _dot1(qks[ch], wu[ch]) for ch in combos}
    kwu = {(c, h): _dot1(ks[h][rsl(c)] * jnp.exp(glast[c, h] - gcol[c, h]), wu[c, h], TN) for c, h in combos}
    for c in range(nch):
        rs = rsl(c)
        s0 = [s_ref[h] if c == 0 else s_new[h] for h in range(GDN_HEADS)]
        s_new = []
        for h in range(GDN_HEADS):
            sl = slice(h * GDN_DV, (h + 1) * GDN_DV)
            q_eff = qs[h][rs] * egc[c, h] - qwu[c, h][:, :GDN_DK]
            o = _dot1(q_eff, s0[h]) + qwu[c, h][:, GDN_DK:]
            s_new.append(s0[h] * jnp.exp(glast[c, h]) - _dot1(kwu[c, h][:, :GDN_DK], s0[h])
                         + kwu[c, h][:, GDN_DK:])
            o = o * lax.rsqrt(jnp.mean(o * o, axis=-1, keepdims=True) + EPS) * nw
            o_ref[rs, sl] = o * _silu(z[rs, sl])
    for h in range(GDN_HEADS):
        s_ref[h] = s_new[h]


def _gdn(proj, par, l, nch):
    t = proj.shape[0]
    rows = nch * CHUNK
    return pl.pallas_call(
        functools.partial(_gdn_kernel, nch=nch),
        out_shape=jax.ShapeDtypeStruct((t, MIX_W), F32),
        grid=(t // rows,),
        in_specs=[pl.BlockSpec((rows, 1536), lambda i: (i, 0)),
                  pl.BlockSpec((rows, 512), lambda i: (i, 8)),
                  pl.BlockSpec((rows, 128), lambda i: (i, 52)),
                  _layer_spec((PAR_ROWS, PAR_W), l),
                  _const_spec((rows, rows)), _const_spec((rows, rows))],
        out_specs=pl.BlockSpec((rows, MIX_W), lambda i: (i, 0)),
        scratch_shapes=[pltpu.VMEM((rows + 8, 1536), F32),
                        pltpu.VMEM((GDN_HEADS, GDN_DK, GDN_DV), F32)],
        compiler_params=_cparams(1),
    )(proj, proj, proj, par, _block_diag_ones(rows, CHUNK, lower=True, dtype=BF16),
      jnp.eye(rows, dtype=BF16))


def _ret_kernel(qk_ref, v_ref, g_ref, cos_ref, sin_ref, par_ref, o_ref, r_ref, *, nch):
    @pl.when(pl.program_id(0) == 0)
    def _():
        r_ref[...] = jnp.zeros_like(r_ref)

    rows = nch * CHUNK
    qk = qk_ref[...]
    cos = cos_ref[...]
    sin = sin_ref[...]
    even = (_iota2((rows, 256), 1) % 2) == 0

    def rot(x):
        partner = jnp.where(even, pltpu.roll(x, 255, 1), pltpu.roll(x, 1, 1))
        return x * cos + partner * sin

    q = rot(qk[:, 0:256])
    k = rot(qk[:, 256:512]) * (RET_DK ** -0.5)
    v = v_ref[...]
    gate = g_ref[...]
    nw = _par(par_ref, R_RET_NW, MIX_W)

    pair = _Pair()
    pos = _iota2((CHUNK, 2 * CHUNK), 0).astype(F32)
    zeros_v = jnp.zeros((CHUNK, RET_DV), F32)
    r2 = _iota2((2 * RET_DK, 2 * RET_DV), 0)
    c2 = _iota2((2 * RET_DK, 2 * RET_DV), 1)
    state_block = (r2 < RET_DK) == (c2 < RET_DV)

    for p in range(RET_HEADS // 2):
        lg0 = math.log1p(-(2.0 ** (-5.0 - 2 * p)))
        lg1 = math.log1p(-(2.0 ** (-5.0 - (2 * p + 1))))
        lg = jnp.where(pair.lo, lg0, lg1)
        dmask = jnp.exp(jnp.where(pair.causal, pair.dist * lg, -jnp.inf))
        k_dec = jnp.exp((CHUNK - 1.0 - pos) * lg)
        q_dec = jnp.exp((pos + 1.0) * lg)
        lg_v = jnp.where(_iota2((1, 2 * RET_DV), 1) < RET_DV, lg0, lg1)
        chunk_decay = jnp.exp(CHUNK * lg_v)
        ls = slice(p * 2 * RET_DK, (p + 1) * 2 * RET_DK)
        vsl = slice(p * 2 * RET_DV, (p + 1) * 2 * RET_DV)
        rsl = lambda c: slice(c * CHUNK, (c + 1) * CHUNK)
        sc = [_dot1(q[rsl(c), ls], pair.bd(k[rsl(c), ls]), NT) * dmask for c in range(nch)]
        upd = [_dot1(k[rsl(c), ls] * k_dec, v[rsl(c), vsl], TN) for c in range(nch)]
        r_start = []
        r_cur = r_ref[p]
        for c in range(nch):
            r_start.append(r_cur)
            r_cur = r_cur * chunk_decay + jnp.where(state_block, upd[c], 0.0)
        r_ref[p] = r_cur
        for c in range(nch):
            rs = rsl(c)
            vp = v[rs, vsl]
            v_bd = jnp.concatenate(
                [jnp.concatenate([vp[:, :RET_DV], zeros_v], axis=1),
                 jnp.concatenate([zeros_v, vp[:, RET_DV:]], axis=1)], axis=0)
            o = _dot1(sc[c], v_bd) + _dot1(q[rs, ls] * q_dec, r_start[c])
            for hh in range(2):
                sl = slice(vsl.start + hh * RET_DV, vsl.start + (hh + 1) * RET_DV)
                oh = o[:, hh * RET_DV:(hh + 1) * RET_DV]
                mu = jnp.mean(oh, axis=-1, keepdims=True)
                oc = oh - mu
                oh = oc * lax.rsqrt(jnp.mean(oc * oc, axis=-1, keepdims=True) + EPS) * nw[:, sl]
                o_ref[rs, sl] = oh * _silu(gate[rs, sl])


def _retention(proj, cos, sin, par, l, nch):
    t = proj.shape[0]
    rows = nch * CHUNK
    return pl.pallas_call(
        functools.partial(_ret_kernel, nch=nch),
        out_shape=jax.ShapeDtypeStruct((t, MIX_W), F32),
        grid=(t // rows,),
        in_specs=[pl.BlockSpec((rows, 512), lambda i: (i, 9)),
                  pl.BlockSpec((rows, 512), lambda i: (i, 10)),
                  pl.BlockSpec((rows, 512), lambda i: (i, 11)),
                  pl.BlockSpec((rows, 256), lambda i: (i, 0)),
                  pl.BlockSpec((rows, 256), lambda i: (i, 0)),
                  _layer_spec((PAR_ROWS, PAR_W), l)],
        out_specs=pl.BlockSpec((rows, MIX_W), lambda i: (i, 0)),
        scratch_shapes=[pltpu.VMEM((RET_HEADS // 2, 2 * RET_DK, 2 * RET_DV), F32)],
        compiler_params=_cparams(1),
    )(proj, proj, proj, cos, sin, par)


def _ssd_kernel(z_ref, xbc_ref, dt_ref, par_ref, btri_ref, eye_ref, expand_ref, o_ref, ext_ref, h_ref,
                *, nch):
    rows = nch * CHUNK

    @pl.when(pl.program_id(0) == 0)
    def _():
        h_ref[...] = jnp.zeros_like(h_ref)

    _stage_with_halo(ext_ref, (xbc_ref,), rows)
    xbc = _silu(_causal_conv(ext_ref, _par(par_ref, R_SSD_CONV, 1024, CONV_W), rows)
                + _par(par_ref, R_SSD_CB, 1024))
    _keep_halo(ext_ref, rows)
    x = xbc[:, 0:512]

    pair = _Pair()
    dt = _softplus(dt_ref[...] + _par(par_ref, R_SSD_DTB, 128))
    g_all = _dot_sel(btri_ref[...], dt * (-jnp.exp(_par(par_ref, R_SSD_ALOG, 128))))
    g_t = _dot_sel(g_all, eye_ref[...], TN)
    wide = _dot_sel(jnp.concatenate([g_all, dt], axis=0), expand_ref[...])
    g_w, dt_w = wide[:rows], wide[rows:]
    gtot_w = _rows_bcast(g_w, nch)
    xdt = x * dt_w
    xtail = xdt * jnp.exp(gtot_w - g_w)
    eg_w = jnp.exp(g_w)
    etot_w = jnp.exp(gtot_w)
    z = z_ref[...]
    d_w = _par(par_ref, R_SSD_D, MIX_W)
    nw = _par(par_ref, R_SSD_NW, MIX_W)
    gw = M2_HEADS // M2_GROUPS * M2_HEADDIM

    combos = [(c, gi) for c in range(nch) for gi in range(M2_GROUPS)]
    rsl = lambda c: slice(c * CHUNK, (c + 1) * CHUNK)
    bg = {(c, gi): xbc[rsl(c), 512 + gi * M2_STATE:512 + (gi + 1) * M2_STATE] for c, gi in combos}
    cg = {(c, gi): xbc[rsl(c), 768 + gi * M2_STATE:768 + (gi + 1) * M2_STATE] for c, gi in combos}
    cb2 = {k: _dot1(cg[k], jnp.concatenate([bg[k], bg[k]], axis=0), NT) for k in combos}
    upd = {(c, gi): _dot1(bg[c, gi], xtail[rsl(c), gi * gw:(gi + 1) * gw], TN) for c, gi in combos}
    y_in = {}
    for c, gi in combos:
        for pp in range(2):
            ha = gi * 4 + 2 * pp
            gcol = jnp.where(pair.lo, g_all[rsl(c), ha:ha + 1], g_all[rsl(c), ha + 1:ha + 2])
            grow = jnp.concatenate([g_t[ha:ha + 1, rsl(c)], g_t[ha + 1:ha + 2, rsl(c)]], axis=1)
            decay = jnp.exp(jnp.where(pair.causal, gcol - grow, -jnp.inf))
            y_in[c, gi, pp] = _dot1(cb2[c, gi] * decay,
                                    pair.bd(xdt[rsl(c), ha * M2_HEADDIM:(ha + 2) * M2_HEADDIM]))
    h_start = {}
    for gi in range(M2_GROUPS):
        h = h_ref[gi]
        for c in range(nch):
            h_start[c, gi] = h
            h = h * etot_w[c * CHUNK:c * CHUNK + 1, gi * gw:(gi + 1) * gw] + upd[c, gi]
        h_ref[gi] = h
    for c, gi in combos:
        rs, gs = rsl(c), slice(gi * gw, (gi + 1) * gw)
        y = jnp.concatenate([y_in[c, gi, 0], y_in[c, gi, 1]], axis=1)
        y = y + _dot1(cg[c, gi], h_start[c, gi]) * eg_w[rs, gs]
        y = (y + d_w[:, gs] * x[rs, gs]) * _silu(z[rs, gs])
        y = y * lax.rsqrt(jnp.mean(y * y, axis=-1, keepdims=True) + EPS) * nw[:, gs]
        o_ref[rs, gs] = y


def _head_expand_matrix(n_heads, width):
    r = jnp.arange(128)[:, None]
    c = jnp.arange(n_heads * width)[None, :]
    return (r == c // width).astype(BF16)


def _ssd(proj, par, l, nch):
    t = proj.shape[0]
    rows = nch * CHUNK
    return pl.pallas_call(
        functools.partial(_ssd_kernel, nch=nch),
        out_shape=jax.ShapeDtypeStruct((t, MIX_W), F32),
        grid=(t // rows,),
        in_specs=[pl.BlockSpec((rows, 512), lambda i: (i, 12)),
                  pl.BlockSpec((rows, 1024), lambda i: (i, 3)),
                  pl.BlockSpec((rows, 128), lambda i: (i, 53)),
                  _layer_spec((PAR_ROWS, PAR_W), l),
                  _const_spec((rows, rows)), _const_spec((rows, rows)), _const_spec((128, 512))],
        out_specs=pl.BlockSpec((rows, MIX_W), lambda i: (i, 0)),
        scratch_shapes=[pltpu.VMEM((rows + 8, 1024), F32),
                        pltpu.VMEM((M2_GROUPS, M2_STATE, 256), F32)],
        compiler_params=_cparams(1),
    )(proj, proj, proj, par,
      _block_diag_ones(rows, CHUNK, lower=True, dtype=BF16), jnp.eye(rows, dtype=BF16),
      _head_expand_matrix(M2_HEADS, M2_HEADDIM))


def _rwkv_kernel(rkv_ref, lora_ref, par_ref, lmat_ref, btri_ref, hblk_ref, o_ref, ext_ref, s_ref, *, nch):
    rows = nch * CHUNK

    @pl.when(pl.program_id(0) == 0)
    def _():
        s_ref[...] = jnp.zeros_like(s_ref)

    _stage_with_halo(ext_ref, (rkv_ref, lora_ref), rows)
    cur = ext_ref[pl.ds(8, rows), :]
    prev = ext_ref[pl.ds(7, rows), :]
    _keep_halo(ext_ref, rows)
    mixed = cur + (prev - cur) * _par(par_ref, R_RW_MU, PAR_W)
    r = mixed[:, 0:512]
    k = mixed[:, 512:1024]
    v = mixed[:, 1024:1536]
    lora = mixed[:, 1536:1792]

    log_d = -math.exp(-0.5) * _sigmoid(_par(par_ref, R_RW_W0, MIX_W) + _dot1(jnp.tanh(lora), lmat_ref[0:256, :]))
    a = _sigmoid(_par(par_ref, R_RW_A0, MIX_W) + _dot1(lora, lmat_ref[256:512, :]))
    gate = _dot1(_sigmoid(lora), lmat_ref[512:768, :])
    hblk = hblk_ref[...]

    g_in = _dot_sel(btri_ref[...], log_d)
    g_tot = _rows_bcast(g_in, nch)
    e_in = jnp.exp(g_in)
    e_neg = jnp.exp(-g_in)
    e_ex = jnp.exp(g_in - log_d)
    e_tail = jnp.exp(g_tot - g_in)
    e_end = jnp.exp(g_tot)

    kk = k * _par(par_ref, R_RW_KK, MIX_W)
    kk = kk * lax.rsqrt(_sum_bcast(kk * kk, hblk) + EPS)
    k_mod = k * (1.0 + (a - 1.0) * _par(par_ref, R_RW_KA, MIX_W))
    a_vec = -(a * kk)
    r_t = r * e_in
    b_t = kk * e_ex
    k_t = k_mod * e_neg
    a_t = a_vec * e_neg
    k_c = k_mod * e_tail
    a_c = a_vec * e_tail

    pair = _Pair()
    npair = RW_HEADS // 2
    combos = [(c, j) for c in range(nch) for j in range(npair)]
    cut = lambda arr, cj: arr[cj[0] * CHUNK:(cj[0] + 1) * CHUNK, cj[1] * 2 * RW_N:(cj[1] + 1) * 2 * RW_N]
    lhs = {cj: jnp.concatenate([cut(b_t, cj), cut(r_t, cj)], axis=0) for cj in combos}
    x1 = {cj: _dot1(lhs[cj], pair.bd(cut(k_t, cj)), NT) for cj in combos}
    x2 = {cj: _dot1(lhs[cj], pair.bd(cut(a_t, cj)), NT) for cj in combos}
    a_ra = {cj: jnp.where(pair.causal, x2[cj][CHUNK:], 0.0) for cj in combos}
    tinv = dict(zip(combos, pair.inverse([-jnp.where(pair.strict, x2[cj][:CHUNK], 0.0) for cj in combos])))
    av = {cj: _dot1(jnp.concatenate([jnp.where(pair.strict, x1[cj][:CHUNK], 0.0),
                                     jnp.where(pair.causal, x1[cj][CHUNK:], 0.0)], axis=0),
                    pair.bd(cut(v, cj))) for cj in combos}
    tz = {cj: _dot1(tinv[cj], jnp.concatenate([pair.bd(cut(b_t, cj)), pair.bd(av[cj][:CHUNK])], axis=1))
          for cj in combos}
    az = {cj: _dot1(a_ra[cj], jnp.concatenate([pair.bd(tz[cj][:, :2 * RW_N]), pair.bd(tz[cj][:, 2 * RW_N:])],
                                              axis=1)) for cj in combos}
    r_eff = {cj: cut(r_t, cj) + az[cj][:, :2 * RW_N] for cj in combos}
    y0 = {cj: av[cj][CHUNK:] + az[cj][:, 2 * RW_N:] for cj in combos}
    p_low = {cj: jnp.where(pair.same_block, _dot1(tz[cj][:, :2 * RW_N], cut(a_c, cj), TN), 0.0) for cj in combos}
    q_mat = {cj: jnp.where(pair.same_block,
                           _dot1(jnp.concatenate([cut(v, cj), tz[cj][:, 2 * RW_N:]], axis=0),
                                 jnp.concatenate([cut(k_c, cj), cut(a_c, cj)], axis=0), TN), 0.0)
             for cj in combos}
    y_rows = []
    for c in range(nch):
        s0 = [s_ref[j] if c == 0 else s_new[j] for j in range(npair)]
        y_rows.append(jnp.concatenate(
            [_dot1(r_eff[c, j], s0[j], NT) + y0[c, j] for j in range(npair)], axis=1))
        s_new = [s0[j] * e_end[c * CHUNK:c * CHUNK + 1, j * 2 * RW_N:(j + 1) * 2 * RW_N]
                 + _dot1(s0[j], p_low[c, j]) + q_mat[c, j] for j in range(npair)]
    for j in range(npair):
        s_ref[j] = s_new[j]
    y = jnp.concatenate(y_rows, axis=0)

    inv_n = 1.0 / RW_N
    mu = _sum_bcast(y, hblk, pieces=2) * inv_n
    yc = y - mu
    var = _sum_bcast(yc * yc, hblk) * inv_n
    y = yc * lax.rsqrt(var + RWKV_LN_EPS) * _par(par_ref, R_RW_LNW, MIX_W) + _par(par_ref, R_RW_LNB, MIX_W)
    y = y + _sum_bcast(r * k_mod * _par(par_ref, R_RW_RK, MIX_W), hblk) * v
    o_ref[...] = y * gate


def _rwkv(proj, par, lmat, l, nch):
    t = proj.shape[0]
    rows = nch * CHUNK
    return pl.pallas_call(
        functools.partial(_rwkv_kernel, nch=nch),
        out_shape=jax.ShapeDtypeStruct((t, MIX_W), F32),
        grid=(t // rows,),
        in_specs=[pl.BlockSpec((rows, 1536), lambda i: (i, 1)),
                  pl.BlockSpec((rows, 256), lambda i: (i, 27)),
                  _layer_spec((PAR_ROWS, PAR_W), l), _layer_spec((768, MIX_W), l),
                  _const_spec((rows, rows)), _const_spec((512, 512))],
        out_specs=pl.BlockSpec((rows, MIX_W), lambda i: (i, 0)),
        scratch_shapes=[pltpu.VMEM((rows + 8, 1792), F32),
                        pltpu.VMEM((RW_HEADS // 2, 2 * RW_N, 2 * RW_N), F32)],
        compiler_params=_cparams(1),
    )(proj, proj, par, lmat,
      _block_diag_ones(rows, CHUNK, lower=True, dtype=BF16), _block_diag_ones(MIX_W, RW_N, dtype=BF16))


def _layout_w_in(w):
    cols = [jnp.pad(w[:, :, s:s + n], ((0, 0), (0, 0), (0, p - n))) for s, n, p in _SRC_PIECES]
    return jnp.concatenate(cols, axis=2).astype(BF16)


def _param_slab(p):
    depth = p['w_in'].shape[0]

    def rows(a, lane=0):
        a = a.astype(F32).reshape(depth, -1, a.shape[-1])
        return jnp.pad(a, ((0, 0), (0, 0), (lane, PAR_W - lane - a.shape[-1])))

    pieces = [rows(p['gdn_conv_w']), rows(p['gdn_a_log'], 4), rows(p['gdn_dt_bias'], 4),
              rows(p['gdn_norm_w']), rows(p['ret_norm_w']),
              rows(p['m2_conv_w']), rows(p['m2_conv_b']), rows(p['m2_a_log']), rows(p['m2_dt_bias']),
              rows(jnp.repeat(p['m2_d'], M2_HEADDIM, axis=1)), rows(p['m2_norm_w']),
              rows(p['rw_mu']), rows(p['rw_w0']), rows(p['rw_a0']), rows(p['rw_k_k']), rows(p['rw_k_a']),
              rows(p['rw_r_k'].reshape(depth, -1)), rows(p['rw_ln_w']), rows(p['rw_ln_b'])]
    slab = jnp.concatenate(pieces, axis=1)
    return jnp.pad(slab, ((0, 0), (0, PAR_ROWS - slab.shape[1]), (0, 0)))


def _lora_mats(p):
    def at(m, offset):
        return jnp.pad(m, ((0, 0), (offset, 256 - offset - m.shape[1]), (0, 0)))
    return jnp.concatenate([at(p['rw_w_up'], 0), at(p['rw_a_up'], RW_W_LORA),
                            at(p['rw_g_up'], RW_W_LORA + RW_A_LORA)], axis=1).astype(BF16)


def _rotary_tables(t):
    theta = 1.0 / (ROPE_BASE ** jnp.linspace(0.0, 1.0, RET_DK // 2, dtype=F32))
    ang = jnp.arange(t, dtype=F32)[:, None] * theta
    cos = jnp.repeat(jnp.cos(ang), 2, axis=1)
    sin = jnp.stack([-jnp.sin(ang), jnp.sin(ang)], axis=-1).reshape(t, RET_DK)
    return jnp.tile(cos, (1, RET_HEADS)), jnp.tile(sin, (1, RET_HEADS))


NCH_GDN, NCH_RET, NCH_SSD, NCH_RWKV = 4, 4, 4, 4


def _prepare(p, t):
    cos, sin = _rotary_tables(t)
    return dict(norm1=p['norm1_w'].astype(F32)[:, None, :], w_in=_layout_w_in(p['w_in']),
                par=_param_slab(p), lmat=_lora_mats(p), w_out=p['w_out'].astype(BF16), cos=cos, sin=sin)


def _token_mix(h, l, q):
    t = h.shape[0]
    nch = lambda n: min(n, t // CHUNK)
    proj = _norm_matmul(h, q['norm1'], q['w_in'], l)
    o_a = _gdn(proj, q['par'], l, nch(NCH_GDN))
    o_b = _retention(proj, q['cos'], q['sin'], q['par'], l, nch(NCH_RET))
    o_c = _ssd(proj, q['par'], l, nch(NCH_SSD))
    o_d = _rwkv(proj, q['par'], q['lmat'], l, nch(NCH_RWKV))
    return _out_proj(h, (o_a, o_b, o_c, o_d), q['w_out'], l)


def kernel(x, norm1_w, w_in, gdn_conv_w, gdn_a_log, gdn_dt_bias, gdn_norm_w, ret_norm_w, m2_conv_w, m2_conv_b, m2_a_log, m2_dt_bias, m2_d, m2_norm_w, rw_mu, rw_w0, rw_w_up, rw_a0, rw_a_up, rw_g_up, rw_k_k, rw_k_a, rw_r_k, rw_ln_w, rw_ln_b, w_out, norm2_w, w_ffn_up, w_ffn_down, final_norm_w):
    p = dict(norm1_w=norm1_w, w_in=w_in, gdn_conv_w=gdn_conv_w, gdn_a_log=gdn_a_log,
             gdn_dt_bias=gdn_dt_bias, gdn_norm_w=gdn_norm_w, ret_norm_w=ret_norm_w,
             m2_conv_w=m2_conv_w, m2_conv_b=m2_conv_b, m2_a_log=m2_a_log, m2_dt_bias=m2_dt_bias,
             m2_d=m2_d, m2_norm_w=m2_norm_w, rw_mu=rw_mu, rw_w0=rw_w0, rw_w_up=rw_w_up,
             rw_a0=rw_a0, rw_a_up=rw_a_up, rw_g_up=rw_g_up, rw_k_k=rw_k_k, rw_k_a=rw_k_a,
             rw_r_k=rw_r_k, rw_ln_w=rw_ln_w, rw_ln_b=rw_ln_b, w_out=w_out)
    bsz, t, d = x.shape
    depth = w_in.shape[0]
    q = _prepare(p, t)
    norm2 = norm2_w.astype(F32)[:, None, :]
    w_up = w_ffn_up.astype(BF16)
    w_down = w_ffn_down.astype(BF16)
    final_w = final_norm_w.astype(F32).reshape(1, d)
    outs = []
    for b in range(bsz):
        h = x[b]
        for l in range(depth):
            h = _token_mix(h, l, q)
            h = _ffn(h, norm2, w_up, w_down, final_w, l, final_norm=(l == depth - 1))
        outs.append(h)
    return outs[0].reshape(1, t, d) if bsz == 1 else jnp.stack(outs, axis=0)
```
